```python
import math
import jax, jax.numpy as jnp
from jax import lax
import numpy as np

D_MODEL = 4096
BATCH = 4
SEQ = 4096
DEPTH = 1
DEC_BATCH = 8
DEC_SEQ = 2048
PAST_LEN = 128

F32 = jnp.float32
NEG_INF = -1e30
EPS = 1e-6

MIX_WIDTH = D_MODEL
DIFF_HEAD_DIM = 64
DIFF_V_DIM = 2 * DIFF_HEAD_DIM
DIFF_WIDTH = MIX_WIDTH // 2
DIFF_HEADS = DIFF_WIDTH // DIFF_V_DIM
DIL_HEAD_DIM = 128
DIL_WIDTH = MIX_WIDTH - DIFF_WIDTH
DIL_HEADS = DIL_WIDTH // DIL_HEAD_DIM
DIL_CONFIGS = ((128, 1), (512, 4), (2048, 16))
IN_WIDTH = 3 * DIFF_WIDTH + 3 * DIL_WIDTH
Q_BLOCK = 128

N_KEYS = 128
N_EXPERTS = N_KEYS * N_KEYS
PEER_HEADS = 8
PEER_TOPK = 16
PEER_QDIM = 256
PEER_SUBKEY_DIM = PEER_QDIM // 2
TOKEN_BLOCK = 128

kernel_name = "hymba_diffattn_dilated_peer_encoder"


def rms_norm(x, gain):
    xf = x.astype(F32)
    y = xf * lax.rsqrt(jnp.mean(xf * xf, axis=-1, keepdims=True) + EPS)
    return (y * gain.astype(F32)).astype(x.dtype)


def alibi_slopes(n):
    return jnp.asarray(2.0 ** (-8.0 * np.arange(1, n + 1) / n), dtype=F32)


def diff_attention(q, k, v, lam, subln_gain, lambda_init):
    B, H, T = q.shape[:3]
    nblk = T // Q_BLOCK
    scale = DIFF_HEAD_DIM ** -0.5
    slopes = alibi_slopes(DIFF_HEADS)
    kpos = jnp.arange(T)
    qb = q.reshape(B, H, nblk, Q_BLOCK, 2, DIFF_HEAD_DIM).transpose(2, 0, 1, 3, 4, 5)

    def block(args):
        qblk, i = args
        qpos = i * Q_BLOCK + jnp.arange(Q_BLOCK)
        bias = -slopes[:, None, None] * jnp.abs(qpos[:, None] - kpos[None, :]).astype(F32)
        s = jnp.einsum('bhqcd,bhkcd->bhcqk', qblk, k, preferred_element_type=F32) * scale + bias[None, :, None]
        p = jax.nn.softmax(s, axis=-1)
        w = p[:, :, 0] - lam.astype(F32) * p[:, :, 1]
        return jnp.einsum('bhqk,bhkd->bhqd', w.astype(v.dtype), v)

    o = lax.map(block, (qb, jnp.arange(nblk)))
    o = o.transpose(1, 2, 0, 3, 4).reshape(B, H, T, DIFF_V_DIM)
    return rms_norm(o, subln_gain) * (1.0 - lambda_init)


def dilated_branch(q, k, v, slopes, dilation, radius):
    B, H, T, dh = q.shape
    L = T // dilation
    nb = -(-L // radius)
    Lp = nb * radius
    scale = dh ** -0.5

    def to_res(a):
        return a.reshape(B, H, L, dilation, dh).transpose(0, 1, 3, 2, 4)

    def slabs(a):
        a = jnp.pad(a, ((0, 0), (0, 0), (0, 0), (radius, Lp - L + radius), (0, 0)))
        a = a.reshape(B, H, dilation, nb + 2, radius, dh)
        return jnp.concatenate([a[:, :, :, :-2], a[:, :, :, 1:-1], a[:, :, :, 2:]], axis=4)

    qr = jnp.pad(to_res(q), ((0, 0), (0, 0), (0, 0), (0, Lp - L), (0, 0)))
    qb = qr.reshape(B, H, dilation, nb, radius, dh)
    ks = slabs(to_res(k))
    vs = slabs(to_res(v))
    qi = jnp.arange(radius)
    kj = jnp.arange(3 * radius)
    rel = (kj[None, :] - radius) - qi[:, None]
    key_l = (jnp.arange(nb) * radius)[:, None] - radius + kj[None, :]
    valid = (jnp.abs(rel) <= radius)[None] & ((key_l >= 0) & (key_l < L))[:, None, :]
    dist = (jnp.abs(rel) * dilation).astype(F32)
    bias = -slopes[:, None, None, None, None] * dist[None, None, None]
    s = jnp.einsum('bhgnqd,bhgnkd->bhgnqk', qb, ks, preferred_element_type=F32) * scale + bias[None]
    s = jnp.where(valid, s, NEG_INF)
    m = jnp.max(s, axis=-1, keepdims=True)
    e = jnp.exp(s - m)
    den = jnp.sum(e, axis=-1, keepdims=True)
    o = jnp.einsum('bhgnqk,bhgnkd->bhgnqd', e.astype(vs.dtype), vs, preferred_element_type=F32) / den
    lse = (m + jnp.log(den))[..., 0]
    o = o.reshape(B, H, dilation, Lp, dh)[:, :, :, :L].transpose(0, 1, 3, 2, 4).reshape(B, H, T, dh)
    lse = lse.reshape(B, H, dilation, Lp)[:, :, :, :L].transpose(0, 1, 3, 2).reshape(B, H, T)
    return o, lse


def dilated_attention(q, k, v):
    slopes = alibi_slopes(DIL_HEADS)
    outs, lses = [], []
    for window, dilation in DIL_CONFIGS:
        o, lse = dilated_branch(q, k, v, slopes, dilation, window // (2 * dilation))
        outs.append(o)
        lses.append(lse)
    alpha = jax.nn.softmax(jnp.stack(lses), axis=0)
    o = jnp.einsum('nbht,nbhtd->bhtd', alpha, jnp.stack(outs))
    return o.astype(q.dtype)


def peer(h, w_query, sub_keys, expert_u, expert_v):
    B, T, D = h.shape
    xt = h.reshape((B * T) // TOKEN_BLOCK, TOKEN_BLOCK, D)

    def block(xb):
        q = jnp.einsum('td,de->te', xb, w_query).reshape(TOKEN_BLOCK, PEER_HEADS, 2, PEER_SUBKEY_DIM)
        s = jnp.einsum('thcd,hckd->thck', q, sub_keys, preferred_element_type=F32)
        sv, si = lax.top_k(s, PEER_TOPK)
        cand = (sv[:, :, 0, :, None] + sv[:, :, 1, None, :]).reshape(TOKEN_BLOCK, PEER_HEADS, PEER_TOPK * PEER_TOPK)
        cidx = (si[:, :, 0, :, None] * N_KEYS + si[:, :, 1, None, :]).reshape(TOKEN_BLOCK, PEER_HEADS, PEER_TOPK * PEER_TOPK)
        fv, fi = lax.top_k(cand, PEER_TOPK)
        eidx = jnp.take_along_axis(cidx, fi, axis=-1)
        g = jax.nn.softmax(fv, axis=-1)
        u = jnp.take(expert_u, eidx, axis=0)
        a = jax.nn.gelu(jnp.einsum('thkd,td->thk', u, xb, preferred_element_type=F32), approximate=False)
        ve = jnp.take(expert_v, eidx, axis=0)
        return jnp.einsum('thk,thkd->td', (g * a).astype(ve.dtype), ve)

    return lax.map(block, xt).reshape(B, T, D)


def trunk(x, c, norm1_gain, norm2_gain, w_ada, b_ada, w_in, diff_lambda, diff_subln_gain,
          w_out, peer_w_query, peer_sub_keys, peer_u, peer_v, final_gain):
    B, T, _ = x.shape
    for l in range(DEPTH):
        lambda_init = 0.8 - 0.6 * math.exp(-0.3 * l)
        ada = jnp.einsum('bd,de->be', jax.nn.silu(c), w_ada[l]) + b_ada[l]
        sh1, sc1, g1, sh2, sc2, g2 = jnp.split(ada, 6, axis=-1)
        h = rms_norm(x, norm1_gain[l]) * (1 + sc1[:, None]) + sh1[:, None]
        proj = jnp.einsum('btd,de->bte', h, w_in[l])
        cuts = [DIFF_WIDTH, 2 * DIFF_WIDTH, 3 * DIFF_WIDTH, 3 * DIFF_WIDTH + DIL_WIDTH, 3 * DIFF_WIDTH + 2 * DIL_WIDTH]
        qa, ka, va, qd, kd, vd = jnp.split(proj, cuts, axis=-1)
        qa = qa.reshape(B, T, DIFF_HEADS, 2, DIFF_HEAD_DIM).transpose(0, 2, 1, 3, 4)
        ka = ka.reshape(B, T, DIFF_HEADS, 2, DIFF_HEAD_DIM).transpose(0, 2, 1, 3, 4)
        va = va.reshape(B, T, DIFF_HEADS, DIFF_V_DIM).transpose(0, 2, 1, 3)
        lp = diff_lambda[l]
        lam = jnp.exp(jnp.sum(lp[0] * lp[1])) - jnp.exp(jnp.sum(lp[2] * lp[3])) + lambda_init
        oa = diff_attention(qa, ka, va, lam, diff_subln_gain[l], lambda_init)
        to_heads = lambda a: a.reshape(B, T, DIL_HEADS, DIL_HEAD_DIM).transpose(0, 2, 1, 3)
        od = dilated_attention(to_heads(qd), to_heads(kd), to_heads(vd))
        mixed = jnp.concatenate([
            oa.transpose(0, 2, 1, 3).reshape(B, T, DIFF_WIDTH).astype(x.dtype),
            od.transpose(0, 2, 1, 3).reshape(B, T, DIL_WIDTH).astype(x.dtype)], axis=-1)
        x = x + g1[:, None] * jnp.einsum('bte,ed->btd', mixed, w_out[l])
        h = rms_norm(x, norm2_gain[l]) * (1 + sc2[:, None]) + sh2[:, None]
        x = x + g2[:, None] * peer(h, peer_w_query[l], peer_sub_keys[l], peer_u[l], peer_v[l]).astype(x.dtype)
    return rms_norm(x, final_gain)


def setup_inputs(seed: int = 0) -> dict:
    key = jax.random.key(seed)
    ks = jax.random.split(key, 20)
    D = D_MODEL
    nrm = lambda k, shape, s: s * jax.random.normal(k, shape, F32)
    return {
        "x_prompt": nrm(ks[0], (BATCH, SEQ, D), 1.0),
        "x_sample": nrm(ks[1], (DEC_BATCH, DEC_SEQ, D), 1.0),
        "c_prompt": nrm(ks[2], (BATCH, D), 1.0),
        "c_sample": nrm(ks[3], (DEC_BATCH, D), 1.0),
        "norm1_gain": 1.0 + nrm(ks[4], (DEPTH, D), 0.02),
        "norm2_gain": 1.0 + nrm(ks[5], (DEPTH, D), 0.02),
        "w_ada": nrm(ks[6], (DEPTH, D, 6 * D), 0.2 * D ** -0.5),
        "b_ada": nrm(ks[7], (DEPTH, 6 * D), 0.01),
        "w_in": nrm(ks[8], (DEPTH, D, IN_WIDTH), D ** -0.5),
        "diff_lambda": nrm(ks[9], (DEPTH, 4, DIFF_HEAD_DIM), 0.1),
        "diff_subln_gain": 1.0 + nrm(ks[10], (DEPTH, DIFF_V_DIM), 0.02),
        "w_out": nrm(ks[11], (DEPTH, MIX_WIDTH, D), MIX_WIDTH ** -0.5),
        "peer_w_query": nrm(ks[12], (DEPTH, D, PEER_HEADS * PEER_QDIM), D ** -0.5),
        "peer_sub_keys": nrm(ks[13], (DEPTH, PEER_HEADS, 2, N_KEYS, PEER_SUBKEY_DIM), PEER_SUBKEY_DIM ** -0.5),
        "peer_u": nrm(ks[14], (DEPTH, N_EXPERTS, D), D ** -0.5),
        "peer_v": nrm(ks[15], (DEPTH, N_EXPERTS, D), 1.0),
        "final_gain": 1.0 + nrm(ks[16], (D,), 0.02),
    }


def reference(x_prompt, x_sample, c_prompt, c_sample, norm1_gain, norm2_gain, w_ada, b_ada, w_in,
              diff_lambda, diff_subln_gain, w_out, peer_w_query, peer_sub_keys, peer_u, peer_v, final_gain):
    y_prompt = trunk(x_prompt, c_prompt, norm1_gain, norm2_gain, w_ada, b_ada, w_in, diff_lambda,
                     diff_subln_gain, w_out, peer_w_query, peer_sub_keys, peer_u, peer_v, final_gain)
    y_sample = trunk(x_sample, c_sample, norm1_gain, norm2_gain, w_ada, b_ada, w_in, diff_lambda,
                     diff_subln_gain, w_out, peer_w_query, peer_sub_keys, peer_u, peer_v, final_gain)
    return (y_prompt, y_sample)
```

```python
import functools
import math

import numpy as np
import jax
import jax.numpy as jnp
from jax import lax
from jax.experimental import pallas as pl
from jax.experimental.pallas import tpu as pltpu

F32 = jnp.float32
BF16 = jnp.bfloat16
EPS = 1e-6
MASK_VALUE = -1e30
LANES = 128
VMEM_CAP_BYTES = 60000 * 1024
DIFF_HALF_DIM = 64
DIL_CONFIGS = ((128, 1), (512, 4), (2048, 16))
PEER_TOPK = 16
NT_DIMS = (((1,), (1,)), ((), ()))


def _params(semantics, vmem_bytes):
    return pltpu.CompilerParams(
        dimension_semantics=semantics,
        vmem_limit_bytes=int(min(max(vmem_bytes, 16 * 2**20), VMEM_CAP_BYTES)))


def _alibi_slopes(n):
    return 2.0 ** (-8.0 * np.arange(1, n + 1) / n)


def _ada_kernel(c_ref, w_ref, b_ref, o_ref):
    c = c_ref[...]
    a = (c / (1.0 + jnp.exp(-c))).astype(BF16)
    o_ref[...] = jnp.dot(a, w_ref[...].astype(BF16), preferred_element_type=F32) + b_ref[...]


def _ada(c_all, w_ada, b_ada):
    rows, d = c_all.shape
    n = w_ada.shape[1]
    tn = 512
    return pl.pallas_call(
        _ada_kernel,
        grid=(n // tn,),
        in_specs=[pl.BlockSpec((rows, d), lambda j: (0, 0)),
                  pl.BlockSpec((d, tn), lambda j: (0, j)),
                  pl.BlockSpec((1, tn), lambda j: (0, j))],
        out_specs=pl.BlockSpec((rows, tn), lambda j: (0, j)),
        out_shape=jax.ShapeDtypeStruct((rows, n), F32),
        compiler_params=_params(("arbitrary",), 2 * d * tn * 4 + d * tn * 2 + 4 * 2**20),
        name="ada",
    )(c_all, w_ada, b_ada.reshape(1, n))


def _norm_mod_kernel(x_ref, gain_ref, sc_ref, sh_ref, o_ref, *, transpose):
    x = x_ref[0]
    ms = jnp.mean(x * x, axis=-1, keepdims=True)
    y = x * lax.rsqrt(ms + EPS) * gain_ref[...]
    h = y * (1.0 + sc_ref[0]) + sh_ref[0]
    if transpose:
        o_ref[...] = h.T.astype(o_ref.dtype)
    else:
        o_ref[...] = h.astype(o_ref.dtype)


def _norm_mod(x, gain, sc, sh, *, transpose, tm=256):
    b, t, d = x.shape
    m = b * t
    tm = min(tm, t)
    nt = t // tm
    if transpose:
        out_spec = pl.BlockSpec((d, tm), lambda bi, i: (0, bi * nt + i))
        out_shape = jax.ShapeDtypeStruct((d, m), BF16)
    else:
        out_spec = pl.BlockSpec((tm, d), lambda bi, i: (bi * nt + i, 0))
        out_shape = jax.ShapeDtypeStruct((m, d), BF16)
    return pl.pallas_call(
        functools.partial(_norm_mod_kernel, transpose=transpose),
        grid=(b, nt),
        in_specs=[pl.BlockSpec((1, tm, d), lambda bi, i: (bi, i, 0)),
                  pl.BlockSpec((1, d), lambda bi, i: (0, 0)),
                  pl.BlockSpec((1, 1, d), lambda bi, i: (bi, 0, 0)),
                  pl.BlockSpec((1, 1, d), lambda bi, i: (bi, 0, 0))],
        out_specs=out_spec,
        out_shape=out_shape,
        compiler_params=_params(("parallel", "parallel"), 6 * tm * d * 4 + 4 * 2**20),
        name="norm_mod_t" if transpose else "norm_mod",
    )(x, gain.reshape(1, d), sc, sh)


def _inproj_kernel(a_ref, w_ref, cs_ref, o_ref):
    acc = jnp.dot(a_ref[...], w_ref[...], preferred_element_type=F32) * cs_ref[...]
    for c in range(o_ref.shape[0]):
        o_ref[c] = acc[:, c * LANES:(c + 1) * LANES].astype(o_ref.dtype)


def _inproj(h, w, colscale, *, tm=1024, tn=512):
    m, k = h.shape
    n = w.shape[1]
    tm, tn = min(tm, m), min(tn, n)
    nb = tn // LANES
    return pl.pallas_call(
        _inproj_kernel,
        grid=(m // tm, n // tn),
        in_specs=[pl.BlockSpec((tm, k), lambda i, j: (i, 0)),
                  pl.BlockSpec((k, tn), lambda i, j: (0, j)),
                  pl.BlockSpec((1, tn), lambda i, j: (0, j))],
        out_specs=pl.BlockSpec((nb, tm, LANES), lambda i, j: (j, i, 0)),
        out_shape=jax.ShapeDtypeStruct((n // LANES, m, LANES), BF16),
        compiler_params=_params(("parallel", "arbitrary"),
                                2 * (tm * k + k * tn + tm * tn) * 2 + 2 * tm * tn * 4 + 2 * 2**20),
        name="inproj",
    )(h, w, colscale)


def _diff_attn_kernel(q_ref, k_ref, v_ref, ns_ref, lp_ref, gain_ref, o_ref, m_sc, l_sc, acc_sc,
                      *, tq, tk, lambda_init):
    qi = pl.program_id(2)
    t = k_ref.shape[1]
    q = q_ref[0]
    lane = lax.broadcasted_iota(jnp.int32, q.shape, 1)
    zero = jnp.zeros_like(q)
    qz = jnp.concatenate([jnp.where(lane < DIFF_HALF_DIM, q, zero),
                          jnp.where(lane >= DIFF_HALF_DIM, q, zero)], axis=0)
    nslope = ns_ref[0][:, :1]
    rel = (lax.broadcasted_iota(jnp.int32, (tq, tk), 0)
           - lax.broadcasted_iota(jnp.int32, (tq, tk), 1)).astype(F32)
    m_sc[...] = jnp.full(m_sc.shape, MASK_VALUE, F32)
    l_sc[...] = jnp.zeros(l_sc.shape, F32)
    acc_sc[...] = jnp.zeros(acc_sc.shape, F32)

    def body(j, carry):
        k = k_ref[0, pl.ds(pl.multiple_of(j * tk, tk), tk), :]
        v = v_ref[0, pl.ds(pl.multiple_of(j * tk, tk), tk), :]
        s = lax.dot_general(qz, k, NT_DIMS, preferred_element_type=F32)
        bias = nslope * jnp.abs(rel + (qi * tq - j * tk).astype(F32))
        s = (s.reshape(2, tq, tk) + bias[None]).reshape(2 * tq, tk)
        m_old = m_sc[...]
        m_new = jnp.maximum(m_old, jnp.max(s, axis=1, keepdims=True))
        alpha = jnp.exp(m_old - m_new)
        p = jnp.exp(s - m_new)
        l_sc[...] = alpha * l_sc[...] + jnp.sum(p, axis=1, keepdims=True)
        acc_sc[...] = alpha * acc_sc[...] + jnp.dot(p.astype(BF16), v, preferred_element_type=F32)
        m_sc[...] = m_new
        return carry

    lax.fori_loop(0, t // tk, body, 0)

    lp = lp_ref[...]
    lam = (jnp.exp(jnp.sum(lp[0:1] * lp[1:2], axis=1, keepdims=True))
           - jnp.exp(jnp.sum(lp[2:3] * lp[3:4], axis=1, keepdims=True)) + lambda_init)
    on = acc_sc[...] / l_sc[...]
    o = on[:tq] - lam * on[tq:]
    o = o * lax.rsqrt(jnp.mean(o * o, axis=-1, keepdims=True) + EPS) * gain_ref[...]
    o_ref[0] = (o * (1.0 - lambda_init)).astype(o_ref.dtype)


def _diff_attention(p, nslopes, lam_params, subln_gain, *, b, t, nh, lambda_init, tq=512, tk=512):
    tq, tk = min(tq, t), min(tk, t)
    nq = t // tq
    return pl.pallas_call(
        functools.partial(_diff_attn_kernel, tq=tq, tk=tk, lambda_init=lambda_init),
        grid=(b, nh, nq),
        in_specs=[pl.BlockSpec((1, tq, LANES), lambda bi, h, qi: (h, bi * nq + qi, 0)),
                  pl.BlockSpec((1, t, LANES), lambda bi, h, qi: (nh + h, bi, 0)),
                  pl.BlockSpec((1, t, LANES), lambda bi, h, qi: (2 * nh + h, bi, 0)),
                  pl.BlockSpec((1, 1, LANES), lambda bi, h, qi: (h, 0, 0)),
                  pl.BlockSpec(lam_params.shape, lambda bi, h, qi: (0, 0)),
                  pl.BlockSpec((1, LANES), lambda bi, h, qi: (0, 0))],
        out_specs=pl.BlockSpec((1, tq, LANES), lambda bi, h, qi: (bi, qi, h)),
        out_shape=jax.ShapeDtypeStruct((b, t, nh * LANES), BF16),
        scratch_shapes=[pltpu.VMEM((2 * tq, 1), F32), pltpu.VMEM((2 * tq, 1), F32),
                        pltpu.VMEM((2 * tq, LANES), F32)],
        compiler_params=_params(("parallel", "parallel", "arbitrary"),
                                4 * t * LANES * 2 + 6 * 2 * tq * tk * 4 + 3 * 2 * tq * LANES * 4 + 4 * 2**20),
        name="diff_attn",
    )(p, p, p, nslopes, lam_params, subln_gain.reshape(1, LANES))


def _dil_attn_kernel(q_ref, k_ref, v_ref, ns_ref, o_ref, qf, kf, vf, qr, kr, vr, ob, lb, *, blk):
    t = q_ref.shape[1]
    nslope = ns_ref[0][:, :1]
    qf[...] = q_ref[0].astype(F32)
    kf[...] = k_ref[0].astype(F32)
    vf[...] = v_ref[0].astype(F32)

    for n, (window, g) in enumerate(DIL_CONFIGS):
        radius = window // (2 * g)
        ln = t // g
        wk = min(blk + 2 * radius, ln)
        nblk = ln // blk
        for rho in range(g):
            rows = pl.ds(rho, ln, stride=g) if g > 1 else pl.ds(0, ln)
            qr[rho * ln:(rho + 1) * ln, :] = qf[rows, :].astype(BF16)
            kr[rho * ln:(rho + 1) * ln, :] = kf[rows, :].astype(BF16)
            vr[rho * ln:(rho + 1) * ln, :] = vf[rows, :].astype(BF16)
        rel0 = (lax.broadcasted_iota(jnp.int32, (blk, wk), 1)
                - lax.broadcasted_iota(jnp.int32, (blk, wk), 0))

        def body(i, carry, n=n, g=g, radius=radius, ln=ln, wk=wk, nblk=nblk, rel0=rel0):
            rho = i // nblk
            l0 = (i % nblk) * blk
            ws = jnp.clip(l0 - radius, 0, ln - wk)
            base = rho * ln
            qb = qr[pl.ds(pl.multiple_of(base + l0, blk), blk), :]
            kw = kr[pl.ds(pl.multiple_of(base + ws, radius), wk), :]
            vw = vr[pl.ds(pl.multiple_of(base + ws, radius), wk), :]
            s = lax.dot_general(qb, kw, NT_DIMS, preferred_element_type=F32)
            ar = jnp.abs(rel0 + (ws - l0))
            s = jnp.where(ar <= radius, s + (nslope * float(g)) * ar.astype(F32), MASK_VALUE)
            mx = jnp.max(s, axis=1, keepdims=True)
            e = jnp.exp(s - mx)
            den = jnp.sum(e, axis=1, keepdims=True)
            o = jnp.dot(e.astype(BF16), vw, preferred_element_type=F32) / den
            lse = mx + jnp.log(den)
            rows = pl.ds(l0 * g + rho, blk, stride=g) if g > 1 else pl.ds(pl.multiple_of(l0, blk), blk)
            ob[n, rows, :] = o
            lb[n, rows, :] = jnp.broadcast_to(lse, (blk, LANES))
            return carry

        lax.fori_loop(0, g * nblk, body, 0)

    chunk = min(512, t)

    def merge(i, carry):
        rows = pl.ds(pl.multiple_of(i * chunk, chunk), chunk)
        l0, l1, l2 = lb[0, rows, :], lb[1, rows, :], lb[2, rows, :]
        mx = jnp.maximum(jnp.maximum(l0, l1), l2)
        w0, w1, w2 = jnp.exp(l0 - mx), jnp.exp(l1 - mx), jnp.exp(l2 - mx)
        o = (w0 * ob[0, rows, :] + w1 * ob[1, rows, :] + w2 * ob[2, rows, :]) / (w0 + w1 + w2)
        o_ref[0, rows, :] = o.astype(o_ref.dtype)
        return carry

    lax.fori_loop(0, t // chunk, merge, 0)


def _dil_attention(p, nslopes, *, b, t, nh, blk=128):
    nbr = len(DIL_CONFIGS)
    return pl.pallas_call(
        functools.partial(_dil_attn_kernel, blk=blk),
        grid=(b, nh),
        in_specs=[pl.BlockSpec((1, t, LANES), lambda bi, h: (3 * nh + h, bi, 0)),
                  pl.BlockSpec((1, t, LANES), lambda bi, h: (4 * nh + h, bi, 0)),
                  pl.BlockSpec((1, t, LANES), lambda bi, h: (5 * nh + h, bi, 0)),
                  pl.BlockSpec((1, 1, LANES), lambda bi, h: (h, 0, 0))],
        out_specs=pl.BlockSpec((1, t, LANES), lambda bi, h: (bi, 0, h)),
        out_shape=jax.ShapeDtypeStruct((b, t, nh * LANES), BF16),
        scratch_shapes=[pltpu.VMEM((t, LANES), F32)] * 3 + [pltpu.VMEM((t, LANES), BF16)] * 3
                       + [pltpu.VMEM((nbr, t, LANES), F32)] * 2,
        compiler_params=_params(("parallel", "parallel"),
                                8 * t * LANES * 2 + 3 * t * LANES * 6 + 2 * nbr * t * LANES * 4 + 8 * 2**20),
        name="dil_attn",
    )(p, p, p, nslopes)


def _outproj_kernel(ma_ref, md_ref, wt_ref, wb_ref, x_ref, g_ref, o_ref):
    acc = (jnp.dot(ma_ref[...], wt_ref[...], preferred_element_type=F32)
           + jnp.dot(md_ref[...], wb_ref[...], preferred_element_type=F32))
    o_ref[...] = x_ref[...] + g_ref[0] * acc


def _outproj(ma, md, w_out, x2d, gate, *, t, tm=1024, tn=512):
    m, kh = ma.shape
    n = w_out.shape[1]
    tm, tn = min(tm, t), min(tn, n)
    return pl.pallas_call(
        _outproj_kernel,
        grid=(m // tm, n // tn),
        in_specs=[pl.BlockSpec((tm, kh), lambda i, j: (i, 0)),
                  pl.BlockSpec((tm, kh), lambda i, j: (i, 0)),
                  pl.BlockSpec((kh, tn), lambda i, j: (0, j)),
                  pl.BlockSpec((kh, tn), lambda i, j: (1, j)),
                  pl.BlockSpec((tm, tn), lambda i, j: (i, j)),
                  pl.BlockSpec((1, 1, tn), lambda i, j: ((i * tm) // t, 0, j))],
        out_specs=pl.BlockSpec((tm, tn), lambda i, j: (i, j)),
        out_shape=jax.ShapeDtypeStruct((m, n), F32),
        compiler_params=_params(("parallel", "arbitrary"),
                                2 * (2 * tm * kh + 2 * kh * tn) * 2 + 5 * tm * tn * 4 + 2 * 2**20),
        name="outproj",
    )(ma, md, w_out, w_out, x2d, gate)


def _take_top_rows(s, rows_ref, count):
    for r in range(count):
        mx = jnp.max(s, axis=0, keepdims=True)
        rows_ref[r:r + 1, :] = mx
        s = jnp.where(s >= mx, -jnp.inf, s)


def _router_kernel(h_ref, wq_ref, keys_ref, s0_ref, s1_ref, a_ref, b_ref, tau_ref, ra_ref, rb_ref, rc_ref):
    k = PEER_TOPK
    sub = keys_ref.shape[2]
    qt = jnp.dot(wq_ref[...], h_ref[...], preferred_element_type=F32)
    s0 = jnp.dot(keys_ref[0, 0], qt[:sub].astype(BF16), preferred_element_type=F32)
    s1 = jnp.dot(keys_ref[0, 1], qt[sub:].astype(BF16), preferred_element_type=F32)
    _take_top_rows(s0, ra_ref, k)
    _take_top_rows(s1, rb_ref, k)
    ra, rb = ra_ref[...], rb_ref[...]
    row8 = lax.broadcasted_iota(jnp.int32, (8, ra.shape[1]), 0)
    pieces = [ra[0:1] + rb, ra[1:2] + rb[0:8]]
    for r in range(2, 8):
        pieces.append(jnp.where(row8 < k // (r + 1), ra[r:r + 1] + rb[0:8], -jnp.inf))
    pieces.append(ra[8:16] + rb[0:1])
    cand = jnp.concatenate(pieces, axis=0)
    _take_top_rows(cand, rc_ref, k)
    tau = rc_ref[k - 1:k, :]
    top = ra[0:1] + rb[0:1]
    z = jnp.sum(jnp.where(cand >= tau, jnp.exp(cand - top), 0.0), axis=0, keepdims=True)
    s0_ref[0] = s0
    s1_ref[0] = s1
    a_ref[0] = jnp.exp(s0 - ra[0:1]) / z
    b_ref[0] = jnp.exp(s1 - rb[0:1])
    tau_ref[0] = tau


def _router(ht, wq_t, keys, *, tm=512):
    d, m = ht.shape
    heads, _, n_keys, sub = keys.shape
    tm = min(tm, m)
    big = jax.ShapeDtypeStruct((heads, n_keys, m), F32)
    big_spec = pl.BlockSpec((1, n_keys, tm), lambda i, h: (h, 0, i))
    return pl.pallas_call(
        _router_kernel,
        grid=(m // tm, heads),
        in_specs=[pl.BlockSpec((d, tm), lambda i, h: (0, i)),
                  pl.BlockSpec((2 * sub, d), lambda i, h: (h, 0)),
                  pl.BlockSpec((1, 2, n_keys, sub), lambda i, h: (h, 0, 0, 0))],
        out_specs=[big_spec, big_spec, big_spec, big_spec,
                   pl.BlockSpec((1, 1, tm), lambda i, h: (h, 0, i))],
        out_shape=[big, big, big, big, jax.ShapeDtypeStruct((heads, 1, m), F32)],
        scratch_shapes=[pltpu.VMEM((PEER_TOPK, tm), F32)] * 3,
        compiler_params=_params(("parallel", "arbitrary"),
                                2 * d * tm * 2 + 2 * 2 * sub * d * 2 + 8 * n_keys * tm * 4 + 16 * 2**20),
        name="peer_router",
    )(ht, wq_t, keys)


def _peer_dense_kernel(h_ref, u_ref, vt_ref, s0_ref, s1_ref, a_ref, b_ref, tau_ref, o_ref):
    e = pl.program_id(1)
    heads, n_keys, tm = s1_ref.shape
    te = u_ref.shape[0]
    act = jnp.dot(u_ref[...], h_ref[...], preferred_element_type=F32)
    gel = 0.5 * act * (1.0 + lax.erf(act * (1.0 / math.sqrt(2.0))))
    parts = []
    for ii in range(te // n_keys):
        i = e * (te // n_keys) + ii
        gates = jnp.zeros((n_keys, tm), F32)
        for h in range(heads):
            pair = s1_ref[h] + s0_ref[h, pl.ds(i, 1), :]
            gates = gates + jnp.where(pair >= tau_ref[h], b_ref[h], 0.0) * a_ref[h, pl.ds(i, 1), :]
        parts.append(gates)
    w = (jnp.concatenate(parts, axis=0) * gel).astype(BF16)
    contrib = jnp.dot(vt_ref[...], w, preferred_element_type=F32)

    @pl.when(e == 0)
    def _():
        o_ref[...] = contrib

    @pl.when(e > 0)
    def _():
        o_ref[...] += contrib


def _peer_dense(ht, u, vt, s0, s1, a, bf, tau, *, tm=512, te=512):
    d, m = ht.shape
    n_exp = u.shape[0]
    heads, n_keys, _ = s0.shape
    tm = min(tm, m)
    once = pl.Buffered(1)
    tok = lambda i, e: (0, 0, i)
    big_spec = pl.BlockSpec((heads, n_keys, tm), tok, pipeline_mode=once)
    return pl.pallas_call(
        _peer_dense_kernel,
        grid=(m // tm, n_exp // te),
        in_specs=[pl.BlockSpec((d, tm), lambda i, e: (0, i), pipeline_mode=once),
                  pl.BlockSpec((te, d), lambda i, e: (e, 0)),
                  pl.BlockSpec((d, te), lambda i, e: (0, e)),
                  big_spec, big_spec, big_spec, big_spec,
                  pl.BlockSpec((heads, 1, tm), tok, pipeline_mode=once)],
        out_specs=pl.BlockSpec((d, tm), lambda i, e: (0, i)),
        out_shape=jax.ShapeDtypeStruct((d, m), F32),
        compiler_params=_params(("parallel", "arbitrary"),
                                d * tm * 2 + 4 * te * d * 2 + 2 * d * tm * 4 + 4 * heads * n_keys * tm * 4
                                + 6 * te * tm * 4 + d * tm * 4),
        name="peer_dense",
    )(ht, u, vt, s0, s1, a, bf, tau)


def _final_kernel(x_ref, pt_ref, g_ref, gain_ref, o_ref):
    x = x_ref[...] + g_ref[0] * pt_ref[...].T
    ms = jnp.mean(x * x, axis=-1, keepdims=True)
    o_ref[...] = x * lax.rsqrt(ms + EPS) * gain_ref[...]


def _final(x1, pt, gate, gain, *, t, tm=256):
    m, d = x1.shape
    tm = min(tm, t)
    return pl.pallas_call(
        _final_kernel,
        grid=(m // tm,),
        in_specs=[pl.BlockSpec((tm, d), lambda i: (i, 0)),
                  pl.BlockSpec((d, tm), lambda i: (0, i)),
                  pl.BlockSpec((1, 1, d), lambda i: ((i * tm) // t, 0, 0)),
                  pl.BlockSpec((1, d), lambda i: (0, 0))],
        out_specs=pl.BlockSpec((tm, d), lambda i: (i, 0)),
        out_shape=jax.ShapeDtypeStruct((m, d), F32),
        compiler_params=_params(("parallel",), 8 * tm * d * 4 + 4 * 2**20),
        name="final_norm",
    )(x1, pt, gate, gain.reshape(1, d))


def _trunk(x, ada, w, depth_index=0):
    b, t, d = x.shape
    nh = d // (2 * LANES)
    lambda_init = 0.8 - 0.6 * math.exp(-0.3 * depth_index)
    sh1, sc1, g1, sh2, sc2, g2 = [a.reshape(b, 1, d) for a in jnp.split(ada, 6, axis=-1)]

    h = _norm_mod(x, w["norm1_gain"], sc1, sh1, transpose=False)
    p = _inproj(h, w["w_in"], w["colscale"])
    ma = _diff_attention(p, w["diff_nslopes"], w["diff_lambda"], w["diff_subln_gain"],
                         b=b, t=t, nh=nh, lambda_init=lambda_init)
    md = _dil_attention(p, w["dil_nslopes"], b=b, t=t, nh=nh)
    x1 = _outproj(ma.reshape(b * t, nh * LANES), md.reshape(b * t, nh * LANES), w["w_out"],
                  x.reshape(b * t, d), g1, t=t)

    ht = _norm_mod(x1.reshape(b, t, d), w["norm2_gain"], sc2, sh2, transpose=True)
    s0, s1, a, bf, tau = _router(ht, w["wq_t"], w["sub_keys"])
    pt = _peer_dense(ht, w["peer_u"], w["peer_vt"], s0, s1, a, bf, tau)
    y = _final(x1, pt, g2, w["final_gain"], t=t)
    return y.reshape(b, t, d)


def kernel(x_prompt, x_sample, c_prompt, c_sample, norm1_gain, norm2_gain, w_ada, b_ada, w_in, diff_lambda,
           diff_subln_gain, w_out, peer_w_query, peer_sub_keys, peer_u, peer_v, final_gain):
    d = x_prompt.shape[-1]
    nh = d // (2 * LANES)
    bp, bs = c_prompt.shape[0], c_sample.shape[0]
    assert norm1_gain.shape[0] == 1, "single-layer trunk"
    assert peer_sub_keys.shape[3] == LANES and peer_sub_keys.shape[4] == LANES

    rows = -(-(bp + bs) // 8) * 8
    c_all = jnp.concatenate([c_prompt, c_sample, jnp.zeros((rows - bp - bs, d), F32)], axis=0)
    ada = _ada(c_all, w_ada[0], b_ada[0])

    colscale = np.ones((1, w_in.shape[-1]), np.float32)
    colscale[0, :nh * LANES] = DIFF_HALF_DIM ** -0.5
    colscale[0, 3 * nh * LANES:4 * nh * LANES] = LANES ** -0.5
    lane_bcast = lambda v: jnp.asarray(np.broadcast_to(np.asarray(v, np.float32)[:, None, None], (nh, 1, LANES)))
    w = dict(
        norm1_gain=norm1_gain[0], norm2_gain=norm2_gain[0], final_gain=final_gain,
        w_in=w_in[0].astype(BF16), colscale=jnp.asarray(colscale),
        diff_nslopes=lane_bcast(-_alibi_slopes(nh)), dil_nslopes=lane_bcast(-_alibi_slopes(nh)),
        diff_lambda=diff_lambda[0], diff_subln_gain=diff_subln_gain[0],
        w_out=w_out[0].astype(BF16),
        wq_t=peer_w_query[0].T.astype(BF16), sub_keys=peer_sub_keys[0].astype(BF16),
        peer_u=peer_u[0].astype(BF16), peer_vt=peer_v[0].T.astype(BF16),
    )
    y_prompt = _trunk(x_prompt, ada[:bp], w)
    y_sample = _trunk(x_sample, ada[bp:bp + bs], w)
    return (y_prompt, y_sample)
```

```python
import functools
import math

import ml_dtypes
import numpy as np
import jax
import jax.numpy as jnp
from jax import lax
from jax.experimental import pallas as pl
from jax.experimental.pallas import tpu as pltpu

F32 = jnp.float32
BF16 = jnp.bfloat16
EPS = 1e-6
MASK_VALUE = -1e30
LANES = 128
VMEM_CAP_BYTES = 60000 * 1024
DIFF_HALF_DIM = 64
DIFF_KEY_BLOCK = 256
DIFF_PAD_ROWS = 16
DIL_CONFIGS = ((128, 1), (512, 4), (2048, 16))
DIL_UNROLL = 8
PEER_TOPK = 16
PEER_SUBTILE = 256
NT_DIMS = (((1,), (1,)), ((), ()))


def _params(semantics, vmem_bytes):
    return pltpu.CompilerParams(
        dimension_semantics=semantics,
        vmem_limit_bytes=int(min(max(vmem_bytes, 16 * 2**20), VMEM_CAP_BYTES)))


def _alibi_slopes(n):
    return 2.0 ** (-8.0 * np.arange(1, n + 1) / n)


def _ada_kernel(c_ref, w_ref, b_ref, o_ref):
    c = c_ref[...]
    a = (c / (1.0 + jnp.exp(-c))).astype(BF16)
    o_ref[...] = jnp.dot(a, w_ref[...].astype(BF16), preferred_element_type=F32) + b_ref[...]


def _ada(c_all, w_ada, b_ada):
    rows, d = c_all.shape
    n = w_ada.shape[1]
    tn = 512
    return pl.pallas_call(
        _ada_kernel,
        grid=(n // tn,),
        in_specs=[pl.BlockSpec((rows, d), lambda j: (0, 0)),
                  pl.BlockSpec((d, tn), lambda j: (0, j)),
                  pl.BlockSpec((1, tn), lambda j: (0, j))],
        out_specs=pl.BlockSpec((rows, tn), lambda j: (0, j)),
        out_shape=jax.ShapeDtypeStruct((rows, n), F32),
        compiler_params=_params(("arbitrary",), 2 * d * tn * 4 + d * tn * 2 + 4 * 2**20),
        name="ada",
    )(c_all, w_ada, b_ada.reshape(1, n))


def _norm_mod_kernel(x_ref, gain_ref, sc_ref, sh_ref, o_ref, *, transpose):
    x = x_ref[0]
    ms = jnp.mean(x * x, axis=-1, keepdims=True)
    y = x * lax.rsqrt(ms + EPS) * gain_ref[...]
    h = y * (1.0 + sc_ref[0]) + sh_ref[0]
    if transpose:
        o_ref[...] = h.T.astype(o_ref.dtype)
    else:
        o_ref[...] = h.astype(o_ref.dtype)


def _norm_mod(x, gain, sc, sh, *, transpose, tm=256):
    b, t, d = x.shape
    m = b * t
    tm = min(tm, t)
    nt = t // tm
    if transpose:
        out_spec = pl.BlockSpec((d, tm), lambda bi, i: (0, bi * nt + i))
        out_shape = jax.ShapeDtypeStruct((d, m), BF16)
    else:
        out_spec = pl.BlockSpec((tm, d), lambda bi, i: (bi * nt + i, 0))
        out_shape = jax.ShapeDtypeStruct((m, d), BF16)
    return pl.pallas_call(
        functools.partial(_norm_mod_kernel, transpose=transpose),
        grid=(b, nt),
        in_specs=[pl.BlockSpec((1, tm, d), lambda bi, i: (bi, i, 0)),
                  pl.BlockSpec((1, d), lambda bi, i: (0, 0)),
                  pl.BlockSpec((1, 1, d), lambda bi, i: (bi, 0, 0)),
                  pl.BlockSpec((1, 1, d), lambda bi, i: (bi, 0, 0))],
        out_specs=out_spec,
        out_shape=out_shape,
        compiler_params=_params(("parallel", "parallel"), 6 * tm * d * 4 + 4 * 2**20),
        name="norm_mod_t" if transpose else "norm_mod",
    )(x, gain.reshape(1, d), sc, sh)


def _inproj_kernel(a_ref, w_ref, cs_ref, o_ref):
    acc = jnp.dot(a_ref[...], w_ref[...], preferred_element_type=F32) * cs_ref[...]
    for c in range(o_ref.shape[0]):
        o_ref[c] = acc[:, c * LANES:(c + 1) * LANES].astype(o_ref.dtype)


def _inproj(h, w, colscale, *, tm=1024, tn=512):
    m, k = h.shape
    n = w.shape[1]
    tm, tn = min(tm, m), min(tn, n)
    nb = tn // LANES
    return pl.pallas_call(
        _inproj_kernel,
        grid=(m // tm, n // tn),
        in_specs=[pl.BlockSpec((tm, k), lambda i, j: (i, 0)),
                  pl.BlockSpec((k, tn), lambda i, j: (0, j)),
                  pl.BlockSpec((1, tn), lambda i, j: (0, j))],
        out_specs=pl.BlockSpec((nb, tm, LANES), lambda i, j: (j, i, 0)),
        out_shape=jax.ShapeDtypeStruct((n // LANES, m, LANES), BF16),
        compiler_params=_params(("parallel", "arbitrary"),
                                2 * (tm * k + k * tn + tm * tn) * 2 + 2 * tm * tn * 4 + 2 * 2**20),
        name="inproj",
    )(h, w, colscale)


def _alibi_tables(nh, tq, tk):
    bf = lambda a: np.asarray(a, np.float64).astype(ml_dtypes.bfloat16).astype(np.float64)
    slope = _alibi_slopes(nh) * math.log2(math.e)
    hi = bf(slope)
    lo = bf(slope - hi)
    augq = np.zeros((nh, tq, LANES), np.float64)
    augk = np.zeros((nh, 2, tk, LANES), np.float64)
    qpos, kpos = np.arange(tq), np.arange(tk)
    for h in range(nh):
        qcols = [qpos // 16, qpos // 16, qpos % 16, qpos % 16,
                 16 * hi[h] + 0 * qpos, 16 * lo[h] + 0 * qpos, hi[h] + 0 * qpos, lo[h] + 0 * qpos]
        kcols = [-16 * hi[h] + 0 * kpos, -16 * lo[h] + 0 * kpos, -hi[h] + 0 * kpos, -lo[h] + 0 * kpos,
                 kpos // 16, kpos // 16, kpos % 16, kpos % 16]
        for c in range(8):
            augq[h, :, c] = qcols[c]
            augk[h, 0, :, c] = kcols[c]
            augk[h, 1, :, c] = -kcols[c]
    nslope = np.broadcast_to(-(hi + lo)[:, None, None], (nh, 1, LANES))
    return (jnp.asarray(augq, BF16), jnp.asarray(augk, BF16), jnp.asarray(nslope, F32))


def _diff_attn_kernel(q_ref, k_ref, v_ref, augq_ref, augk_ref, ns_ref, lp_ref, gain_ref, o_ref,
                      vt_sc, st0, st1, p0, p1, al0, al1, m_sc, acc_sc, *, tile, lambda_init):
    qi = pl.program_id(2)
    t = k_ref.shape[1]
    n = t // tile
    kb = min(tile, DIFF_KEY_BLOCK)
    groups = tile // LANES

    @pl.when(qi == 0)
    def _():
        ones_row = jnp.where(lax.broadcasted_iota(jnp.int32, (DIFF_PAD_ROWS, tile), 0) == 0, 1.0, 0.0)
        for c in range(n):
            vt_sc[c, :LANES, :] = v_ref[0, c * tile:(c + 1) * tile, :].astype(F32).T.astype(BF16)
            vt_sc[c, LANES:, :] = ones_row.astype(BF16)

    q = q_ref[0]
    lane = lax.broadcasted_iota(jnp.int32, q.shape, 1)
    zero = jnp.zeros_like(q)
    qz = jnp.concatenate([jnp.where(lane < DIFF_HALF_DIM, q, zero),
                          jnp.where(lane >= DIFF_HALF_DIM, q, zero)], axis=0)
    aq = augq_ref[0]
    qx = jnp.concatenate([qz, jnp.concatenate([aq, aq], axis=0)], axis=1)
    nslope = ns_ref[0][:1, :1]
    m_sc[...] = jnp.full(m_sc.shape, MASK_VALUE, F32)
    acc_sc[...] = jnp.zeros(acc_sc.shape, F32)
    st_sc, p_sc, al_sc = (st0, st1), (p0, p1), (al0, al1)

    def chunk_at(pos):
        return jnp.where(pos == 0, qi, pos - (pos <= qi).astype(jnp.int32))

    def scores(j, side):
        rows = pl.ds(pl.multiple_of(j * tile, tile), tile)
        kx = jnp.concatenate([k_ref[0, rows, :], augk_ref[0, side]], axis=1)
        return lax.dot_general(kx, qx, NT_DIMS, preferred_element_type=F32)

    def softmax(j, buf):
        offset = nslope * (jnp.abs(qi - j) * tile).astype(F32)
        for g in range(2 * groups):
            cols = slice(g * LANES, (g + 1) * LANES)
            m_old = m_sc[:, cols]
            mx = m_old
            for r in range(tile // kb):
                s = st_sc[buf][r * kb:(r + 1) * kb, cols]
                mx = jnp.maximum(mx, jnp.max(s, axis=0, keepdims=True) + offset)
            al_sc[buf][:, cols] = jnp.exp2(m_old - mx)
            m_sc[:, cols] = mx
            sub = mx - offset
            for r in range(tile // kb):
                rows = slice(r * kb, (r + 1) * kb)
                p_sc[buf][rows, cols] = jnp.exp2(st_sc[buf][rows, cols] - sub).astype(BF16)

    def accumulate(j, buf):
        acc_sc[...] = al_sc[buf][...] * acc_sc[...] + jnp.dot(vt_sc[j], p_sc[buf][...], preferred_element_type=F32)

    p_sc[1][...] = jnp.zeros(p_sc[1].shape, BF16)
    al_sc[1][...] = jnp.ones(al_sc[1].shape, F32)
    st_sc[0][...] = jnp.minimum(scores(qi, 0), scores(qi, 1))

    def step(pos, buf):
        nxt = chunk_at(pos + 1)
        st_sc[1 - buf][...] = scores(nxt, (nxt > qi).astype(jnp.int32))
        softmax(chunk_at(pos), buf)
        accumulate(chunk_at(jnp.maximum(pos - 1, 0)), 1 - buf)

    def body(i, carry):
        step(2 * i, 0)
        step(2 * i + 1, 1)
        return carry

    lax.fori_loop(0, (n - 1) // 2, body, 0)
    if (n - 1) % 2:
        step(jnp.int32(n - 2), (n - 2) & 1)
    last = (n - 1) & 1
    softmax(chunk_at(jnp.int32(n - 1)), last)
    if n > 1:
        accumulate(chunk_at(jnp.int32(n - 2)), 1 - last)
    accumulate(chunk_at(jnp.int32(n - 1)), last)

    lp = lp_ref[...]
    lam = (jnp.exp(jnp.sum(lp[0:1] * lp[1:2], axis=1, keepdims=True))
           - jnp.exp(jnp.sum(lp[2:3] * lp[3:4], axis=1, keepdims=True)) + lambda_init)
    on = acc_sc[:LANES, :] / acc_sc[LANES:LANES + 1, :]
    o = on[:, :tile] - lam * on[:, tile:]
    o = o * lax.rsqrt(jnp.mean(o * o, axis=0, keepdims=True) + EPS) * (1.0 - lambda_init)
    o_ref[0] = (o.T * gain_ref[...]).astype(o_ref.dtype)


def _diff_attention(p, lam_params, subln_gain, *, b, t, nh, lambda_init, tile=512):
    tile = min(tile, t)
    nq = t // tile
    augq, augk, nslopes = _alibi_tables(nh, tile, tile)
    return pl.pallas_call(
        functools.partial(_diff_attn_kernel, tile=tile, lambda_init=lambda_init),
        grid=(b, nh, nq),
        in_specs=[pl.BlockSpec((1, tile, LANES), lambda bi, h, qi: (h, bi * nq + qi, 0)),
                  pl.BlockSpec((1, t, LANES), lambda bi, h, qi: (nh + h, bi, 0)),
                  pl.BlockSpec((1, t, LANES), lambda bi, h, qi: (2 * nh + h, bi, 0)),
                  pl.BlockSpec((1, tile, LANES), lambda bi, h, qi: (h, 0, 0)),
                  pl.BlockSpec((1, 2, tile, LANES), lambda bi, h, qi: (h, 0, 0, 0)),
                  pl.BlockSpec((1, 1, LANES), lambda bi, h, qi: (h, 0, 0)),
                  pl.BlockSpec(lam_params.shape, lambda bi, h, qi: (0, 0)),
                  pl.BlockSpec((1, LANES), lambda bi, h, qi: (0, 0))],
        out_specs=pl.BlockSpec((1, tile, LANES), lambda bi, h, qi: (bi, qi, h)),
        out_shape=jax.ShapeDtypeStruct((b, t, nh * LANES), BF16),
        scratch_shapes=[pltpu.VMEM((t // tile, LANES + DIFF_PAD_ROWS, tile), BF16),
                        pltpu.VMEM((tile, 2 * tile), F32), pltpu.VMEM((tile, 2 * tile), F32),
                        pltpu.VMEM((tile, 2 * tile), BF16), pltpu.VMEM((tile, 2 * tile), BF16),
                        pltpu.VMEM((1, 2 * tile), F32), pltpu.VMEM((1, 2 * tile), F32),
                        pltpu.VMEM((1, 2 * tile), F32),
                        pltpu.VMEM((LANES + DIFF_PAD_ROWS, 2 * tile), F32)],
        compiler_params=_params(("parallel", "parallel", "arbitrary"),
                                5 * t * LANES * 2 + 2 * tile * tile * 20 + 4 * LANES * tile * 4 + 8 * 2**20),
        name="diff_attn",
    )(p, p, p, augq, augk, nslopes, lam_params, subln_gain.reshape(1, LANES))


def _dil_attn_kernel(q_ref, k_ref, v_ref, ns_ref, o_ref, qf, kf, vf, qr, kr, vr, ob, lb, bt, *, blk):
    t = q_ref.shape[1]
    nslope = ns_ref[0][:, :1]
    if any(g > 1 for _, g in DIL_CONFIGS):
        qf[...] = q_ref[0].astype(F32)
        kf[...] = k_ref[0].astype(F32)
        vf[...] = v_ref[0].astype(F32)

    for n, (window, g) in enumerate(DIL_CONFIGS):
        radius = window // (2 * g)
        ln = t // g
        wk = min(blk + 2 * radius, ln)
        nblk = ln // blk
        if g > 1:
            for rho in range(g):
                rows = pl.ds(rho, ln, stride=g)
                qr[rho * ln:(rho + 1) * ln, :] = qf[rows, :].astype(BF16)
                kr[rho * ln:(rho + 1) * ln, :] = kf[rows, :].astype(BF16)
                vr[rho * ln:(rho + 1) * ln, :] = vf[rows, :].astype(BF16)
            qs, ks, vs = qr, kr, vr
        else:
            qs, ks, vs = q_ref.at[0], k_ref.at[0], v_ref.at[0]
        rel0 = (lax.broadcasted_iota(jnp.int32, (blk, wk), 1)
                - lax.broadcasted_iota(jnp.int32, (blk, wk), 0))
        for idx, shift in enumerate((0, -radius, blk - wk)):
            ar = jnp.abs(rel0 + shift)
            bt[idx, :, :wk] = jnp.where(ar <= radius, (nslope * float(g)) * ar.astype(F32), MASK_VALUE)

        def body(i, carry, n=n, g=g, radius=radius, ln=ln, wk=wk, nblk=nblk, qs=qs, ks=ks, vs=vs):
            rho = i // nblk
            bi = i % nblk
            l0 = bi * blk
            ws = jnp.clip(l0 - radius, 0, ln - wk)
            base = rho * ln
            qb = qs[pl.ds(pl.multiple_of(base + l0, blk), blk), :]
            kw = ks[pl.ds(pl.multiple_of(base + ws, radius), wk), :]
            vw = vs[pl.ds(pl.multiple_of(base + ws, radius), wk), :]
            place = jnp.where(bi == 0, 0, jnp.where(bi == nblk - 1, 2, 1))
            s = lax.dot_general(qb, kw, NT_DIMS, preferred_element_type=F32) + bt[place, :, :wk]
            mx = jnp.max(s, axis=1, keepdims=True)
            e = jnp.exp2(s - mx)
            den = jnp.sum(e, axis=1, keepdims=True)
            o = jnp.dot(e.astype(BF16), vw, preferred_element_type=F32) / den
            lse = mx + jnp.log2(den)
            rows = pl.ds(l0 * g + rho, blk, stride=g) if g > 1 else pl.ds(pl.multiple_of(l0, blk), blk)
            ob[n, rows, :] = o
            lb[n, rows, :] = jnp.broadcast_to(lse, (blk, LANES))
            return carry

        lax.fori_loop(0, g * nblk, body, 0, unroll=min(DIL_UNROLL, g * nblk))

    chunk = min(512, t)

    def merge(i, carry):
        rows = pl.ds(pl.multiple_of(i * chunk, chunk), chunk)
        l0, l1, l2 = lb[0, rows, :], lb[1, rows, :], lb[2, rows, :]
        mx = jnp.maximum(jnp.maximum(l0, l1), l2)
        w0, w1, w2 = jnp.exp2(l0 - mx), jnp.exp2(l1 - mx), jnp.exp2(l2 - mx)
        o = (w0 * ob[0, rows, :] + w1 * ob[1, rows, :] + w2 * ob[2, rows, :]) / (w0 + w1 + w2)
        o_ref[0, rows, :] = o.astype(o_ref.dtype)
        return carry

    lax.fori_loop(0, t // chunk, merge, 0)


def _dil_attention(p, nslopes, *, b, t, nh, blk=128):
    nbr = len(DIL_CONFIGS)
    return pl.pallas_call(
        functools.partial(_dil_attn_kernel, blk=blk),
        grid=(b, nh),
        in_specs=[pl.BlockSpec((1, t, LANES), lambda bi, h: (3 * nh + h, bi, 0)),
                  pl.BlockSpec((1, t, LANES), lambda bi, h: (4 * nh + h, bi, 0)),
                  pl.BlockSpec((1, t, LANES), lambda bi, h: (5 * nh + h, bi, 0)),
                  pl.BlockSpec((1, 1, LANES), lambda bi, h: (h, 0, 0))],
        out_specs=pl.BlockSpec((1, t, LANES), lambda bi, h: (bi, 0, h)),
        out_shape=jax.ShapeDtypeStruct((b, t, nh * LANES), BF16),
        scratch_shapes=[pltpu.VMEM((t, LANES), F32)] * 3 + [pltpu.VMEM((t, LANES), BF16)] * 3
                       + [pltpu.VMEM((nbr, t, LANES), F32)] * 2 + [pltpu.VMEM((3, blk, 2 * blk), F32)],
        compiler_params=_params(("parallel", "parallel"),
                                8 * t * LANES * 2 + 3 * t * LANES * 6 + 2 * nbr * t * LANES * 4 + 8 * 2**20),
        name="dil_attn",
    )(p, p, p, nslopes)


def _outproj_kernel(ma_ref, md_ref, wt_ref, wb_ref, x_ref, g_ref, o_ref):
    acc = (jnp.dot(ma_ref[...], wt_ref[...], preferred_element_type=F32)
           + jnp.dot(md_ref[...], wb_ref[...], preferred_element_type=F32))
    o_ref[...] = x_ref[...] + g_ref[0] * acc


def _outproj(ma, md, w_out, x2d, gate, *, t, tm=1024, tn=512):
    m, kh = ma.shape
    n = w_out.shape[1]
    tm, tn = min(tm, t), min(tn, n)
    return pl.pallas_call(
        _outproj_kernel,
        grid=(m // tm, n // tn),
        in_specs=[pl.BlockSpec((tm, kh), lambda i, j: (i, 0)),
                  pl.BlockSpec((tm, kh), lambda i, j: (i, 0)),
                  pl.BlockSpec((kh, tn), lambda i, j: (0, j)),
                  pl.BlockSpec((kh, tn), lambda i, j: (1, j)),
                  pl.BlockSpec((tm, tn), lambda i, j: (i, j)),
                  pl.BlockSpec((1, 1, tn), lambda i, j: ((i * tm) // t, 0, j))],
        out_specs=pl.BlockSpec((tm, tn), lambda i, j: (i, j)),
        out_shape=jax.ShapeDtypeStruct((m, n), F32),
        compiler_params=_params(("parallel", "arbitrary"),
                                2 * (2 * tm * kh + 2 * kh * tn) * 2 + 5 * tm * tn * 4 + 2 * 2**20),
        name="outproj",
    )(ma, md, w_out, w_out, x2d, gate)


def _take_top_rows(s, rows_ref, count):
    for r in range(count):
        mx = jnp.max(s, axis=0, keepdims=True)
        rows_ref[r:r + 1, :] = mx
        s = jnp.where(s >= mx, -jnp.inf, s)


def _router_kernel(h_ref, wq_ref, keys_ref, s0_ref, s1_ref, a_ref, b_ref, tau_ref, ra_ref, rb_ref, rc_ref):
    k = PEER_TOPK
    sub = keys_ref.shape[2]
    qt = jnp.dot(wq_ref[...], h_ref[...], preferred_element_type=F32)
    s0 = jnp.dot(keys_ref[0, 0], qt[:sub].astype(BF16), preferred_element_type=F32)
    s1 = jnp.dot(keys_ref[0, 1], qt[sub:].astype(BF16), preferred_element_type=F32)
    _take_top_rows(s0, ra_ref, k)
    _take_top_rows(s1, rb_ref, k)
    ra, rb = ra_ref[...], rb_ref[...]
    row8 = lax.broadcasted_iota(jnp.int32, (8, ra.shape[1]), 0)
    pieces = [ra[0:1] + rb, ra[1:2] + rb[0:8]]
    for r in range(2, 8):
        pieces.append(jnp.where(row8 < k // (r + 1), ra[r:r + 1] + rb[0:8], -jnp.inf))
    pieces.append(ra[8:16] + rb[0:1])
    cand = jnp.concatenate(pieces, axis=0)
    _take_top_rows(cand, rc_ref, k)
    tau = rc_ref[k - 1:k, :]
    top = ra[0:1] + rb[0:1]
    z = jnp.sum(jnp.where(cand >= tau, jnp.exp(cand - top), 0.0), axis=0, keepdims=True)
    s0_ref[0] = s0
    s1_ref[0] = s1
    a_ref[0] = jnp.exp(s0 - ra[0:1]) / z
    b_ref[0] = jnp.exp(s1 - rb[0:1])
    tau_ref[0] = tau


def _router(ht, wq_t, keys, *, tm=512):
    d, m = ht.shape
    heads, _, n_keys, sub = keys.shape
    tm = min(tm, m)
    big = jax.ShapeDtypeStruct((heads, n_keys, m), F32)
    big_spec = pl.BlockSpec((1, n_keys, tm), lambda i, h: (h, 0, i))
    return pl.pallas_call(
        _router_kernel,
        grid=(m // tm, heads),
        in_specs=[pl.BlockSpec((d, tm), lambda i, h: (0, i)),
                  pl.BlockSpec((2 * sub, d), lambda i, h: (h, 0)),
                  pl.BlockSpec((1, 2, n_keys, sub), lambda i, h: (h, 0, 0, 0))],
        out_specs=[big_spec, big_spec, big_spec, big_spec,
                   pl.BlockSpec((1, 1, tm), lambda i, h: (h, 0, i))],
        out_shape=[big, big, big, big, jax.ShapeDtypeStruct((heads, 1, m), F32)],
        scratch_shapes=[pltpu.VMEM((PEER_TOPK, tm), F32)] * 3,
        compiler_params=_params(("parallel", "arbitrary"),
                                2 * d * tm * 2 + 2 * 2 * sub * d * 2 + 8 * n_keys * tm * 4 + 16 * 2**20),
        name="peer_router",
    )(ht, wq_t, keys)


def _peer_dense_kernel(h_ref, u_ref, vt_ref, s0_ref, s1_ref, a_ref, b_ref, tau_ref, o_ref, g_sc):
    e = pl.program_id(1)
    heads, n_keys, tm = s1_ref.shape
    te = u_ref.shape[0]
    sub = min(te, PEER_SUBTILE)

    @pl.when(e == 0)
    def _():
        o_ref[...] = jnp.zeros(o_ref.shape, F32)

    for ii in range(te // n_keys):
        i = e * (te // n_keys) + ii
        gates = jnp.zeros((n_keys, tm), F32)
        for h in range(heads):
            pair = s1_ref[h] + s0_ref[h, pl.ds(i, 1), :]
            gates = gates + jnp.where(pair >= tau_ref[h], b_ref[h], 0.0) * a_ref[h, pl.ds(i, 1), :]
        g_sc[ii * n_keys:(ii + 1) * n_keys, :] = gates
    nsub = te // sub
    acts = [jnp.dot(u_ref[t * sub:(t + 1) * sub, :], h_ref[...], preferred_element_type=F32)
            for t in range(nsub)]
    for t in range(nsub):
        gel = 0.5 * acts[t] * (1.0 + lax.erf(acts[t] * (1.0 / math.sqrt(2.0))))
        w = (g_sc[t * sub:(t + 1) * sub, :] * gel).astype(BF16)
        o_ref[...] += jnp.dot(vt_ref[:, t * sub:(t + 1) * sub], w, preferred_element_type=F32)


def _peer_dense(ht, u, vt, s0, s1, a, bf, tau, *, tm=512, te=512):
    d, m = ht.shape
    n_exp = u.shape[0]
    heads, n_keys, _ = s0.shape
    tm = min(tm, m)
    once = pl.Buffered(1)
    tok = lambda i, e: (0, 0, i)
    big_spec = pl.BlockSpec((heads, n_keys, tm), tok, pipeline_mode=once)
    return pl.pallas_call(
        _peer_dense_kernel,
        grid=(m // tm, n_exp // te),
        in_specs=[pl.BlockSpec((d, tm), lambda i, e: (0, i), pipeline_mode=once),
                  pl.BlockSpec((te, d), lambda i, e: (e, 0)),
                  pl.BlockSpec((d, te), lambda i, e: (0, e)),
                  big_spec, big_spec, big_spec, big_spec,
                  pl.BlockSpec((heads, 1, tm), tok, pipeline_mode=once)],
        out_specs=pl.BlockSpec((d, tm), lambda i, e: (0, i)),
        out_shape=jax.ShapeDtypeStruct((d, m), F32),
        scratch_shapes=[pltpu.VMEM((te, tm), F32)],
        compiler_params=_params(("parallel", "arbitrary"),
                                d * tm * 2 + 4 * te * d * 2 + 2 * d * tm * 4 + 4 * heads * n_keys * tm * 4
                                + 6 * te * tm * 4 + d * tm * 4),
        name="peer_dense",
    )(ht, u, vt, s0, s1, a, bf, tau)


def _final_kernel(x_ref, pt_ref, g_ref, gain_ref, o_ref):
    x = x_ref[...] + g_ref[0] * pt_ref[...].T
    ms = jnp.mean(x * x, axis=-1, keepdims=True)
    o_ref[...] = x * lax.rsqrt(ms + EPS) * gain_ref[...]


def _final(x1, pt, gate, gain, *, t, tm=256):
    m, d = x1.shape
    tm = min(tm, t)
    return pl.pallas_call(
        _final_kernel,
        grid=(m // tm,),
        in_specs=[pl.BlockSpec((tm, d), lambda i: (i, 0)),
                  pl.BlockSpec((d, tm), lambda i: (0, i)),
                  pl.BlockSpec((1, 1, d), lambda i: ((i * tm) // t, 0, 0)),
                  pl.BlockSpec((1, d), lambda i: (0, 0))],
        out_specs=pl.BlockSpec((tm, d), lambda i: (i, 0)),
        out_shape=jax.ShapeDtypeStruct((m, d), F32),
        compiler_params=_params(("parallel",), 8 * tm * d * 4 + 4 * 2**20),
        name="final_norm",
    )(x1, pt, gate, gain.reshape(1, d))


def _trunk(x, ada, w, depth_index=0):
    b, t, d = x.shape
    nh = d // (2 * LANES)
    lambda_init = 0.8 - 0.6 * math.exp(-0.3 * depth_index)
    sh1, sc1, g1, sh2, sc2, g2 = [a.reshape(b, 1, d) for a in jnp.split(ada, 6, axis=-1)]

    h = _norm_mod(x, w["norm1_gain"], sc1, sh1, transpose=False)
    p = _inproj(h, w["w_in"], w["colscale"])
    ma = _diff_attention(p, w["diff_lambda"], w["diff_subln_gain"], b=b, t=t, nh=nh, lambda_init=lambda_init)
    md = _dil_attention(p, w["dil_nslopes"], b=b, t=t, nh=nh)
    x1 = _outproj(ma.reshape(b * t, nh * LANES), md.reshape(b * t, nh * LANES), w["w_out"],
                  x.reshape(b * t, d), g1, t=t)

    ht = _norm_mod(x1.reshape(b, t, d), w["norm2_gain"], sc2, sh2, transpose=True)
    s0, s1, a, bf, tau = _router(ht, w["wq_t"], w["sub_keys"])
    pt = _peer_dense(ht, w["peer_u"], w["peer_vt"], s0, s1, a, bf, tau)
    y = _final(x1, pt, g2, w["final_gain"], t=t)
    return y.reshape(b, t, d)


def kernel(x_prompt, x_sample, c_prompt, c_sample, norm1_gain, norm2_gain, w_ada, b_ada, w_in, diff_lambda,
           diff_subln_gain, w_out, peer_w_query, peer_sub_keys, peer_u, peer_v, final_gain):
    d = x_prompt.shape[-1]
    nh = d // (2 * LANES)
    bp, bs = c_prompt.shape[0], c_sample.shape[0]
    assert norm1_gain.shape[0] == 1, "single-layer trunk"
    assert peer_sub_keys.shape[3] == LANES and peer_sub_keys.shape[4] == LANES

    rows = -(-(bp + bs) // 8) * 8
    c_all = jnp.concatenate([c_prompt, c_sample, jnp.zeros((rows - bp - bs, d), F32)], axis=0)
    ada = _ada(c_all, w_ada[0], b_ada[0])

    colscale = np.ones((1, w_in.shape[-1]), np.float32)
    colscale[0, :nh * LANES] = DIFF_HALF_DIM ** -0.5 * math.log2(math.e)
    colscale[0, 3 * nh * LANES:4 * nh * LANES] = LANES ** -0.5 * math.log2(math.e)
    lane_bcast = lambda v: jnp.asarray(np.broadcast_to(np.asarray(v, np.float32)[:, None, None], (nh, 1, LANES)))
    w = dict(
        norm1_gain=norm1_gain[0], norm2_gain=norm2_gain[0], final_gain=final_gain,
        w_in=w_in[0].astype(BF16), colscale=jnp.asarray(colscale),
        dil_nslopes=lane_bcast(-_alibi_slopes(nh) * math.log2(math.e)),
        diff_lambda=diff_lambda[0], diff_subln_gain=diff_subln_gain[0],
        w_out=w_out[0].astype(BF16),
        wq_t=peer_w_query[0].T.astype(BF16), sub_keys=peer_sub_keys[0].astype(BF16),
        peer_u=peer_u[0].astype(BF16), peer_vt=peer_v[0].T.astype(BF16),
    )
    y_prompt = _trunk(x_prompt, ada[:bp], w)
    y_sample = _trunk(x_sample, ada[bp:bp + bs], w)
    return (y_prompt, y_sample)
```

```python
import functools
import math

import ml_dtypes
import numpy as np
import jax
import jax.numpy as jnp
from jax import lax
from jax.experimental import pallas as pl
from jax.experimental.pallas import tpu as pltpu

F32 = jnp.float32
BF16 = jnp.bfloat16
F8 = jnp.float8_e4m3fn
F8_TARGET = 240.0
TINY = 1e-30
EPS = 1e-6
MASK_VALUE = -1e30
LANES = 128
VMEM_CAP_BYTES = 60000 * 1024
DIFF_HALF_DIM = 64
DIFF_KEY_BLOCK = 256
DIFF_PAD_ROWS = 16
DIL_CONFIGS = ((128, 1), (512, 4), (2048, 16))
DIL_UNROLL = 8
PEER_TOPK = 16
PEER_SUBTILE = 256
NT_DIMS = (((1,), (1,)), ((), ()))


def _params(semantics, vmem_bytes):
    return pltpu.CompilerParams(
        dimension_semantics=semantics,
        vmem_limit_bytes=int(min(max(vmem_bytes, 16 * 2**20), VMEM_CAP_BYTES)))


def _alibi_slopes(n):
    return 2.0 ** (-8.0 * np.arange(1, n + 1) / n)


def _ada_kernel(c_ref, w_ref, b_ref, o_ref):
    c = c_ref[...]
    a = (c / (1.0 + jnp.exp(-c))).astype(BF16)
    o_ref[...] = jnp.dot(a, w_ref[...].astype(BF16), preferred_element_type=F32) + b_ref[...]


def _ada(c_all, w_ada, b_ada):
    rows, d = c_all.shape
    n = w_ada.shape[1]
    tn = 512
    return pl.pallas_call(
        _ada_kernel,
        grid=(n // tn,),
        in_specs=[pl.BlockSpec((rows, d), lambda j: (0, 0)),
                  pl.BlockSpec((d, tn), lambda j: (0, j)),
                  pl.BlockSpec((1, tn), lambda j: (0, j))],
        out_specs=pl.BlockSpec((rows, tn), lambda j: (0, j)),
        out_shape=jax.ShapeDtypeStruct((rows, n), F32),
        compiler_params=_params(("arbitrary",), 2 * d * tn * 4 + d * tn * 2 + 4 * 2**20),
        name="ada",
    )(c_all, w_ada, b_ada.reshape(1, n))


def _norm_mod_kernel(x_ref, gain_ref, sc_ref, sh_ref, o_ref, *rest, transpose):
    x = x_ref[0]
    ms = jnp.mean(x * x, axis=-1, keepdims=True)
    y = x * lax.rsqrt(ms + EPS) * gain_ref[...]
    h = y * (1.0 + sc_ref[0]) + sh_ref[0]
    if transpose:
        q_ref, stat_ref = rest
        ht = h.T
        o_ref[...] = ht.astype(o_ref.dtype)
        amax = jnp.maximum(jnp.max(jnp.abs(ht), axis=0, keepdims=True), TINY)
        q_ref[...] = (ht * (F8_TARGET / amax)).astype(q_ref.dtype)
        stat_ref[0:1, :] = amax * (1.0 / F8_TARGET)
        stat_ref[1:2, :] = jnp.sqrt(jnp.sum(ht * ht, axis=0, keepdims=True))
    else:
        o_ref[...] = h.astype(o_ref.dtype)


def _norm_mod(x, gain, sc, sh, *, transpose, tm=256):
    b, t, d = x.shape
    m = b * t
    tm = min(tm, t)
    nt = t // tm
    if transpose:
        col = lambda bi, i: (0, bi * nt + i)
        out_spec = [pl.BlockSpec((d, tm), col), pl.BlockSpec((d, tm), col), pl.BlockSpec((2, tm), col)]
        out_shape = [jax.ShapeDtypeStruct((d, m), BF16), jax.ShapeDtypeStruct((d, m), F8),
                     jax.ShapeDtypeStruct((2, m), F32)]
    else:
        out_spec = pl.BlockSpec((tm, d), lambda bi, i: (bi * nt + i, 0))
        out_shape = jax.ShapeDtypeStruct((m, d), BF16)
    return pl.pallas_call(
        functools.partial(_norm_mod_kernel, transpose=transpose),
        grid=(b, nt),
        in_specs=[pl.BlockSpec((1, tm, d), lambda bi, i: (bi, i, 0)),
                  pl.BlockSpec((1, d), lambda bi, i: (0, 0)),
                  pl.BlockSpec((1, 1, d), lambda bi, i: (bi, 0, 0)),
                  pl.BlockSpec((1, 1, d), lambda bi, i: (bi, 0, 0))],
        out_specs=out_spec,
        out_shape=out_shape,
        compiler_params=_params(("parallel", "parallel"), 8 * tm * d * 4 + 4 * 2**20),
        name="norm_mod_t" if transpose else "norm_mod",
    )(x, gain.reshape(1, d), sc, sh)


def _inproj_kernel(a_ref, w_ref, cs_ref, o_ref):
    acc = jnp.dot(a_ref[...], w_ref[...], preferred_element_type=F32) * cs_ref[...]
    for c in range(o_ref.shape[0]):
        o_ref[c] = acc[:, c * LANES:(c + 1) * LANES].astype(o_ref.dtype)


def _inproj(h, w, colscale, *, tm=1024, tn=512):
    m, k = h.shape
    n = w.shape[1]
    tm, tn = min(tm, m), min(tn, n)
    nb = tn // LANES
    return pl.pallas_call(
        _inproj_kernel,
        grid=(m // tm, n // tn),
        in_specs=[pl.BlockSpec((tm, k), lambda i, j: (i, 0)),
                  pl.BlockSpec((k, tn), lambda i, j: (0, j)),
                  pl.BlockSpec((1, tn), lambda i, j: (0, j))],
        out_specs=pl.BlockSpec((nb, tm, LANES), lambda i, j: (j, i, 0)),
        out_shape=jax.ShapeDtypeStruct((n // LANES, m, LANES), BF16),
        compiler_params=_params(("parallel", "arbitrary"),
                                2 * (tm * k + k * tn + tm * tn) * 2 + 2 * tm * tn * 4 + 2 * 2**20),
        name="inproj",
    )(h, w, colscale)


def _alibi_tables(nh, tq, tk):
    bf = lambda a: np.asarray(a, np.float64).astype(ml_dtypes.bfloat16).astype(np.float64)
    slope = _alibi_slopes(nh) * math.log2(math.e)
    hi = bf(slope)
    lo = bf(slope - hi)
    augq = np.zeros((nh, tq, LANES), np.float64)
    augk = np.zeros((nh, 2, tk, LANES), np.float64)
    qpos, kpos = np.arange(tq), np.arange(tk)
    for h in range(nh):
        qcols = [qpos // 16, qpos // 16, qpos % 16, qpos % 16,
                 16 * hi[h] + 0 * qpos, 16 * lo[h] + 0 * qpos, hi[h] + 0 * qpos, lo[h] + 0 * qpos]
        kcols = [-16 * hi[h] + 0 * kpos, -16 * lo[h] + 0 * kpos, -hi[h] + 0 * kpos, -lo[h] + 0 * kpos,
                 kpos // 16, kpos // 16, kpos % 16, kpos % 16]
        for c in range(8):
            augq[h, :, c] = qcols[c]
            augk[h, 0, :, c] = kcols[c]
            augk[h, 1, :, c] = -kcols[c]
    nslope = np.broadcast_to(-(hi + lo)[:, None, None], (nh, 1, LANES))
    return (jnp.asarray(augq, BF16), jnp.asarray(augk, BF16), jnp.asarray(nslope, F32))


def _diff_attn_kernel(q_ref, k_ref, v_ref, augq_ref, augk_ref, ns_ref, lp_ref, gain_ref, o_ref,
                      vt_sc, st0, st1, p0, p1, al0, al1, m_sc, acc_sc, *, tile, lambda_init):
    qi = pl.program_id(2)
    t = k_ref.shape[1]
    n = t // tile
    kb = min(tile, DIFF_KEY_BLOCK)
    groups = tile // LANES

    @pl.when(qi == 0)
    def _():
        ones_row = jnp.where(lax.broadcasted_iota(jnp.int32, (DIFF_PAD_ROWS, tile), 0) == 0, 1.0, 0.0)
        for c in range(n):
            vt_sc[c, :LANES, :] = v_ref[0, c * tile:(c + 1) * tile, :].astype(F32).T.astype(BF16)
            vt_sc[c, LANES:, :] = ones_row.astype(BF16)

    q = q_ref[0]
    lane = lax.broadcasted_iota(jnp.int32, q.shape, 1)
    zero = jnp.zeros_like(q)
    qz = jnp.concatenate([jnp.where(lane < DIFF_HALF_DIM, q, zero),
                          jnp.where(lane >= DIFF_HALF_DIM, q, zero)], axis=0)
    aq = augq_ref[0]
    qx = jnp.concatenate([qz, jnp.concatenate([aq, aq], axis=0)], axis=1)
    nslope = ns_ref[0][:1, :1]
    m_sc[...] = jnp.full(m_sc.shape, MASK_VALUE, F32)
    acc_sc[...] = jnp.zeros(acc_sc.shape, F32)
    st_sc, p_sc, al_sc = (st0, st1), (p0, p1), (al0, al1)

    def chunk_at(pos):
        return jnp.where(pos == 0, qi, pos - (pos <= qi).astype(jnp.int32))

    def scores(j, side):
        rows = pl.ds(pl.multiple_of(j * tile, tile), tile)
        kx = jnp.concatenate([k_ref[0, rows, :], augk_ref[0, side]], axis=1)
        return lax.dot_general(kx, qx, NT_DIMS, preferred_element_type=F32)

    def softmax(j, buf):
        offset = nslope * (jnp.abs(qi - j) * tile).astype(F32)
        for g in range(2 * groups):
            cols = slice(g * LANES, (g + 1) * LANES)
            m_old = m_sc[:, cols]
            mx = m_old
            for r in range(tile // kb):
                s = st_sc[buf][r * kb:(r + 1) * kb, cols]
                mx = jnp.maximum(mx, jnp.max(s, axis=0, keepdims=True) + offset)
            al_sc[buf][:, cols] = jnp.exp2(m_old - mx)
            m_sc[:, cols] = mx
            sub = mx - offset
            for r in range(tile // kb):
                rows = slice(r * kb, (r + 1) * kb)
                p_sc[buf][rows, cols] = jnp.exp2(st_sc[buf][rows, cols] - sub).astype(BF16)

    def accumulate(j, buf):
        acc_sc[...] = al_sc[buf][...] * acc_sc[...] + jnp.dot(vt_sc[j], p_sc[buf][...], preferred_element_type=F32)

    p_sc[1][...] = jnp.zeros(p_sc[1].shape, BF16)
    al_sc[1][...] = jnp.ones(al_sc[1].shape, F32)
    st_sc[0][...] = jnp.minimum(scores(qi, 0), scores(qi, 1))

    def step(pos, buf):
        nxt = chunk_at(pos + 1)
        st_sc[1 - buf][...] = scores(nxt, (nxt > qi).astype(jnp.int32))
        softmax(chunk_at(pos), buf)
        accumulate(chunk_at(jnp.maximum(pos - 1, 0)), 1 - buf)

    def body(i, carry):
        step(2 * i, 0)
        step(2 * i + 1, 1)
        return carry

    lax.fori_loop(0, (n - 1) // 2, body, 0)
    if (n - 1) % 2:
        step(jnp.int32(n - 2), (n - 2) & 1)
    last = (n - 1) & 1
    softmax(chunk_at(jnp.int32(n - 1)), last)
    if n > 1:
        accumulate(chunk_at(jnp.int32(n - 2)), 1 - last)
    accumulate(chunk_at(jnp.int32(n - 1)), last)

    lp = lp_ref[...]
    lam = (jnp.exp(jnp.sum(lp[0:1] * lp[1:2], axis=1, keepdims=True))
           - jnp.exp(jnp.sum(lp[2:3] * lp[3:4], axis=1, keepdims=True)) + lambda_init)
    on = acc_sc[:LANES, :] / acc_sc[LANES:LANES + 1, :]
    o = on[:, :tile] - lam * on[:, tile:]
    o = o * lax.rsqrt(jnp.mean(o * o, axis=0, keepdims=True) + EPS) * (1.0 - lambda_init)
    o_ref[0] = (o.T * gain_ref[...]).astype(o_ref.dtype)


def _diff_attention(p, lam_params, subln_gain, *, b, t, nh, lambda_init, tile=512):
    tile = min(tile, t)
    nq = t // tile
    augq, augk, nslopes = _alibi_tables(nh, tile, tile)
    return pl.pallas_call(
        functools.partial(_diff_attn_kernel, tile=tile, lambda_init=lambda_init),
        grid=(b, nh, nq),
        in_specs=[pl.BlockSpec((1, tile, LANES), lambda bi, h, qi: (h, bi * nq + qi, 0)),
                  pl.BlockSpec((1, t, LANES), lambda bi, h, qi: (nh + h, bi, 0)),
                  pl.BlockSpec((1, t, LANES), lambda bi, h, qi: (2 * nh + h, bi, 0)),
                  pl.BlockSpec((1, tile, LANES), lambda bi, h, qi: (h, 0, 0)),
                  pl.BlockSpec((1, 2, tile, LANES), lambda bi, h, qi: (h, 0, 0, 0)),
                  pl.BlockSpec((1, 1, LANES), lambda bi, h, qi: (h, 0, 0)),
                  pl.BlockSpec(lam_params.shape, lambda bi, h, qi: (0, 0)),
                  pl.BlockSpec((1, LANES), lambda bi, h, qi: (0, 0))],
        out_specs=pl.BlockSpec((1, tile, LANES), lambda bi, h, qi: (bi, qi, h)),
        out_shape=jax.ShapeDtypeStruct((b, t, nh * LANES), BF16),
        scratch_shapes=[pltpu.VMEM((t // tile, LANES + DIFF_PAD_ROWS, tile), BF16),
                        pltpu.VMEM((tile, 2 * tile), F32), pltpu.VMEM((tile, 2 * tile), F32),
                        pltpu.VMEM((tile, 2 * tile), BF16), pltpu.VMEM((tile, 2 * tile), BF16),
                        pltpu.VMEM((1, 2 * tile), F32), pltpu.VMEM((1, 2 * tile), F32),
                        pltpu.VMEM((1, 2 * tile), F32),
                        pltpu.VMEM((LANES + DIFF_PAD_ROWS, 2 * tile), F32)],
        compiler_params=_params(("parallel", "parallel", "arbitrary"),
                                5 * t * LANES * 2 + 2 * tile * tile * 20 + 4 * LANES * tile * 4 + 8 * 2**20),
        name="diff_attn",
    )(p, p, p, augq, augk, nslopes, lam_params, subln_gain.reshape(1, LANES))


def _dil_attn_kernel(q_ref, k_ref, v_ref, ns_ref, o_ref, qf, kf, vf, qr, kr, vr, ob, lb, bt, *, blk):
    t = q_ref.shape[1]
    nslope = ns_ref[0][:, :1]
    if any(g > 1 for _, g in DIL_CONFIGS):
        qf[...] = q_ref[0].astype(F32)
        kf[...] = k_ref[0].astype(F32)
        vf[...] = v_ref[0].astype(F32)

    for n, (window, g) in enumerate(DIL_CONFIGS):
        radius = window // (2 * g)
        ln = t // g
        wk = min(blk + 2 * radius, ln)
        nblk = ln // blk
        if g > 1:
            for rho in range(g):
                rows = pl.ds(rho, ln, stride=g)
                qr[rho * ln:(rho + 1) * ln, :] = qf[rows, :].astype(BF16)
                kr[rho * ln:(rho + 1) * ln, :] = kf[rows, :].astype(BF16)
                vr[rho * ln:(rho + 1) * ln, :] = vf[rows, :].astype(BF16)
            qs, ks, vs = qr, kr, vr
        else:
            qs, ks, vs = q_ref.at[0], k_ref.at[0], v_ref.at[0]
        rel0 = (lax.broadcasted_iota(jnp.int32, (blk, wk), 1)
                - lax.broadcasted_iota(jnp.int32, (blk, wk), 0))
        for idx, shift in enumerate((0, -radius, blk - wk)):
            ar = jnp.abs(rel0 + shift)
            bt[idx, :, :wk] = jnp.where(ar <= radius, (nslope * float(g)) * ar.astype(F32), MASK_VALUE)

        def body(i, carry, n=n, g=g, radius=radius, ln=ln, wk=wk, nblk=nblk, qs=qs, ks=ks, vs=vs):
            rho = i // nblk
            bi = i % nblk
            l0 = bi * blk
            ws = jnp.clip(l0 - radius, 0, ln - wk)
            base = rho * ln
            qb = qs[pl.ds(pl.multiple_of(base + l0, blk), blk), :]
            kw = ks[pl.ds(pl.multiple_of(base + ws, radius), wk), :]
            vw = vs[pl.ds(pl.multiple_of(base + ws, radius), wk), :]
            place = jnp.where(bi == 0, 0, jnp.where(bi == nblk - 1, 2, 1))
            s = lax.dot_general(qb, kw, NT_DIMS, preferred_element_type=F32) + bt[place, :, :wk]
            mx = jnp.max(s, axis=1, keepdims=True)
            e = jnp.exp2(s - mx)
            den = jnp.sum(e, axis=1, keepdims=True)
            o = jnp.dot(e.astype(BF16), vw, preferred_element_type=F32) / den
            lse = mx + jnp.log2(den)
            rows = pl.ds(l0 * g + rho, blk, stride=g) if g > 1 else pl.ds(pl.multiple_of(l0, blk), blk)
            ob[n, rows, :] = o
            lb[n, rows, :] = jnp.broadcast_to(lse, (blk, LANES))
            return carry

        lax.fori_loop(0, g * nblk, body, 0, unroll=min(DIL_UNROLL, g * nblk))

    chunk = min(512, t)

    def merge(i, carry):
        rows = pl.ds(pl.multiple_of(i * chunk, chunk), chunk)
        l0, l1, l2 = lb[0, rows, :], lb[1, rows, :], lb[2, rows, :]
        mx = jnp.maximum(jnp.maximum(l0, l1), l2)
        w0, w1, w2 = jnp.exp2(l0 - mx), jnp.exp2(l1 - mx), jnp.exp2(l2 - mx)
        o = (w0 * ob[0, rows, :] + w1 * ob[1, rows, :] + w2 * ob[2, rows, :]) / (w0 + w1 + w2)
        o_ref[0, rows, :] = o.astype(o_ref.dtype)
        return carry

    lax.fori_loop(0, t // chunk, merge, 0)


def _dil_attention(p, nslopes, *, b, t, nh, blk=128):
    nbr = len(DIL_CONFIGS)
    return pl.pallas_call(
        functools.partial(_dil_attn_kernel, blk=blk),
        grid=(b, nh),
        in_specs=[pl.BlockSpec((1, t, LANES), lambda bi, h: (3 * nh + h, bi, 0)),
                  pl.BlockSpec((1, t, LANES), lambda bi, h: (4 * nh + h, bi, 0)),
                  pl.BlockSpec((1, t, LANES), lambda bi, h: (5 * nh + h, bi, 0)),
                  pl.BlockSpec((1, 1, LANES), lambda bi, h: (h, 0, 0))],
        out_specs=pl.BlockSpec((1, t, LANES), lambda bi, h: (bi, 0, h)),
        out_shape=jax.ShapeDtypeStruct((b, t, nh * LANES), BF16),
        scratch_shapes=[pltpu.VMEM((t, LANES), F32)] * 3 + [pltpu.VMEM((t, LANES), BF16)] * 3
                       + [pltpu.VMEM((nbr, t, LANES), F32)] * 2 + [pltpu.VMEM((3, blk, 2 * blk), F32)],
        compiler_params=_params(("parallel", "parallel"),
                                8 * t * LANES * 2 + 3 * t * LANES * 6 + 2 * nbr * t * LANES * 4 + 8 * 2**20),
        name="dil_attn",
    )(p, p, p, nslopes)


def _outproj_kernel(ma_ref, md_ref, wt_ref, wb_ref, x_ref, g_ref, o_ref):
    acc = (jnp.dot(ma_ref[...], wt_ref[...], preferred_element_type=F32)
           + jnp.dot(md_ref[...], wb_ref[...], preferred_element_type=F32))
    o_ref[...] = x_ref[...] + g_ref[0] * acc


def _outproj(ma, md, w_out, x2d, gate, *, t, tm=1024, tn=512):
    m, kh = ma.shape
    n = w_out.shape[1]
    tm, tn = min(tm, t), min(tn, n)
    return pl.pallas_call(
        _outproj_kernel,
        grid=(m // tm, n // tn),
        in_specs=[pl.BlockSpec((tm, kh), lambda i, j: (i, 0)),
                  pl.BlockSpec((tm, kh), lambda i, j: (i, 0)),
                  pl.BlockSpec((kh, tn), lambda i, j: (0, j)),
                  pl.BlockSpec((kh, tn), lambda i, j: (1, j)),
                  pl.BlockSpec((tm, tn), lambda i, j: (i, j)),
                  pl.BlockSpec((1, 1, tn), lambda i, j: ((i * tm) // t, 0, j))],
        out_specs=pl.BlockSpec((tm, tn), lambda i, j: (i, j)),
        out_shape=jax.ShapeDtypeStruct((m, n), F32),
        compiler_params=_params(("parallel", "arbitrary"),
                                2 * (2 * tm * kh + 2 * kh * tn) * 2 + 5 * tm * tn * 4 + 2 * 2**20),
        name="outproj",
    )(ma, md, w_out, w_out, x2d, gate)


def _take_top_rows(s, rows_ref, count):
    for r in range(count):
        mx = jnp.max(s, axis=0, keepdims=True)
        rows_ref[r:r + 1, :] = mx
        s = jnp.where(s >= mx, -jnp.inf, s)


def _router_kernel(h_ref, wq_ref, keys_ref, s0_ref, s1_ref, a_ref, b_ref, tau_ref, ra_ref, rb_ref, rc_ref):
    k = PEER_TOPK
    sub = keys_ref.shape[2]
    qt = jnp.dot(wq_ref[...], h_ref[...], preferred_element_type=F32)
    s0 = jnp.dot(keys_ref[0, 0], qt[:sub].astype(BF16), preferred_element_type=F32)
    s1 = jnp.dot(keys_ref[0, 1], qt[sub:].astype(BF16), preferred_element_type=F32)
    _take_top_rows(s0, ra_ref, k)
    _take_top_rows(s1, rb_ref, k)
    ra, rb = ra_ref[...], rb_ref[...]
    row8 = lax.broadcasted_iota(jnp.int32, (8, ra.shape[1]), 0)
    pieces = [ra[0:1] + rb, ra[1:2] + rb[0:8]]
    for r in range(2, 8):
        pieces.append(jnp.where(row8 < k // (r + 1), ra[r:r + 1] + rb[0:8], -jnp.inf))
    pieces.append(ra[8:16] + rb[0:1])
    cand = jnp.concatenate(pieces, axis=0)
    _take_top_rows(cand, rc_ref, k)
    tau = rc_ref[k - 1:k, :]
    top = ra[0:1] + rb[0:1]
    z = jnp.sum(jnp.where(cand >= tau, jnp.exp(cand - top), 0.0), axis=0, keepdims=True)
    s0_ref[0] = s0
    s1_ref[0] = s1
    a_ref[0] = jnp.exp(s0 - ra[0:1]) / z
    b_ref[0] = jnp.exp(s1 - rb[0:1])
    tau_ref[0] = tau


def _router(ht, wq_t, keys, *, tm=512):
    d, m = ht.shape
    heads, _, n_keys, sub = keys.shape
    tm = min(tm, m)
    big = jax.ShapeDtypeStruct((heads, n_keys, m), F32)
    big_spec = pl.BlockSpec((1, n_keys, tm), lambda i, h: (h, 0, i))
    return pl.pallas_call(
        _router_kernel,
        grid=(m // tm, heads),
        in_specs=[pl.BlockSpec((d, tm), lambda i, h: (0, i)),
                  pl.BlockSpec((2 * sub, d), lambda i, h: (h, 0)),
                  pl.BlockSpec((1, 2, n_keys, sub), lambda i, h: (h, 0, 0, 0))],
        out_specs=[big_spec, big_spec, big_spec, big_spec,
                   pl.BlockSpec((1, 1, tm), lambda i, h: (h, 0, i))],
        out_shape=[big, big, big, big, jax.ShapeDtypeStruct((heads, 1, m), F32)],
        scratch_shapes=[pltpu.VMEM((PEER_TOPK, tm), F32)] * 3,
        compiler_params=_params(("parallel", "arbitrary"),
                                2 * d * tm * 2 + 2 * 2 * sub * d * 2 + 8 * n_keys * tm * 4 + 16 * 2**20),
        name="peer_router",
    )(ht, wq_t, keys)


def _peer_dense_kernel(h_ref, u_ref, vt_ref, sc_ref, s0_ref, s1_ref, a_ref, b_ref, tau_ref, o_ref, g_sc):
    e = pl.program_id(1)
    heads, n_keys, tm = s1_ref.shape
    te = u_ref.shape[0]
    sub = min(te, PEER_SUBTILE)

    @pl.when(e == 0)
    def _():
        o_ref[...] = jnp.zeros(o_ref.shape, F32)

    for ii in range(te // n_keys):
        i = e * (te // n_keys) + ii
        gates = jnp.zeros((n_keys, tm), F32)
        for h in range(heads):
            pair = s1_ref[h] + s0_ref[h, pl.ds(i, 1), :]
            gates = gates + jnp.where(pair >= tau_ref[h], b_ref[h], 0.0) * a_ref[h, pl.ds(i, 1), :]
        g_sc[ii * n_keys:(ii + 1) * n_keys, :] = gates
    nsub = te // sub
    act_scale, w_scale = sc_ref[0:1, :], sc_ref[1:2, :]
    acts = [jnp.dot(u_ref[t * sub:(t + 1) * sub, :], h_ref[...], preferred_element_type=F32) * act_scale
            for t in range(nsub)]
    for t in range(nsub):
        gel = 0.5 * acts[t] * (1.0 + lax.erf(acts[t] * (1.0 / math.sqrt(2.0))))
        w = (g_sc[t * sub:(t + 1) * sub, :] * gel * w_scale).astype(vt_ref.dtype)
        o_ref[...] += jnp.dot(vt_ref[:, t * sub:(t + 1) * sub], w, preferred_element_type=F32)


def _peer_dense(ht, u, vt, scales, s0, s1, a, bf, tau, *, tm=512, te=512):
    d, m = ht.shape
    n_exp = u.shape[0]
    heads, n_keys, _ = s0.shape
    tm = min(tm, m)
    once = pl.Buffered(1)
    tok = lambda i, e: (0, 0, i)
    big_spec = pl.BlockSpec((heads, n_keys, tm), tok, pipeline_mode=once)
    return pl.pallas_call(
        _peer_dense_kernel,
        grid=(m // tm, n_exp // te),
        in_specs=[pl.BlockSpec((d, tm), lambda i, e: (0, i), pipeline_mode=once),
                  pl.BlockSpec((te, d), lambda i, e: (e, 0)),
                  pl.BlockSpec((d, te), lambda i, e: (0, e)),
                  pl.BlockSpec((2, tm), lambda i, e: (0, i), pipeline_mode=once),
                  big_spec, big_spec, big_spec, big_spec,
                  pl.BlockSpec((heads, 1, tm), tok, pipeline_mode=once)],
        out_specs=pl.BlockSpec((d, tm), lambda i, e: (0, i)),
        out_shape=jax.ShapeDtypeStruct((d, m), F32),
        scratch_shapes=[pltpu.VMEM((te, tm), F32)],
        compiler_params=_params(("parallel", "arbitrary"),
                                d * tm * 2 + 4 * te * d * 2 + 2 * d * tm * 4 + 4 * heads * n_keys * tm * 4
                                + 6 * te * tm * 4 + d * tm * 4),
        name="peer_dense",
    )(ht, u, vt, scales, s0, s1, a, bf, tau)


def _final_kernel(x_ref, pt_ref, ps_ref, g_ref, gain_ref, o_ref):
    x = x_ref[...] + g_ref[0] * (pt_ref[...] * ps_ref[...]).T
    ms = jnp.mean(x * x, axis=-1, keepdims=True)
    o_ref[...] = x * lax.rsqrt(ms + EPS) * gain_ref[...]


def _final(x1, pt, pscale, gate, gain, *, t, tm=256):
    m, d = x1.shape
    tm = min(tm, t)
    return pl.pallas_call(
        _final_kernel,
        grid=(m // tm,),
        in_specs=[pl.BlockSpec((tm, d), lambda i: (i, 0)),
                  pl.BlockSpec((d, tm), lambda i: (0, i)),
                  pl.BlockSpec((1, tm), lambda i: (0, i)),
                  pl.BlockSpec((1, 1, d), lambda i: ((i * tm) // t, 0, 0)),
                  pl.BlockSpec((1, d), lambda i: (0, 0))],
        out_specs=pl.BlockSpec((tm, d), lambda i: (i, 0)),
        out_shape=jax.ShapeDtypeStruct((m, d), F32),
        compiler_params=_params(("parallel",), 8 * tm * d * 4 + 4 * 2**20),
        name="final_norm",
    )(x1, pt, pscale, gate, gain.reshape(1, d))


def _trunk(x, ada, w, depth_index=0):
    b, t, d = x.shape
    nh = d // (2 * LANES)
    lambda_init = 0.8 - 0.6 * math.exp(-0.3 * depth_index)
    sh1, sc1, g1, sh2, sc2, g2 = [a.reshape(b, 1, d) for a in jnp.split(ada, 6, axis=-1)]

    h = _norm_mod(x, w["norm1_gain"], sc1, sh1, transpose=False)
    p = _inproj(h, w["w_in"], w["colscale"])
    ma = _diff_attention(p, w["diff_lambda"], w["diff_subln_gain"], b=b, t=t, nh=nh, lambda_init=lambda_init)
    md = _dil_attention(p, w["dil_nslopes"], b=b, t=t, nh=nh)
    x1 = _outproj(ma.reshape(b * t, nh * LANES), md.reshape(b * t, nh * LANES), w["w_out"],
                  x.reshape(b * t, d), g1, t=t)

    ht, ht8, stat = _norm_mod(x1.reshape(b, t, d), w["norm2_gain"], sc2, sh2, transpose=True)
    s0, s1, a, bf, tau = _router(ht, w["wq_t"], w["sub_keys"])
    act_scale = stat[0:1] * w["u_inv_scale"]
    w_scale = F8_TARGET / jnp.maximum(stat[1:2] * w["u_row_norm"], TINY)
    pt = _peer_dense(ht8, w["peer_u"], w["peer_vt"], jnp.concatenate([act_scale, w_scale], axis=0),
                     s0, s1, a, bf, tau)
    y = _final(x1, pt, w["v_inv_scale"] / w_scale, g2, w["final_gain"], t=t)
    return y.reshape(b, t, d)


def kernel(x_prompt, x_sample, c_prompt, c_sample, norm1_gain, norm2_gain, w_ada, b_ada, w_in, diff_lambda,
           diff_subln_gain, w_out, peer_w_query, peer_sub_keys, peer_u, peer_v, final_gain):
    d = x_prompt.shape[-1]
    nh = d // (2 * LANES)
    bp, bs = c_prompt.shape[0], c_sample.shape[0]
    assert norm1_gain.shape[0] == 1, "single-layer trunk"
    assert peer_sub_keys.shape[3] == LANES and peer_sub_keys.shape[4] == LANES

    rows = -(-(bp + bs) // 8) * 8
    c_all = jnp.concatenate([c_prompt, c_sample, jnp.zeros((rows - bp - bs, d), F32)], axis=0)
    ada = _ada(c_all, w_ada[0], b_ada[0])

    colscale = np.ones((1, w_in.shape[-1]), np.float32)
    colscale[0, :nh * LANES] = DIFF_HALF_DIM ** -0.5 * math.log2(math.e)
    colscale[0, 3 * nh * LANES:4 * nh * LANES] = LANES ** -0.5 * math.log2(math.e)
    lane_bcast = lambda v: jnp.asarray(np.broadcast_to(np.asarray(v, np.float32)[:, None, None], (nh, 1, LANES)))
    w = dict(
        norm1_gain=norm1_gain[0], norm2_gain=norm2_gain[0], final_gain=final_gain,
        w_in=w_in[0].astype(BF16), colscale=jnp.asarray(colscale),
        dil_nslopes=lane_bcast(-_alibi_slopes(nh) * math.log2(math.e)),
        diff_lambda=diff_lambda[0], diff_subln_gain=diff_subln_gain[0],
        w_out=w_out[0].astype(BF16),
        wq_t=peer_w_query[0].T.astype(BF16), sub_keys=peer_sub_keys[0].astype(BF16),
    )
    u_amax = jnp.maximum(jnp.max(jnp.abs(peer_u[0])), TINY)
    v_amax = jnp.maximum(jnp.max(jnp.abs(peer_v[0])), TINY)
    w.update(
        peer_u=(peer_u[0] * (F8_TARGET / u_amax)).astype(F8), u_inv_scale=u_amax / F8_TARGET,
        u_row_norm=jnp.sqrt(jnp.max(jnp.sum(peer_u[0] * peer_u[0], axis=1))),
        peer_vt=(peer_v[0].T * (F8_TARGET / v_amax)).astype(F8), v_inv_scale=v_amax / F8_TARGET,
    )
    y_prompt = _trunk(x_prompt, ada[:bp], w)
    y_sample = _trunk(x_sample, ada[bp:bp + bs], w)
    return (y_prompt, y_sample)
```

```python
import functools
import math

import ml_dtypes
import numpy as np
import jax
import jax.numpy as jnp
from jax import lax
from jax.experimental import pallas as pl
from jax.experimental.pallas import tpu as pltpu

F32 = jnp.float32
BF16 = jnp.bfloat16
F8 = jnp.float8_e4m3fn
F8_TARGET = 240.0
TINY = 1e-30
EPS = 1e-6
MASK_VALUE = -1e30
LANES = 128
VMEM_CAP_BYTES = 60000 * 1024
DIFF_HALF_DIM = 64
DIFF_KEY_BLOCK = 256
DIFF_SKIP_MARGIN = 140.0
DIFF_PAD_ROWS = 16
DIL_CONFIGS = ((128, 1), (512, 4), (2048, 16))
DIL_UNROLL = 8
PEER_TOPK = 16
PEER_SUBTILE = 256
NT_DIMS = (((1,), (1,)), ((), ()))


def _params(semantics, vmem_bytes):
    return pltpu.CompilerParams(
        dimension_semantics=semantics,
        vmem_limit_bytes=int(min(max(vmem_bytes, 16 * 2**20), VMEM_CAP_BYTES)))


def _alibi_slopes(n):
    return 2.0 ** (-8.0 * np.arange(1, n + 1) / n)


def _ada_kernel(c_ref, w_ref, b_ref, o_ref):
    c = c_ref[...]
    a = (c / (1.0 + jnp.exp(-c))).astype(BF16)
    o_ref[...] = jnp.dot(a, w_ref[...].astype(BF16), preferred_element_type=F32) + b_ref[...]


def _ada(c_all, w_ada, b_ada):
    rows, d = c_all.shape
    n = w_ada.shape[1]
    tn = 512
    return pl.pallas_call(
        _ada_kernel,
        grid=(n // tn,),
        in_specs=[pl.BlockSpec((rows, d), lambda j: (0, 0)),
                  pl.BlockSpec((d, tn), lambda j: (0, j)),
                  pl.BlockSpec((1, tn), lambda j: (0, j))],
        out_specs=pl.BlockSpec((rows, tn), lambda j: (0, j)),
        out_shape=jax.ShapeDtypeStruct((rows, n), F32),
        compiler_params=_params(("arbitrary",), 2 * d * tn * 4 + d * tn * 2 + 4 * 2**20),
        name="ada",
    )(c_all, w_ada, b_ada.reshape(1, n))


def _norm_mod_kernel(x_ref, gain_ref, sc_ref, sh_ref, o_ref, *rest, transpose):
    x = x_ref[0]
    ms = jnp.mean(x * x, axis=-1, keepdims=True)
    y = x * lax.rsqrt(ms + EPS) * gain_ref[...]
    h = y * (1.0 + sc_ref[0]) + sh_ref[0]
    if transpose:
        q_ref, stat_ref = rest
        ht = h.T
        o_ref[...] = ht.astype(o_ref.dtype)
        amax = jnp.maximum(jnp.max(jnp.abs(ht), axis=0, keepdims=True), TINY)
        q_ref[...] = (ht * (F8_TARGET / amax)).astype(q_ref.dtype)
        stat_ref[0:1, :] = amax * (1.0 / F8_TARGET)
        stat_ref[1:2, :] = jnp.sqrt(jnp.sum(ht * ht, axis=0, keepdims=True))
    else:
        o_ref[...] = h.astype(o_ref.dtype)


def _norm_mod(x, gain, sc, sh, *, transpose, tm=256):
    b, t, d = x.shape
    m = b * t
    tm = min(tm, t)
    nt = t // tm
    if transpose:
        col = lambda bi, i: (0, bi * nt + i)
        out_spec = [pl.BlockSpec((d, tm), col), pl.BlockSpec((d, tm), col), pl.BlockSpec((2, tm), col)]
        out_shape = [jax.ShapeDtypeStruct((d, m), BF16), jax.ShapeDtypeStruct((d, m), F8),
                     jax.ShapeDtypeStruct((2, m), F32)]
    else:
        out_spec = pl.BlockSpec((tm, d), lambda bi, i: (bi * nt + i, 0))
        out_shape = jax.ShapeDtypeStruct((m, d), BF16)
    return pl.pallas_call(
        functools.partial(_norm_mod_kernel, transpose=transpose),
        grid=(b, nt),
        in_specs=[pl.BlockSpec((1, tm, d), lambda bi, i: (bi, i, 0)),
                  pl.BlockSpec((1, d), lambda bi, i: (0, 0)),
                  pl.BlockSpec((1, 1, d), lambda bi, i: (bi, 0, 0)),
                  pl.BlockSpec((1, 1, d), lambda bi, i: (bi, 0, 0))],
        out_specs=out_spec,
        out_shape=out_shape,
        compiler_params=_params(("parallel", "parallel"), 8 * tm * d * 4 + 4 * 2**20),
        name="norm_mod_t" if transpose else "norm_mod",
    )(x, gain.reshape(1, d), sc, sh)


def _inproj_kernel(a_ref, w_ref, cs_ref, o_ref):
    acc = jnp.dot(a_ref[...], w_ref[...], preferred_element_type=F32) * cs_ref[...]
    for c in range(o_ref.shape[0]):
        o_ref[c] = acc[:, c * LANES:(c + 1) * LANES].astype(o_ref.dtype)


def _inproj(h, w, colscale, *, tm=1024, tn=512):
    m, k = h.shape
    n = w.shape[1]
    tm, tn = min(tm, m), min(tn, n)
    nb = tn // LANES
    return pl.pallas_call(
        _inproj_kernel,
        grid=(m // tm, n // tn),
        in_specs=[pl.BlockSpec((tm, k), lambda i, j: (i, 0)),
                  pl.BlockSpec((k, tn), lambda i, j: (0, j)),
                  pl.BlockSpec((1, tn), lambda i, j: (0, j))],
        out_specs=pl.BlockSpec((nb, tm, LANES), lambda i, j: (j, i, 0)),
        out_shape=jax.ShapeDtypeStruct((n // LANES, m, LANES), BF16),
        compiler_params=_params(("parallel", "arbitrary"),
                                2 * (tm * k + k * tn + tm * tn) * 2 + 2 * tm * tn * 4 + 2 * 2**20),
        name="inproj",
    )(h, w, colscale)


def _alibi_tables(nh, tq, tk):
    bf = lambda a: np.asarray(a, np.float64).astype(ml_dtypes.bfloat16).astype(np.float64)
    slope = _alibi_slopes(nh) * math.log2(math.e)
    hi = bf(slope)
    lo = bf(slope - hi)
    augq = np.zeros((nh, tq, LANES), np.float64)
    augk = np.zeros((nh, 2, tk, LANES), np.float64)
    qpos, kpos = np.arange(tq), np.arange(tk)
    for h in range(nh):
        qcols = [qpos // 16, qpos // 16, qpos % 16, qpos % 16,
                 16 * hi[h] + 0 * qpos, 16 * lo[h] + 0 * qpos, hi[h] + 0 * qpos, lo[h] + 0 * qpos]
        kcols = [-16 * hi[h] + 0 * kpos, -16 * lo[h] + 0 * kpos, -hi[h] + 0 * kpos, -lo[h] + 0 * kpos,
                 kpos // 16, kpos // 16, kpos % 16, kpos % 16]
        for c in range(8):
            augq[h, :, c] = qcols[c]
            augk[h, 0, :, c] = kcols[c]
            augk[h, 1, :, c] = -kcols[c]
    nslope = np.broadcast_to(-(hi + lo)[:, None, None], (nh, 1, LANES))
    return (jnp.asarray(augq, BF16), jnp.asarray(augk, BF16), jnp.asarray(nslope, F32))


def _diff_attn_kernel(q_ref, k_ref, v_ref, augq_ref, augk_ref, ns_ref, sl_ref, lp_ref, gain_ref, o_ref,
                      vt_sc, st0, st1, p0, p1, al0, al1, m_sc, acc_sc, kmax_sm, *, tile, lambda_init):
    qi = pl.program_id(2)
    t = k_ref.shape[1]
    n = t // tile
    kb = min(tile, DIFF_KEY_BLOCK)
    groups = tile // LANES

    @pl.when(qi == 0)
    def _():
        ones_row = jnp.where(lax.broadcasted_iota(jnp.int32, (DIFF_PAD_ROWS, tile), 0) == 0, 1.0, 0.0)
        for c in range(n):
            vt_sc[c, :LANES, :] = v_ref[0, c * tile:(c + 1) * tile, :].astype(F32).T.astype(BF16)
            vt_sc[c, LANES:, :] = ones_row.astype(BF16)
            kf = k_ref[0, c * tile:(c + 1) * tile, :].astype(F32)
            kmax_sm[c] = jnp.max(jnp.sqrt(jnp.sum(kf * kf, axis=1, keepdims=True)))

    q = q_ref[0]
    lane = lax.broadcasted_iota(jnp.int32, q.shape, 1)
    zero = jnp.zeros_like(q)
    qz = jnp.concatenate([jnp.where(lane < DIFF_HALF_DIM, q, zero),
                          jnp.where(lane >= DIFF_HALF_DIM, q, zero)], axis=0)
    aq = augq_ref[0]
    qx = jnp.concatenate([qz, jnp.concatenate([aq, aq], axis=0)], axis=1)
    qxt = qx.astype(F32).T.astype(BF16)
    nslope = ns_ref[0][:1, :1]
    m_sc[...] = jnp.full(m_sc.shape, MASK_VALUE, F32)
    acc_sc[...] = jnp.zeros(acc_sc.shape, F32)
    st_sc, p_sc, al_sc = (st0, st1), (p0, p1), (al0, al1)

    qf = q.astype(F32)
    bq = jnp.max(jnp.sqrt(jnp.sum(qf * qf, axis=1, keepdims=True)))
    slope = sl_ref[pl.program_id(1)]
    floor_m = -(bq * kmax_sm[qi]) - DIFF_SKIP_MARGIN

    def dead(j):
        gap = ((jnp.abs(qi - j) - 1) * tile + 1).astype(F32)
        return bq * kmax_sm[j] - slope * gap < floor_m

    lo = jnp.int32(0)
    for j in range(n - 1):
        lo = lo + ((lo == j) & (j < qi) & dead(j)).astype(jnp.int32)
    hi = jnp.int32(n - 1)
    for j in range(n - 1, 0, -1):
        hi = hi - ((hi == j) & (j > qi) & dead(j)).astype(jnp.int32)
    odd = ((hi - lo + 1 - n) & 1) == 1
    lo, hi = jnp.where(odd & (lo > 0), lo - 1, lo), jnp.where(odd & (lo == 0), hi + 1, hi)
    cnt = hi - lo + 1

    def chunk_at(pos):
        nxt = lo + pos - 1
        return jnp.where(pos == 0, qi, nxt + (nxt >= qi).astype(jnp.int32))

    def scores(j, side):
        rows = pl.ds(pl.multiple_of(j * tile, tile), tile)
        kx = jnp.concatenate([k_ref[0, rows, :], augk_ref[0, side]], axis=1)
        return jnp.dot(kx, qxt, preferred_element_type=F32)

    def softmax(j, buf):
        offset = nslope * (jnp.abs(qi - j) * tile).astype(F32)
        for g in range(2 * groups):
            cols = slice(g * LANES, (g + 1) * LANES)
            m_old = m_sc[:, cols]
            mx = m_old
            for r in range(tile // kb):
                s = st_sc[buf][r * kb:(r + 1) * kb, cols]
                mx = jnp.maximum(mx, jnp.max(s, axis=0, keepdims=True) + offset)
            al_sc[buf][:, cols] = jnp.exp2(m_old - mx)
            m_sc[:, cols] = mx
            sub = mx - offset
            for r in range(tile // kb):
                rows = slice(r * kb, (r + 1) * kb)
                p_sc[buf][rows, cols] = jnp.exp2(st_sc[buf][rows, cols] - sub).astype(BF16)

    def accumulate(j, buf):
        acc_sc[...] = al_sc[buf][...] * acc_sc[...] + jnp.dot(vt_sc[j], p_sc[buf][...], preferred_element_type=F32)

    p_sc[1][...] = jnp.zeros(p_sc[1].shape, BF16)
    al_sc[1][...] = jnp.ones(al_sc[1].shape, F32)
    st_sc[0][...] = jnp.minimum(scores(qi, 0), scores(qi, 1))

    def step(pos, buf):
        nxt = chunk_at(pos + 1)
        st_sc[1 - buf][...] = scores(nxt, (nxt > qi).astype(jnp.int32))
        softmax(chunk_at(pos), buf)
        accumulate(chunk_at(jnp.maximum(pos - 1, 0)), 1 - buf)

    def body(i, carry):
        step(2 * i, 0)
        step(2 * i + 1, 1)
        return carry

    lax.fori_loop(0, (cnt - 1) // 2, body, 0)
    if (n - 1) % 2:
        step(cnt - 2, (n - 2) & 1)
    last = (n - 1) & 1
    softmax(chunk_at(cnt - 1), last)
    if n > 1:
        accumulate(chunk_at(jnp.maximum(cnt - 2, 0)), 1 - last)
    accumulate(chunk_at(cnt - 1), last)

    lp = lp_ref[...]
    lam = (jnp.exp(jnp.sum(lp[0:1] * lp[1:2], axis=1, keepdims=True))
           - jnp.exp(jnp.sum(lp[2:3] * lp[3:4], axis=1, keepdims=True)) + lambda_init)
    on = acc_sc[:LANES, :] / acc_sc[LANES:LANES + 1, :]
    o = on[:, :tile] - lam * on[:, tile:]
    o = o * lax.rsqrt(jnp.mean(o * o, axis=0, keepdims=True) + EPS) * (1.0 - lambda_init)
    o_ref[0] = (o.T * gain_ref[...]).astype(o_ref.dtype)


def _diff_attention(p, lam_params, subln_gain, *, b, t, nh, lambda_init, tile=512):
    tile = min(tile, t)
    nq = t // tile
    augq, augk, nslopes = _alibi_tables(nh, tile, tile)
    return pl.pallas_call(
        functools.partial(_diff_attn_kernel, tile=tile, lambda_init=lambda_init),
        grid=(b, nh, nq),
        in_specs=[pl.BlockSpec((1, tile, LANES), lambda bi, h, qi: (h, bi * nq + qi, 0)),
                  pl.BlockSpec((1, t, LANES), lambda bi, h, qi: (nh + h, bi, 0)),
                  pl.BlockSpec((1, t, LANES), lambda bi, h, qi: (2 * nh + h, bi, 0)),
                  pl.BlockSpec((1, tile, LANES), lambda bi, h, qi: (h, 0, 0)),
                  pl.BlockSpec((1, 2, tile, LANES), lambda bi, h, qi: (h, 0, 0, 0)),
                  pl.BlockSpec((1, 1, LANES), lambda bi, h, qi: (h, 0, 0)),
                  pl.BlockSpec(memory_space=pltpu.SMEM),
                  pl.BlockSpec(lam_params.shape, lambda bi, h, qi: (0, 0)),
                  pl.BlockSpec((1, LANES), lambda bi, h, qi: (0, 0))],
        out_specs=pl.BlockSpec((1, tile, LANES), lambda bi, h, qi: (bi, qi, h)),
        out_shape=jax.ShapeDtypeStruct((b, t, nh * LANES), BF16),
        scratch_shapes=[pltpu.VMEM((t // tile, LANES + DIFF_PAD_ROWS, tile), BF16),
                        pltpu.VMEM((tile, 2 * tile), F32), pltpu.VMEM((tile, 2 * tile), F32),
                        pltpu.VMEM((tile, 2 * tile), BF16), pltpu.VMEM((tile, 2 * tile), BF16),
                        pltpu.VMEM((1, 2 * tile), F32), pltpu.VMEM((1, 2 * tile), F32),
                        pltpu.VMEM((1, 2 * tile), F32),
                        pltpu.VMEM((LANES + DIFF_PAD_ROWS, 2 * tile), F32),
                        pltpu.SMEM((t // tile,), F32)],
        compiler_params=_params(("parallel", "parallel", "arbitrary"),
                                5 * t * LANES * 2 + 2 * tile * tile * 20 + 4 * LANES * tile * 4 + 8 * 2**20),
        name="diff_attn",
    )(p, p, p, augq, augk, nslopes, -nslopes[:, 0, 0], lam_params, subln_gain.reshape(1, LANES))


def _dil_attn_kernel(q_ref, k_ref, v_ref, ns_ref, o_ref, qf, kf, vf, qr, kr, vr, ob, lb, bt, *, blk):
    t = q_ref.shape[1]
    nslope = ns_ref[0][:, :1]
    if any(g > 1 for _, g in DIL_CONFIGS):
        qf[...] = q_ref[0].astype(F32)
        kf[...] = k_ref[0].astype(F32)
        vf[...] = v_ref[0].astype(F32)

    for n, (window, g) in enumerate(DIL_CONFIGS):
        radius = window // (2 * g)
        ln = t // g
        wk = min(blk + 2 * radius, ln)
        nblk = ln // blk
        if g > 1:
            for rho in range(g):
                rows = pl.ds(rho, ln, stride=g)
                qr[rho * ln:(rho + 1) * ln, :] = qf[rows, :].astype(BF16)
                kr[rho * ln:(rho + 1) * ln, :] = kf[rows, :].astype(BF16)
                vr[rho * ln:(rho + 1) * ln, :] = vf[rows, :].astype(BF16)
            qs, ks, vs = qr, kr, vr
        else:
            qs, ks, vs = q_ref.at[0], k_ref.at[0], v_ref.at[0]
        rel0 = (lax.broadcasted_iota(jnp.int32, (blk, wk), 1)
                - lax.broadcasted_iota(jnp.int32, (blk, wk), 0))
        for idx, shift in enumerate((0, -radius, blk - wk)):
            ar = jnp.abs(rel0 + shift)
            bt[idx, :, :wk] = jnp.where(ar <= radius, (nslope * float(g)) * ar.astype(F32), MASK_VALUE)

        def body(i, carry, n=n, g=g, radius=radius, ln=ln, wk=wk, nblk=nblk, qs=qs, ks=ks, vs=vs):
            rho = i // nblk
            bi = i % nblk
            l0 = bi * blk
            ws = jnp.clip(l0 - radius, 0, ln - wk)
            base = rho * ln
            qb = qs[pl.ds(pl.multiple_of(base + l0, blk), blk), :]
            kw = ks[pl.ds(pl.multiple_of(base + ws, radius), wk), :]
            vw = vs[pl.ds(pl.multiple_of(base + ws, radius), wk), :]
            place = jnp.where(bi == 0, 0, jnp.where(bi == nblk - 1, 2, 1))
            s = lax.dot_general(qb, kw, NT_DIMS, preferred_element_type=F32) + bt[place, :, :wk]
            mx = jnp.max(s, axis=1, keepdims=True)
            e = jnp.exp2(s - mx)
            den = jnp.sum(e, axis=1, keepdims=True)
            o = jnp.dot(e.astype(BF16), vw, preferred_element_type=F32) / den
            lse = mx + jnp.log2(den)
            rows = pl.ds(l0 * g + rho, blk, stride=g) if g > 1 else pl.ds(pl.multiple_of(l0, blk), blk)
            ob[n, rows, :] = o
            lb[n, rows, :] = jnp.broadcast_to(lse, (blk, LANES))
            return carry

        lax.fori_loop(0, g * nblk, body, 0, unroll=min(DIL_UNROLL, g * nblk))

    chunk = min(512, t)

    def merge(i, carry):
        rows = pl.ds(pl.multiple_of(i * chunk, chunk), chunk)
        l0, l1, l2 = lb[0, rows, :], lb[1, rows, :], lb[2, rows, :]
        mx = jnp.maximum(jnp.maximum(l0, l1), l2)
        w0, w1, w2 = jnp.exp2(l0 - mx), jnp.exp2(l1 - mx), jnp.exp2(l2 - mx)
        o = (w0 * ob[0, rows, :] + w1 * ob[1, rows, :] + w2 * ob[2, rows, :]) / (w0 + w1 + w2)
        o_ref[0, rows, :] = o.astype(o_ref.dtype)
        return carry

    lax.fori_loop(0, t // chunk, merge, 0)


def _dil_attention(p, nslopes, *, b, t, nh, blk=128):
    nbr = len(DIL_CONFIGS)
    return pl.pallas_call(
        functools.partial(_dil_attn_kernel, blk=blk),
        grid=(b, nh),
        in_specs=[pl.BlockSpec((1, t, LANES), lambda bi, h: (3 * nh + h, bi, 0)),
                  pl.BlockSpec((1, t, LANES), lambda bi, h: (4 * nh + h, bi, 0)),
                  pl.BlockSpec((1, t, LANES), lambda bi, h: (5 * nh + h, bi, 0)),
                  pl.BlockSpec((1, 1, LANES), lambda bi, h: (h, 0, 0))],
        out_specs=pl.BlockSpec((1, t, LANES), lambda bi, h: (bi, 0, h)),
        out_shape=jax.ShapeDtypeStruct((b, t, nh * LANES), BF16),
        scratch_shapes=[pltpu.VMEM((t, LANES), F32)] * 3 + [pltpu.VMEM((t, LANES), BF16)] * 3
                       + [pltpu.VMEM((nbr, t, LANES), F32)] * 2 + [pltpu.VMEM((3, blk, 2 * blk), F32)],
        compiler_params=_params(("parallel", "parallel"),
                                8 * t * LANES * 2 + 3 * t * LANES * 6 + 2 * nbr * t * LANES * 4 + 8 * 2**20),
        name="dil_attn",
    )(p, p, p, nslopes)


def _outproj_kernel(ma_ref, md_ref, wt_ref, wb_ref, x_ref, g_ref, o_ref):
    acc = (jnp.dot(ma_ref[...], wt_ref[...], preferred_element_type=F32)
           + jnp.dot(md_ref[...], wb_ref[...], preferred_element_type=F32))
    o_ref[...] = x_ref[...] + g_ref[0] * acc


def _outproj(ma, md, w_out, x2d, gate, *, t, tm=1024, tn=512):
    m, kh = ma.shape
    n = w_out.shape[1]
    tm, tn = min(tm, t), min(tn, n)
    return pl.pallas_call(
        _outproj_kernel,
        grid=(m // tm, n // tn),
        in_specs=[pl.BlockSpec((tm, kh), lambda i, j: (i, 0)),
                  pl.BlockSpec((tm, kh), lambda i, j: (i, 0)),
                  pl.BlockSpec((kh, tn), lambda i, j: (0, j)),
                  pl.BlockSpec((kh, tn), lambda i, j: (1, j)),
                  pl.BlockSpec((tm, tn), lambda i, j: (i, j)),
                  pl.BlockSpec((1, 1, tn), lambda i, j: ((i * tm) // t, 0, j))],
        out_specs=pl.BlockSpec((tm, tn), lambda i, j: (i, j)),
        out_shape=jax.ShapeDtypeStruct((m, n), F32),
        compiler_params=_params(("parallel", "arbitrary"),
                                2 * (2 * tm * kh + 2 * kh * tn) * 2 + 5 * tm * tn * 4 + 2 * 2**20),
        name="outproj",
    )(ma, md, w_out, w_out, x2d, gate)


def _take_top_rows(s, rows_ref, count):
    for r in range(count):
        mx = jnp.max(s, axis=0, keepdims=True)
        rows_ref[r:r + 1, :] = mx
        s = jnp.where(s >= mx, -jnp.inf, s)


def _router_kernel(h_ref, wq_ref, keys_ref, thr_ref, s1_ref, a_ref, b_ref, ra_ref, rb_ref, rc_ref):
    k = PEER_TOPK
    sub = keys_ref.shape[2]
    qt = jnp.dot(wq_ref[...], h_ref[...], preferred_element_type=F32)
    s0 = jnp.dot(keys_ref[0, 0], qt[:sub].astype(BF16), preferred_element_type=F32)
    s1 = jnp.dot(keys_ref[0, 1], qt[sub:].astype(BF16), preferred_element_type=F32)
    _take_top_rows(s0, ra_ref, k)
    _take_top_rows(s1, rb_ref, k)
    ra, rb = ra_ref[...], rb_ref[...]
    row8 = lax.broadcasted_iota(jnp.int32, (8, ra.shape[1]), 0)
    pieces = [ra[0:1] + rb, ra[1:2] + rb[0:8]]
    for r in range(2, 8):
        pieces.append(jnp.where(row8 < k // (r + 1), ra[r:r + 1] + rb[0:8], -jnp.inf))
    pieces.append(ra[8:16] + rb[0:1])
    cand = jnp.concatenate(pieces, axis=0)
    _take_top_rows(cand, rc_ref, k)
    tau = rc_ref[k - 1:k, :]
    top = ra[0:1] + rb[0:1]
    z = jnp.sum(jnp.where(cand >= tau, jnp.exp(cand - top), 0.0), axis=0, keepdims=True)
    thr = jnp.full(s0.shape, jnp.inf, F32)
    for r in range(k):
        row_thr = jnp.min(jnp.where(ra[r:r + 1] + rb >= tau, rb, jnp.inf), axis=0, keepdims=True)
        thr = jnp.where(s0 == ra[r:r + 1], row_thr, thr)
    thr_ref[0] = thr
    s1_ref[0] = s1
    a_ref[0] = jnp.exp(s0 - ra[0:1]) / z
    b_ref[0] = jnp.exp(s1 - rb[0:1])


def _router(ht, wq_t, keys, *, tm=512):
    d, m = ht.shape
    heads, _, n_keys, sub = keys.shape
    tm = min(tm, m)
    big = jax.ShapeDtypeStruct((heads, n_keys, m), F32)
    big_spec = pl.BlockSpec((1, n_keys, tm), lambda i, h: (h, 0, i))
    return pl.pallas_call(
        _router_kernel,
        grid=(m // tm, heads),
        in_specs=[pl.BlockSpec((d, tm), lambda i, h: (0, i)),
                  pl.BlockSpec((2 * sub, d), lambda i, h: (h, 0)),
                  pl.BlockSpec((1, 2, n_keys, sub), lambda i, h: (h, 0, 0, 0))],
        out_specs=[big_spec, big_spec, big_spec, big_spec],
        out_shape=[big, big, big, big],
        scratch_shapes=[pltpu.VMEM((PEER_TOPK, tm), F32)] * 3,
        compiler_params=_params(("parallel", "arbitrary"),
                                2 * d * tm * 2 + 2 * 2 * sub * d * 2 + 8 * n_keys * tm * 4 + 16 * 2**20),
        name="peer_router",
    )(ht, wq_t, keys)


def _peer_dense_kernel(h_ref, u_ref, vt_ref, sc_ref, thr_ref, s1_ref, a_ref, b_ref, o_ref, g_sc):
    e = pl.program_id(1)
    heads, n_keys, tm = s1_ref.shape
    te = u_ref.shape[0]
    sub = min(te, PEER_SUBTILE)

    @pl.when(e == 0)
    def _():
        o_ref[...] = jnp.zeros(o_ref.shape, F32)

    for ii in range(te // n_keys):
        i = e * (te // n_keys) + ii
        gates = jnp.zeros((n_keys, tm), F32)
        for h in range(heads):
            gates = gates + (jnp.where(s1_ref[h] >= thr_ref[h, pl.ds(i, 1), :], b_ref[h], 0.0)
                             * a_ref[h, pl.ds(i, 1), :])
        g_sc[ii * n_keys:(ii + 1) * n_keys, :] = gates
    nsub = te // sub
    act_scale, w_scale = sc_ref[0:1, :], sc_ref[1:2, :]
    acts = [jnp.dot(u_ref[t * sub:(t + 1) * sub, :], h_ref[...], preferred_element_type=F32) * act_scale
            for t in range(nsub)]
    for t in range(nsub):
        gel = 0.5 * acts[t] * (1.0 + lax.erf(acts[t] * (1.0 / math.sqrt(2.0))))
        w = (g_sc[t * sub:(t + 1) * sub, :] * gel * w_scale).astype(vt_ref.dtype)
        o_ref[...] += jnp.dot(vt_ref[:, t * sub:(t + 1) * sub], w, preferred_element_type=F32)


def _peer_dense(ht, u, vt, scales, thr, s1, a, bf, *, tm=512, te=1024):
    d, m = ht.shape
    n_exp = u.shape[0]
    heads, n_keys, _ = s1.shape
    tm = min(tm, m)
    once = pl.Buffered(1)
    tok = lambda i, e: (0, 0, i)
    big_spec = pl.BlockSpec((heads, n_keys, tm), tok, pipeline_mode=once)
    return pl.pallas_call(
        _peer_dense_kernel,
        grid=(m // tm, n_exp // te),
        in_specs=[pl.BlockSpec((d, tm), lambda i, e: (0, i), pipeline_mode=once),
                  pl.BlockSpec((te, d), lambda i, e: (e, 0)),
                  pl.BlockSpec((d, te), lambda i, e: (0, e)),
                  pl.BlockSpec((2, tm), lambda i, e: (0, i), pipeline_mode=once),
                  big_spec, big_spec, big_spec, big_spec],
        out_specs=pl.BlockSpec((d, tm), lambda i, e: (0, i)),
        out_shape=jax.ShapeDtypeStruct((d, m), F32),
        scratch_shapes=[pltpu.VMEM((te, tm), F32)],
        compiler_params=_params(("parallel", "arbitrary"),
                                d * tm * 2 + 4 * te * d * 2 + 2 * d * tm * 4 + 4 * heads * n_keys * tm * 4
                                + 6 * te * tm * 4 + d * tm * 4),
        name="peer_dense",
    )(ht, u, vt, scales, thr, s1, a, bf)


def _final_kernel(x_ref, pt_ref, ps_ref, g_ref, gain_ref, o_ref):
    x = x_ref[...] + g_ref[0] * (pt_ref[...] * ps_ref[...]).T
    ms = jnp.mean(x * x, axis=-1, keepdims=True)
    o_ref[...] = x * lax.rsqrt(ms + EPS) * gain_ref[...]


def _final(x1, pt, pscale, gate, gain, *, t, tm=256):
    m, d = x1.shape
    tm = min(tm, t)
    return pl.pallas_call(
        _final_kernel,
        grid=(m // tm,),
        in_specs=[pl.BlockSpec((tm, d), lambda i: (i, 0)),
                  pl.BlockSpec((d, tm), lambda i: (0, i)),
                  pl.BlockSpec((1, tm), lambda i: (0, i)),
                  pl.BlockSpec((1, 1, d), lambda i: ((i * tm) // t, 0, 0)),
                  pl.BlockSpec((1, d), lambda i: (0, 0))],
        out_specs=pl.BlockSpec((tm, d), lambda i: (i, 0)),
        out_shape=jax.ShapeDtypeStruct((m, d), F32),
        compiler_params=_params(("parallel",), 8 * tm * d * 4 + 4 * 2**20),
        name="final_norm",
    )(x1, pt, pscale, gate, gain.reshape(1, d))


def _trunk(x, ada, w, depth_index=0):
    b, t, d = x.shape
    nh = d // (2 * LANES)
    lambda_init = 0.8 - 0.6 * math.exp(-0.3 * depth_index)
    sh1, sc1, g1, sh2, sc2, g2 = [a.reshape(b, 1, d) for a in jnp.split(ada, 6, axis=-1)]

    h = _norm_mod(x, w["norm1_gain"], sc1, sh1, transpose=False)
    p = _inproj(h, w["w_in"], w["colscale"])
    ma = _diff_attention(p, w["diff_lambda"], w["diff_subln_gain"], b=b, t=t, nh=nh, lambda_init=lambda_init)
    md = _dil_attention(p, w["dil_nslopes"], b=b, t=t, nh=nh)
    x1 = _outproj(ma.reshape(b * t, nh * LANES), md.reshape(b * t, nh * LANES), w["w_out"],
                  x.reshape(b * t, d), g1, t=t)

    ht, ht8, stat = _norm_mod(x1.reshape(b, t, d), w["norm2_gain"], sc2, sh2, transpose=True)
    thr, s1, a, bf = _router(ht, w["wq_t"], w["sub_keys"])
    act_scale = stat[0:1] * w["u_inv_scale"]
    w_scale = F8_TARGET / jnp.maximum(stat[1:2] * w["u_row_norm"], TINY)
    pt = _peer_dense(ht8, w["peer_u"], w["peer_vt"], jnp.concatenate([act_scale, w_scale], axis=0),
                     thr, s1, a, bf)
    y = _final(x1, pt, w["v_inv_scale"] / w_scale, g2, w["final_gain"], t=t)
    return y.reshape(b, t, d)


def kernel(x_prompt, x_sample, c_prompt, c_sample, norm1_gain, norm2_gain, w_ada, b_ada, w_in, diff_lambda,
           diff_subln_gain, w_out, peer_w_query, peer_sub_keys, peer_u, peer_v, final_gain):
    d = x_prompt.shape[-1]
    nh = d // (2 * LANES)
    bp, bs = c_prompt.shape[0], c_sample.shape[0]
    assert norm1_gain.shape[0] == 1, "single-layer trunk"
    assert peer_sub_keys.shape[3] == LANES and peer_sub_keys.shape[4] == LANES

    rows = -(-(bp + bs) // 8) * 8
    c_all = jnp.concatenate([c_prompt, c_sample, jnp.zeros((rows - bp - bs, d), F32)], axis=0)
    ada = _ada(c_all, w_ada[0], b_ada[0])

    colscale = np.ones((1, w_in.shape[-1]), np.float32)
    colscale[0, :nh * LANES] = DIFF_HALF_DIM ** -0.5 * math.log2(math.e)
    colscale[0, 3 * nh * LANES:4 * nh * LANES] = LANES ** -0.5 * math.log2(math.e)
    lane_bcast = lambda v: jnp.asarray(np.broadcast_to(np.asarray(v, np.float32)[:, None, None], (nh, 1, LANES)))
    w = dict(
        norm1_gain=norm1_gain[0], norm2_gain=norm2_gain[0], final_gain=final_gain,
        w_in=w_in[0].astype(BF16), colscale=jnp.asarray(colscale),
        dil_nslopes=lane_bcast(-_alibi_slopes(nh) * math.log2(math.e)),
        diff_lambda=diff_lambda[0], diff_subln_gain=diff_subln_gain[0],
        w_out=w_out[0].astype(BF16),
        wq_t=peer_w_query[0].T.astype(BF16), sub_keys=peer_sub_keys[0].astype(BF16),
    )
    u_amax = jnp.maximum(jnp.max(jnp.abs(peer_u[0])), TINY)
    v_amax = jnp.maximum(jnp.max(jnp.abs(peer_v[0])), TINY)
    w.update(
        peer_u=(peer_u[0] * (F8_TARGET / u_amax)).astype(F8), u_inv_scale=u_amax / F8_TARGET,
        u_row_norm=jnp.sqrt(jnp.max(jnp.sum(peer_u[0] * peer_u[0], axis=1))),
        peer_vt=(peer_v[0].T * (F8_TARGET / v_amax)).astype(F8), v_inv_scale=v_amax / F8_TARGET,
    )
    y_prompt = _trunk(x_prompt, ada[:bp], w)
    y_sample = _trunk(x_sample, ada[bp:bp + bs], w)
    return (y_prompt, y_sample)
```

```python
import functools
import math

import ml_dtypes
import numpy as np
import jax
import jax.numpy as jnp
from jax import lax
from jax.experimental import pallas as pl
from jax.experimental.pallas import tpu as pltpu

F32 = jnp.float32
BF16 = jnp.bfloat16
F8 = jnp.float8_e4m3fn
F8_TARGET = 240.0
TINY = 1e-30
EPS = 1e-6
MASK_VALUE = -1e30
LANES = 128
VMEM_CAP_BYTES = 60000 * 1024
DIFF_HALF_DIM = 64
DIFF_KEY_BLOCK = 256
DIFF_SKIP_MARGIN = 140.0
DIFF_PAD_ROWS = 16
DIL_CONFIGS = ((128, 1), (512, 4), (2048, 16))
DIL_UNROLL = 8
PEER_TOPK = 16
PEER_SUBTILE = 256
NT_DIMS = (((1,), (1,)), ((), ()))


def _params(semantics, vmem_bytes):
    return pltpu.CompilerParams(
        dimension_semantics=semantics,
        vmem_limit_bytes=int(min(max(vmem_bytes, 16 * 2**20), VMEM_CAP_BYTES)))


def _alibi_slopes(n):
    return 2.0 ** (-8.0 * np.arange(1, n + 1) / n)


def _ada_kernel(c_ref, w_ref, b_ref, o_ref):
    c = c_ref[...]
    a = (c / (1.0 + jnp.exp(-c))).astype(BF16)
    o_ref[...] = jnp.dot(a, w_ref[...].astype(BF16), preferred_element_type=F32) + b_ref[...]


def _ada(c_all, w_ada, b_ada):
    rows, d = c_all.shape
    n = w_ada.shape[1]
    tn = 512
    return pl.pallas_call(
        _ada_kernel,
        grid=(n // tn,),
        in_specs=[pl.BlockSpec((rows, d), lambda j: (0, 0)),
                  pl.BlockSpec((d, tn), lambda j: (0, j)),
                  pl.BlockSpec((1, tn), lambda j: (0, j))],
        out_specs=pl.BlockSpec((rows, tn), lambda j: (0, j)),
        out_shape=jax.ShapeDtypeStruct((rows, n), F32),
        compiler_params=_params(("arbitrary",), 2 * d * tn * 4 + d * tn * 2 + 4 * 2**20),
        name="ada",
    )(c_all, w_ada, b_ada.reshape(1, n))


def _norm_mod_kernel(x_ref, gain_ref, sc_ref, sh_ref, o_ref, *rest, transpose):
    x = x_ref[0]
    ms = jnp.mean(x * x, axis=-1, keepdims=True)
    y = x * lax.rsqrt(ms + EPS) * gain_ref[...]
    h = y * (1.0 + sc_ref[0]) + sh_ref[0]
    if transpose:
        q_ref, stat_ref = rest
        ht = h.T
        o_ref[...] = ht.astype(o_ref.dtype)
        amax = jnp.maximum(jnp.max(jnp.abs(ht), axis=0, keepdims=True), TINY)
        q_ref[...] = (ht * (F8_TARGET / amax)).astype(q_ref.dtype)
        stat_ref[0:1, :] = amax * (1.0 / F8_TARGET)
        stat_ref[1:2, :] = jnp.sqrt(jnp.sum(ht * ht, axis=0, keepdims=True))
    else:
        o_ref[...] = h.astype(o_ref.dtype)


def _norm_mod(x, gain, sc, sh, *, transpose, tm=256):
    b, t, d = x.shape
    m = b * t
    tm = min(tm, t)
    nt = t // tm
    if transpose:
        col = lambda bi, i: (0, bi * nt + i)
        out_spec = [pl.BlockSpec((d, tm), col), pl.BlockSpec((d, tm), col), pl.BlockSpec((2, tm), col)]
        out_shape = [jax.ShapeDtypeStruct((d, m), BF16), jax.ShapeDtypeStruct((d, m), F8),
                     jax.ShapeDtypeStruct((2, m), F32)]
    else:
        out_spec = pl.BlockSpec((tm, d), lambda bi, i: (bi * nt + i, 0))
        out_shape = jax.ShapeDtypeStruct((m, d), BF16)
    return pl.pallas_call(
        functools.partial(_norm_mod_kernel, transpose=transpose),
        grid=(b, nt),
        in_specs=[pl.BlockSpec((1, tm, d), lambda bi, i: (bi, i, 0)),
                  pl.BlockSpec((1, d), lambda bi, i: (0, 0)),
                  pl.BlockSpec((1, 1, d), lambda bi, i: (bi, 0, 0)),
                  pl.BlockSpec((1, 1, d), lambda bi, i: (bi, 0, 0))],
        out_specs=out_spec,
        out_shape=out_shape,
        compiler_params=_params(("parallel", "parallel"), 8 * tm * d * 4 + 4 * 2**20),
        name="norm_mod_t" if transpose else "norm_mod",
    )(x, gain.reshape(1, d), sc, sh)


def _inproj_kernel(a_ref, w_ref, cs_ref, o_ref):
    acc = jnp.dot(a_ref[...], w_ref[...], preferred_element_type=F32) * cs_ref[...]
    for c in range(o_ref.shape[0]):
        o_ref[c] = acc[:, c * LANES:(c + 1) * LANES].astype(o_ref.dtype)


def _inproj(h, w, colscale, *, tm=1024, tn=512):
    m, k = h.shape
    n = w.shape[1]
    tm, tn = min(tm, m), min(tn, n)
    nb = tn // LANES
    return pl.pallas_call(
        _inproj_kernel,
        grid=(m // tm, n // tn),
        in_specs=[pl.BlockSpec((tm, k), lambda i, j: (i, 0)),
                  pl.BlockSpec((k, tn), lambda i, j: (0, j)),
                  pl.BlockSpec((1, tn), lambda i, j: (0, j))],
        out_specs=pl.BlockSpec((nb, tm, LANES), lambda i, j: (j, i, 0)),
        out_shape=jax.ShapeDtypeStruct((n // LANES, m, LANES), BF16),
        compiler_params=_params(("parallel", "arbitrary"),
                                2 * (tm * k + k * tn + tm * tn) * 2 + 2 * tm * tn * 4 + 2 * 2**20),
        name="inproj",
    )(h, w, colscale)


def _alibi_tables(nh, tq, tk):
    bf = lambda a: np.asarray(a, np.float64).astype(ml_dtypes.bfloat16).astype(np.float64)
    slope = _alibi_slopes(nh) * math.log2(math.e)
    hi = bf(slope)
    lo = bf(slope - hi)
    augq = np.zeros((nh, tq, LANES), np.float64)
    augk = np.zeros((nh, 2, tk, LANES), np.float64)
    qpos, kpos = np.arange(tq), np.arange(tk)
    for h in range(nh):
        qcols = [qpos // 16, qpos // 16, qpos % 16, qpos % 16,
                 16 * hi[h] + 0 * qpos, 16 * lo[h] + 0 * qpos, hi[h] + 0 * qpos, lo[h] + 0 * qpos]
        kcols = [-16 * hi[h] + 0 * kpos, -16 * lo[h] + 0 * kpos, -hi[h] + 0 * kpos, -lo[h] + 0 * kpos,
                 kpos // 16, kpos // 16, kpos % 16, kpos % 16]
        for c in range(8):
            for base in (0, DIFF_HALF_DIM):
                augq[h, :, base + c] = qcols[c]
                augk[h, 0, :, base + c] = kcols[c]
                augk[h, 1, :, base + c] = -kcols[c]
    nslope = np.broadcast_to(-(hi + lo)[:, None, None], (nh, 1, LANES))
    return (jnp.asarray(augq, BF16), jnp.asarray(augk, BF16), jnp.asarray(nslope, F32))


def _diff_attn_kernel(q_ref, qall_ref, k_ref, v_ref, augq_ref, augk_ref, ns_ref, sl_ref, lp_ref, gain_ref, o_ref,
                      vt_sc, st0, st1, p0, p1, al0, al1, m_sc, acc_sc, qmax_sm, kmax_sm, *, tile, lambda_init):
    qi = pl.program_id(2)
    t = k_ref.shape[1]
    n = t // tile
    kb = min(tile, DIFF_KEY_BLOCK)
    groups = tile // LANES

    @pl.when(qi == 0)
    def _():
        ones_row = jnp.where(lax.broadcasted_iota(jnp.int32, (DIFF_PAD_ROWS, tile), 0) == 0, 1.0, 0.0)
        for c in range(n):
            vt_sc[c, :LANES, :] = v_ref[0, c * tile:(c + 1) * tile, :].astype(F32).T.astype(BF16)
            vt_sc[c, LANES:, :] = ones_row.astype(BF16)
            kf = k_ref[0, c * tile:(c + 1) * tile, :].astype(F32)
            kmax_sm[c] = jnp.max(jnp.sqrt(jnp.max(jnp.sum(kf * kf, axis=1, keepdims=True), axis=0, keepdims=True)))
            qf = qall_ref[0, c * tile:(c + 1) * tile, :].astype(F32)
            qmax_sm[c] = jnp.max(jnp.sqrt(jnp.max(jnp.sum(qf * qf, axis=1, keepdims=True), axis=0, keepdims=True)))

    q = q_ref[0]
    lane = lax.broadcasted_iota(jnp.int32, q.shape, 1)
    low = lane < DIFF_HALF_DIM
    aq = augq_ref[0]
    qxt = [jnp.where(low, q, aq).astype(F32).T.astype(BF16),
           jnp.where(low, aq, q).astype(F32).T.astype(BF16)]
    nslope = ns_ref[0][:1, :1]
    m_sc[...] = jnp.full(m_sc.shape, MASK_VALUE, F32)
    acc_sc[...] = jnp.zeros(acc_sc.shape, F32)
    st_sc, p_sc, al_sc = (st0, st1), (p0, p1), (al0, al1)

    bq = qmax_sm[qi]
    slope = sl_ref[pl.program_id(1)]
    floor_m = -(bq * kmax_sm[qi]) - DIFF_SKIP_MARGIN

    def dead(j):
        gap = ((jnp.abs(qi - j) - 1) * tile + 1).astype(F32)
        return bq * kmax_sm[j] - slope * gap < floor_m

    lo = jnp.int32(0)
    for j in range(n - 1):
        lo = lo + ((lo == j) & (j < qi) & dead(j)).astype(jnp.int32)
    hi = jnp.int32(n - 1)
    for j in range(n - 1, 0, -1):
        hi = hi - ((hi == j) & (j > qi) & dead(j)).astype(jnp.int32)
    odd = ((hi - lo + 1 - n) & 1) == 1
    lo, hi = jnp.where(odd & (lo > 0), lo - 1, lo), jnp.where(odd & (lo == 0), hi + 1, hi)
    cnt = hi - lo + 1

    def chunk_at(pos):
        nxt = lo + pos - 1
        return jnp.where(pos == 0, qi, nxt + (nxt >= qi).astype(jnp.int32))

    def scores(j, side):
        rows = pl.ds(pl.multiple_of(j * tile, tile), tile)
        k, ak = k_ref[0, rows, :], augk_ref[0, side]
        return jnp.concatenate([jnp.dot(jnp.where(low, k, ak), qxt[0], preferred_element_type=F32),
                                jnp.dot(jnp.where(low, ak, k), qxt[1], preferred_element_type=F32)],
                               axis=1)

    def softmax(j, buf):
        offset = nslope * (jnp.abs(qi - j) * tile).astype(F32)
        for g in range(2 * groups):
            cols = slice(g * LANES, (g + 1) * LANES)
            m_old = m_sc[:, cols]
            mx = m_old
            for r in range(tile // kb):
                s = st_sc[buf][r * kb:(r + 1) * kb, cols]
                mx = jnp.maximum(mx, jnp.max(s, axis=0, keepdims=True) + offset)
            al_sc[buf][:, cols] = jnp.exp2(m_old - mx)
            m_sc[:, cols] = mx
            sub = mx - offset
            for r in range(tile // kb):
                rows = slice(r * kb, (r + 1) * kb)
                p_sc[buf][rows, cols] = jnp.exp2(st_sc[buf][rows, cols] - sub).astype(BF16)

    def accumulate(j, buf):
        acc_sc[...] = al_sc[buf][...] * acc_sc[...] + jnp.dot(vt_sc[j], p_sc[buf][...], preferred_element_type=F32)

    p_sc[1][...] = jnp.zeros(p_sc[1].shape, BF16)
    al_sc[1][...] = jnp.ones(al_sc[1].shape, F32)
    st_sc[0][...] = jnp.minimum(scores(qi, 0), scores(qi, 1))

    def step(pos, buf):
        nxt = chunk_at(pos + 1)
        st_sc[1 - buf][...] = scores(nxt, (nxt > qi).astype(jnp.int32))
        softmax(chunk_at(pos), buf)
        accumulate(chunk_at(jnp.maximum(pos - 1, 0)), 1 - buf)

    def body(i, carry):
        step(2 * i, 0)
        step(2 * i + 1, 1)
        return carry

    lax.fori_loop(0, (cnt - 1) // 2, body, 0)
    if (n - 1) % 2:
        step(cnt - 2, (n - 2) & 1)
    last = (n - 1) & 1
    softmax(chunk_at(cnt - 1), last)
    if n > 1:
        accumulate(chunk_at(jnp.maximum(cnt - 2, 0)), 1 - last)
    accumulate(chunk_at(cnt - 1), last)

    lp = lp_ref[...]
    lam = (jnp.exp(jnp.sum(lp[0:1] * lp[1:2], axis=1, keepdims=True))
           - jnp.exp(jnp.sum(lp[2:3] * lp[3:4], axis=1, keepdims=True)) + lambda_init)
    on = acc_sc[:LANES, :] / acc_sc[LANES:LANES + 1, :]
    o = on[:, :tile] - lam * on[:, tile:]
    o = o * lax.rsqrt(jnp.mean(o * o, axis=0, keepdims=True) + EPS) * (1.0 - lambda_init)
    o_ref[0] = (o.T * gain_ref[...]).astype(o_ref.dtype)


def _diff_attention(p, lam_params, subln_gain, *, b, t, nh, lambda_init, tile=512):
    tile = min(tile, t)
    nq = t // tile
    augq, augk, nslopes = _alibi_tables(nh, tile, tile)
    return pl.pallas_call(
        functools.partial(_diff_attn_kernel, tile=tile, lambda_init=lambda_init),
        grid=(b, nh, nq),
        in_specs=[pl.BlockSpec((1, tile, LANES), lambda bi, h, qi: (h, bi * nq + qi, 0)),
                  pl.BlockSpec((1, t, LANES), lambda bi, h, qi: (h, bi, 0)),
                  pl.BlockSpec((1, t, LANES), lambda bi, h, qi: (nh + h, bi, 0)),
                  pl.BlockSpec((1, t, LANES), lambda bi, h, qi: (2 * nh + h, bi, 0)),
                  pl.BlockSpec((1, tile, LANES), lambda bi, h, qi: (h, 0, 0)),
                  pl.BlockSpec((1, 2, tile, LANES), lambda bi, h, qi: (h, 0, 0, 0)),
                  pl.BlockSpec((1, 1, LANES), lambda bi, h, qi: (h, 0, 0)),
                  pl.BlockSpec(memory_space=pltpu.SMEM),
                  pl.BlockSpec(lam_params.shape, lambda bi, h, qi: (0, 0)),
                  pl.BlockSpec((1, LANES), lambda bi, h, qi: (0, 0))],
        out_specs=pl.BlockSpec((1, tile, LANES), lambda bi, h, qi: (bi, qi, h)),
        out_shape=jax.ShapeDtypeStruct((b, t, nh * LANES), BF16),
        scratch_shapes=[pltpu.VMEM((t // tile, LANES + DIFF_PAD_ROWS, tile), BF16),
                        pltpu.VMEM((tile, 2 * tile), F32), pltpu.VMEM((tile, 2 * tile), F32),
                        pltpu.VMEM((tile, 2 * tile), BF16), pltpu.VMEM((tile, 2 * tile), BF16),
                        pltpu.VMEM((1, 2 * tile), F32), pltpu.VMEM((1, 2 * tile), F32),
                        pltpu.VMEM((1, 2 * tile), F32),
                        pltpu.VMEM((LANES + DIFF_PAD_ROWS, 2 * tile), F32),
                        pltpu.SMEM((t // tile,), F32), pltpu.SMEM((t // tile,), F32)],
        compiler_params=_params(("parallel", "parallel", "arbitrary"),
                                7 * t * LANES * 2 + 2 * tile * tile * 20 + 4 * LANES * tile * 4 + 8 * 2**20),
        name="diff_attn",
    )(p, p, p, p, augq, augk, nslopes, -nslopes[:, 0, 0], lam_params, subln_gain.reshape(1, LANES))


def _dil_attn_kernel(q_ref, k_ref, v_ref, ns_ref, o_ref, qf, kf, vf, qr, kr, vr, ob, lb, bt, *, blk):
    t = q_ref.shape[1]
    nslope = ns_ref[0][:, :1]
    if any(g > 1 for _, g in DIL_CONFIGS):
        qf[...] = q_ref[0].astype(F32)
        kf[...] = k_ref[0].astype(F32)
        vf[...] = v_ref[0].astype(F32)

    for n, (window, g) in enumerate(DIL_CONFIGS):
        radius = window // (2 * g)
        ln = t // g
        wk = min(blk + 2 * radius, ln)
        nblk = ln // blk
        if g > 1:
            for rho in range(g):
                rows = pl.ds(rho, ln, stride=g)
                qr[rho * ln:(rho + 1) * ln, :] = qf[rows, :].astype(BF16)
                kr[rho * ln:(rho + 1) * ln, :] = kf[rows, :].astype(BF16)
                vr[rho * ln:(rho + 1) * ln, :] = vf[rows, :].astype(BF16)
            qs, ks, vs = qr, kr, vr
        else:
            qs, ks, vs = q_ref.at[0], k_ref.at[0], v_ref.at[0]
        rel0 = (lax.broadcasted_iota(jnp.int32, (blk, wk), 1)
                - lax.broadcasted_iota(jnp.int32, (blk, wk), 0))
        for idx, shift in enumerate((0, -radius, blk - wk)):
            ar = jnp.abs(rel0 + shift)
            bt[idx, :, :wk] = jnp.where(ar <= radius, (nslope * float(g)) * ar.astype(F32), MASK_VALUE)

        def body(i, carry, n=n, g=g, radius=radius, ln=ln, wk=wk, nblk=nblk, qs=qs, ks=ks, vs=vs):
            rho = i // nblk
            bi = i % nblk
            l0 = bi * blk
            ws = jnp.clip(l0 - radius, 0, ln - wk)
            base = rho * ln
            qb = qs[pl.ds(pl.multiple_of(base + l0, blk), blk), :]
            kw = ks[pl.ds(pl.multiple_of(base + ws, radius), wk), :]
            vw = vs[pl.ds(pl.multiple_of(base + ws, radius), wk), :]
            place = jnp.where(bi == 0, 0, jnp.where(bi == nblk - 1, 2, 1))
            s = lax.dot_general(qb, kw, NT_DIMS, preferred_element_type=F32) + bt[place, :, :wk]
            mx = jnp.max(s, axis=1, keepdims=True)
            e = jnp.exp2(s - mx)
            den = jnp.sum(e, axis=1, keepdims=True)
            o = jnp.dot(e.astype(BF16), vw, preferred_element_type=F32) / den
            lse = mx + jnp.log2(den)
            rows = pl.ds(l0 * g + rho, blk, stride=g) if g > 1 else pl.ds(pl.multiple_of(l0, blk), blk)
            ob[n, rows, :] = o
            lb[n, rows, :] = jnp.broadcast_to(lse, (blk, LANES))
            return carry

        lax.fori_loop(0, g * nblk, body, 0, unroll=min(DIL_UNROLL, g * nblk))

    chunk = min(512, t)

    def merge(i, carry):
        rows = pl.ds(pl.multiple_of(i * chunk, chunk), chunk)
        l0, l1, l2 = lb[0, rows, :], lb[1, rows, :], lb[2, rows, :]
        mx = jnp.maximum(jnp.maximum(l0, l1), l2)
        w0, w1, w2 = jnp.exp2(l0 - mx), jnp.exp2(l1 - mx), jnp.exp2(l2 - mx)
        o = (w0 * ob[0, rows, :] + w1 * ob[1, rows, :] + w2 * ob[2, rows, :]) / (w0 + w1 + w2)
        o_ref[0, rows, :] = o.astype(o_ref.dtype)
        return carry

    lax.fori_loop(0, t // chunk, merge, 0)


def _dil_attention(p, nslopes, *, b, t, nh, blk=128):
    nbr = len(DIL_CONFIGS)
    return pl.pallas_call(
        functools.partial(_dil_attn_kernel, blk=blk),
        grid=(b, nh),
        in_specs=[pl.BlockSpec((1, t, LANES), lambda bi, h: (3 * nh + h, bi, 0)),
                  pl.BlockSpec((1, t, LANES), lambda bi, h: (4 * nh + h, bi, 0)),
                  pl.BlockSpec((1, t, LANES), lambda bi, h: (5 * nh + h, bi, 0)),
                  pl.BlockSpec((1, 1, LANES), lambda bi, h: (h, 0, 0))],
        out_specs=pl.BlockSpec((1, t, LANES), lambda bi, h: (bi, 0, h)),
        out_shape=jax.ShapeDtypeStruct((b, t, nh * LANES), BF16),
        scratch_shapes=[pltpu.VMEM((t, LANES), F32)] * 3 + [pltpu.VMEM((t, LANES), BF16)] * 3
                       + [pltpu.VMEM((nbr, t, LANES), F32)] * 2 + [pltpu.VMEM((3, blk, 2 * blk), F32)],
        compiler_params=_params(("parallel", "parallel"),
                                8 * t * LANES * 2 + 3 * t * LANES * 6 + 2 * nbr * t * LANES * 4 + 8 * 2**20),
        name="dil_attn",
    )(p, p, p, nslopes)


def _outproj_kernel(ma_ref, md_ref, wt_ref, wb_ref, x_ref, g_ref, o_ref):
    acc = (jnp.dot(ma_ref[...], wt_ref[...], preferred_element_type=F32)
           + jnp.dot(md_ref[...], wb_ref[...], preferred_element_type=F32))
    o_ref[...] = x_ref[...] + g_ref[0] * acc


def _outproj(ma, md, w_out, x2d, gate, *, t, tm=1024, tn=512):
    m, kh = ma.shape
    n = w_out.shape[1]
    tm, tn = min(tm, t), min(tn, n)
    return pl.pallas_call(
        _outproj_kernel,
        grid=(m // tm, n // tn),
        in_specs=[pl.BlockSpec((tm, kh), lambda i, j: (i, 0)),
                  pl.BlockSpec((tm, kh), lambda i, j: (i, 0)),
                  pl.BlockSpec((kh, tn), lambda i, j: (0, j)),
                  pl.BlockSpec((kh, tn), lambda i, j: (1, j)),
                  pl.BlockSpec((tm, tn), lambda i, j: (i, j)),
                  pl.BlockSpec((1, 1, tn), lambda i, j: ((i * tm) // t, 0, j))],
        out_specs=pl.BlockSpec((tm, tn), lambda i, j: (i, j)),
        out_shape=jax.ShapeDtypeStruct((m, n), F32),
        compiler_params=_params(("parallel", "arbitrary"),
                                2 * (2 * tm * kh + 2 * kh * tn) * 2 + 5 * tm * tn * 4 + 2 * 2**20),
        name="outproj",
    )(ma, md, w_out, w_out, x2d, gate)


def _take_top_rows(s, rows_ref, count):
    for r in range(count):
        mx = jnp.max(s, axis=0, keepdims=True)
        rows_ref[r:r + 1, :] = mx
        s = jnp.where(s >= mx, -jnp.inf, s)


def _router_kernel(h_ref, wq_ref, keys_ref, thr_ref, s1_ref, a_ref, b_ref, ra_ref, rb_ref, rc_ref):
    k = PEER_TOPK
    sub = keys_ref.shape[2]
    qt = jnp.dot(wq_ref[...], h_ref[...], preferred_element_type=F32)
    s0 = jnp.dot(keys_ref[0, 0], qt[:sub].astype(BF16), preferred_element_type=F32)
    s1 = jnp.dot(keys_ref[0, 1], qt[sub:].astype(BF16), preferred_element_type=F32)
    _take_top_rows(s0, ra_ref, k)
    _take_top_rows(s1, rb_ref, k)
    ra, rb = ra_ref[...], rb_ref[...]
    row8 = lax.broadcasted_iota(jnp.int32, (8, ra.shape[1]), 0)
    pieces = [ra[0:1] + rb, ra[1:2] + rb[0:8]]
    for r in range(2, 8):
        pieces.append(jnp.where(row8 < k // (r + 1), ra[r:r + 1] + rb[0:8], -jnp.inf))
    pieces.append(ra[8:16] + rb[0:1])
    cand = jnp.concatenate(pieces, axis=0)
    _take_top_rows(cand, rc_ref, k)
    tau = rc_ref[k - 1:k, :]
    top = ra[0:1] + rb[0:1]
    z = jnp.sum(jnp.where(cand >= tau, jnp.exp(cand - top), 0.0), axis=0, keepdims=True)
    thr = jnp.full(s0.shape, jnp.inf, F32)
    for r in range(k):
        row_thr = jnp.min(jnp.where(ra[r:r + 1] + rb >= tau, rb, jnp.inf), axis=0, keepdims=True)
        thr = jnp.where(s0 == ra[r:r + 1], row_thr, thr)
    thr_ref[0] = thr
    s1_ref[0] = s1
    a_ref[0] = jnp.exp(s0 - ra[0:1]) / z
    b_ref[0] = jnp.exp(s1 - rb[0:1])


def _router(ht, wq_t, keys, *, tm=1024):
    d, m = ht.shape
    heads, _, n_keys, sub = keys.shape
    tm = min(tm, m)
    big = jax.ShapeDtypeStruct((heads, n_keys, m), F32)
    big_spec = pl.BlockSpec((1, n_keys, tm), lambda i, h: (h, 0, i))
    return pl.pallas_call(
        _router_kernel,
        grid=(m // tm, heads),
        in_specs=[pl.BlockSpec((d, tm), lambda i, h: (0, i)),
                  pl.BlockSpec((2 * sub, d), lambda i, h: (h, 0)),
                  pl.BlockSpec((1, 2, n_keys, sub), lambda i, h: (h, 0, 0, 0))],
        out_specs=[big_spec, big_spec, big_spec, big_spec],
        out_shape=[big, big, big, big],
        scratch_shapes=[pltpu.VMEM((PEER_TOPK, tm), F32)] * 3,
        compiler_params=_params(("parallel", "arbitrary"),
                                2 * d * tm * 2 + 2 * 2 * sub * d * 2 + 8 * n_keys * tm * 4 + 16 * 2**20),
        name="peer_router",
    )(ht, wq_t, keys)


def _peer_dense_kernel(h_ref, u_ref, vt_ref, sc_ref, thr_ref, s1_ref, a_ref, b_ref, o_ref, g_sc):
    e = pl.program_id(1)
    heads, n_keys, tm = s1_ref.shape
    te = u_ref.shape[0]
    sub = min(te, PEER_SUBTILE)

    @pl.when(e == 0)
    def _():
        o_ref[...] = jnp.zeros(o_ref.shape, F32)

    for ii in range(te // n_keys):
        i = e * (te // n_keys) + ii
        gates = jnp.zeros((n_keys, tm), F32)
        for h in range(heads):
            gates = gates + (jnp.where(s1_ref[h] >= thr_ref[h, pl.ds(i, 1), :], b_ref[h], 0.0)
                             * a_ref[h, pl.ds(i, 1), :])
        g_sc[ii * n_keys:(ii + 1) * n_keys, :] = gates
    nsub = te // sub
    act_scale, w_scale = sc_ref[0:1, :], sc_ref[1:2, :]
    acts = [jnp.dot(u_ref[t * sub:(t + 1) * sub, :], h_ref[...], preferred_element_type=F32) * act_scale
            for t in range(nsub)]
    for t in range(nsub):
        gel = 0.5 * acts[t] * (1.0 + lax.erf(acts[t] * (1.0 / math.sqrt(2.0))))
        w = (g_sc[t * sub:(t + 1) * sub, :] * gel * w_scale).astype(vt_ref.dtype)
        o_ref[...] += jnp.dot(vt_ref[:, t * sub:(t + 1) * sub], w, preferred_element_type=F32)


def _peer_dense(ht, u, vt, scales, thr, s1, a, bf, *, tm=512, te=1024):
    d, m = ht.shape
    n_exp = u.shape[0]
    heads, n_keys, _ = s1.shape
    tm = min(tm, m)
    once = pl.Buffered(1)
    tok = lambda i, e: (0, 0, i)
    big_spec = pl.BlockSpec((heads, n_keys, tm), tok, pipeline_mode=once)
    return pl.pallas_call(
        _peer_dense_kernel,
        grid=(m // tm, n_exp // te),
        in_specs=[pl.BlockSpec((d, tm), lambda i, e: (0, i), pipeline_mode=once),
                  pl.BlockSpec((te, d), lambda i, e: (e, 0)),
                  pl.BlockSpec((d, te), lambda i, e: (0, e)),
                  pl.BlockSpec((2, tm), lambda i, e: (0, i), pipeline_mode=once),
                  big_spec, big_spec, big_spec, big_spec],
        out_specs=pl.BlockSpec((d, tm), lambda i, e: (0, i)),
        out_shape=jax.ShapeDtypeStruct((d, m), F32),
        scratch_shapes=[pltpu.VMEM((te, tm), F32)],
        compiler_params=_params(("parallel", "arbitrary"),
                                d * tm * 2 + 4 * te * d * 2 + 2 * d * tm * 4 + 4 * heads * n_keys * tm * 4
                                + 6 * te * tm * 4 + d * tm * 4),
        name="peer_dense",
    )(ht, u, vt, scales, thr, s1, a, bf)


def _final_kernel(x_ref, pt_ref, ps_ref, g_ref, gain_ref, o_ref):
    x = x_ref[...] + g_ref[0] * (pt_ref[...] * ps_ref[...]).T
    ms = jnp.mean(x * x, axis=-1, keepdims=True)
    o_ref[...] = x * lax.rsqrt(ms + EPS) * gain_ref[...]


def _final(x1, pt, pscale, gate, gain, *, t, tm=256):
    m, d = x1.shape
    tm = min(tm, t)
    return pl.pallas_call(
        _final_kernel,
        grid=(m // tm,),
        in_specs=[pl.BlockSpec((tm, d), lambda i: (i, 0)),
                  pl.BlockSpec((d, tm), lambda i: (0, i)),
                  pl.BlockSpec((1, tm), lambda i: (0, i)),
                  pl.BlockSpec((1, 1, d), lambda i: ((i * tm) // t, 0, 0)),
                  pl.BlockSpec((1, d), lambda i: (0, 0))],
        out_specs=pl.BlockSpec((tm, d), lambda i: (i, 0)),
        out_shape=jax.ShapeDtypeStruct((m, d), F32),
        compiler_params=_params(("parallel",), 8 * tm * d * 4 + 4 * 2**20),
        name="final_norm",
    )(x1, pt, pscale, gate, gain.reshape(1, d))


def _trunk(x, ada, w, depth_index=0):
    b, t, d = x.shape
    nh = d // (2 * LANES)
    lambda_init = 0.8 - 0.6 * math.exp(-0.3 * depth_index)
    sh1, sc1, g1, sh2, sc2, g2 = [a.reshape(b, 1, d) for a in jnp.split(ada, 6, axis=-1)]

    h = _norm_mod(x, w["norm1_gain"], sc1, sh1, transpose=False)
    p = _inproj(h, w["w_in"], w["colscale"])
    ma = _diff_attention(p, w["diff_lambda"], w["diff_subln_gain"], b=b, t=t, nh=nh, lambda_init=lambda_init)
    md = _dil_attention(p, w["dil_nslopes"], b=b, t=t, nh=nh)
    x1 = _outproj(ma.reshape(b * t, nh * LANES), md.reshape(b * t, nh * LANES), w["w_out"],
                  x.reshape(b * t, d), g1, t=t)

    ht, ht8, stat = _norm_mod(x1.reshape(b, t, d), w["norm2_gain"], sc2, sh2, transpose=True)
    thr, s1, a, bf = _router(ht, w["wq_t"], w["sub_keys"])
    act_scale = stat[0:1] * w["u_inv_scale"]
    w_scale = F8_TARGET / jnp.maximum(stat[1:2] * w["u_row_norm"], TINY)
    pt = _peer_dense(ht8, w["peer_u"], w["peer_vt"], jnp.concatenate([act_scale, w_scale], axis=0),
                     thr, s1, a, bf)
    y = _final(x1, pt, w["v_inv_scale"] / w_scale, g2, w["final_gain"], t=t)
    return y.reshape(b, t, d)


def kernel(x_prompt, x_sample, c_prompt, c_sample, norm1_gain, norm2_gain, w_ada, b_ada, w_in, diff_lambda,
           diff_subln_gain, w_out, peer_w_query, peer_sub_keys, peer_u, peer_v, final_gain):
    d = x_prompt.shape[-1]
    nh = d // (2 * LANES)
    bp, bs = c_prompt.shape[0], c_sample.shape[0]
    assert norm1_gain.shape[0] == 1, "single-layer trunk"
    assert peer_sub_keys.shape[3] == LANES and peer_sub_keys.shape[4] == LANES

    rows = -(-(bp + bs) // 8) * 8
    c_all = jnp.concatenate([c_prompt, c_sample, jnp.zeros((rows - bp - bs, d), F32)], axis=0)
    ada = _ada(c_all, w_ada[0], b_ada[0])

    colscale = np.ones((1, w_in.shape[-1]), np.float32)
    colscale[0, :nh * LANES] = DIFF_HALF_DIM ** -0.5 * math.log2(math.e)
    colscale[0, 3 * nh * LANES:4 * nh * LANES] = LANES ** -0.5 * math.log2(math.e)
    lane_bcast = lambda v: jnp.asarray(np.broadcast_to(np.asarray(v, np.float32)[:, None, None], (nh, 1, LANES)))
    w = dict(
        norm1_gain=norm1_gain[0], norm2_gain=norm2_gain[0], final_gain=final_gain,
        w_in=w_in[0].astype(BF16), colscale=jnp.asarray(colscale),
        dil_nslopes=lane_bcast(-_alibi_slopes(nh) * math.log2(math.e)),
        diff_lambda=diff_lambda[0], diff_subln_gain=diff_subln_gain[0],
        w_out=w_out[0].astype(BF16),
        wq_t=peer_w_query[0].T.astype(BF16), sub_keys=peer_sub_keys[0].astype(BF16),
    )
    u_amax = jnp.maximum(jnp.max(jnp.abs(peer_u[0])), TINY)
    v_amax = jnp.maximum(jnp.max(jnp.abs(peer_v[0])), TINY)
    w.update(
        peer_u=(peer_u[0] * (F8_TARGET / u_amax)).astype(F8), u_inv_scale=u_amax / F8_TARGET,
        u_row_norm=jnp.sqrt(jnp.max(jnp.sum(peer_u[0] * peer_u[0], axis=1))),
        peer_vt=(peer_v[0].T * (F8_TARGET / v_amax)).astype(F8), v_inv_scale=v_amax / F8_TARGET,
    )
    y_prompt = _trunk(x_prompt, ada[:bp], w)
    y_sample = _trunk(x_sample, ada[bp:bp + bs], w)
    return (y_prompt, y_sample)
```

```python
import functools
import math

import ml_dtypes
import numpy as np
import jax
import jax.numpy as jnp
from jax import lax
from jax.experimental import pallas as pl
from jax.experimental.pallas import tpu as pltpu

F32 = jnp.float32
BF16 = jnp.bfloat16
F8 = jnp.float8_e4m3fn
F8_TARGET = 240.0
TINY = 1e-30
EPS = 1e-6
MASK_VALUE = -1e30
LANES = 128
VMEM_CAP_BYTES = 60000 * 1024
DIFF_HALF_DIM = 64
DIFF_KEY_BLOCK = 256
DIFF_SKIP_MIN_CHUNKS = 8
DIFF_SKIP_MARGIN = 140.0
DIFF_PAD_ROWS = 16
DIL_CONFIGS = ((128, 1), (512, 4), (2048, 16))
DIL_UNROLL = 8
PEER_TOPK = 16
PEER_SUBTILE = 256
NT_DIMS = (((1,), (1,)), ((), ()))


def _params(semantics, vmem_bytes):
    return pltpu.CompilerParams(
        dimension_semantics=semantics,
        vmem_limit_bytes=int(min(max(vmem_bytes, 16 * 2**20), VMEM_CAP_BYTES)))


def _alibi_slopes(n):
    return 2.0 ** (-8.0 * np.arange(1, n + 1) / n)


def _ada_kernel(c_ref, w_ref, b_ref, o_ref):
    c = c_ref[...]
    a = (c / (1.0 + jnp.exp(-c))).astype(BF16)
    o_ref[...] = jnp.dot(a, w_ref[...].astype(BF16), preferred_element_type=F32) + b_ref[...]


def _ada(c_all, w_ada, b_ada):
    rows, d = c_all.shape
    n = w_ada.shape[1]
    tn = 512
    return pl.pallas_call(
        _ada_kernel,
        grid=(n // tn,),
        in_specs=[pl.BlockSpec((rows, d), lambda j: (0, 0)),
                  pl.BlockSpec((d, tn), lambda j: (0, j)),
                  pl.BlockSpec((1, tn), lambda j: (0, j))],
        out_specs=pl.BlockSpec((rows, tn), lambda j: (0, j)),
        out_shape=jax.ShapeDtypeStruct((rows, n), F32),
        compiler_params=_params(("arbitrary",), 2 * d * tn * 4 + d * tn * 2 + 4 * 2**20),
        name="ada",
    )(c_all, w_ada, b_ada.reshape(1, n))


def _norm_mod_kernel(x_ref, gain_ref, sc_ref, sh_ref, o_ref, *rest, transpose):
    x = x_ref[0]
    ms = jnp.mean(x * x, axis=-1, keepdims=True)
    y = x * lax.rsqrt(ms + EPS) * gain_ref[...]
    h = y * (1.0 + sc_ref[0]) + sh_ref[0]
    if transpose:
        q_ref, stat_ref = rest
        ht = h.T
        o_ref[...] = ht.astype(o_ref.dtype)
        amax = jnp.maximum(jnp.max(jnp.abs(ht), axis=0, keepdims=True), TINY)
        q_ref[...] = (ht * (F8_TARGET / amax)).astype(q_ref.dtype)
        stat_ref[0:1, :] = amax * (1.0 / F8_TARGET)
        stat_ref[1:2, :] = jnp.sqrt(jnp.sum(ht * ht, axis=0, keepdims=True))
    else:
        o_ref[...] = h.astype(o_ref.dtype)


def _norm_mod(x, gain, sc, sh, *, transpose, tm=256):
    b, t, d = x.shape
    m = b * t
    tm = min(tm, t)
    nt = t // tm
    if transpose:
        col = lambda bi, i: (0, bi * nt + i)
        out_spec = [pl.BlockSpec((d, tm), col), pl.BlockSpec((d, tm), col), pl.BlockSpec((2, tm), col)]
        out_shape = [jax.ShapeDtypeStruct((d, m), BF16), jax.ShapeDtypeStruct((d, m), F8),
                     jax.ShapeDtypeStruct((2, m), F32)]
    else:
        out_spec = pl.BlockSpec((tm, d), lambda bi, i: (bi * nt + i, 0))
        out_shape = jax.ShapeDtypeStruct((m, d), BF16)
    return pl.pallas_call(
        functools.partial(_norm_mod_kernel, transpose=transpose),
        grid=(b, nt),
        in_specs=[pl.BlockSpec((1, tm, d), lambda bi, i: (bi, i, 0)),
                  pl.BlockSpec((1, d), lambda bi, i: (0, 0)),
                  pl.BlockSpec((1, 1, d), lambda bi, i: (bi, 0, 0)),
                  pl.BlockSpec((1, 1, d), lambda bi, i: (bi, 0, 0))],
        out_specs=out_spec,
        out_shape=out_shape,
        compiler_params=_params(("parallel", "parallel"), 8 * tm * d * 4 + 4 * 2**20),
        name="norm_mod_t" if transpose else "norm_mod",
    )(x, gain.reshape(1, d), sc, sh)


def _inproj_kernel(a_ref, w_ref, cs_ref, o_ref):
    acc = jnp.dot(a_ref[...], w_ref[...], preferred_element_type=F32) * cs_ref[...]
    for c in range(o_ref.shape[0]):
        o_ref[c] = acc[:, c * LANES:(c + 1) * LANES].astype(o_ref.dtype)


def _inproj(h, w, colscale, *, tm=1024, tn=512):
    m, k = h.shape
    n = w.shape[1]
    tm, tn = min(tm, m), min(tn, n)
    nb = tn // LANES
    return pl.pallas_call(
        _inproj_kernel,
        grid=(m // tm, n // tn),
        in_specs=[pl.BlockSpec((tm, k), lambda i, j: (i, 0)),
                  pl.BlockSpec((k, tn), lambda i, j: (0, j)),
                  pl.BlockSpec((1, tn), lambda i, j: (0, j))],
        out_specs=pl.BlockSpec((nb, tm, LANES), lambda i, j: (j, i, 0)),
        out_shape=jax.ShapeDtypeStruct((n // LANES, m, LANES), BF16),
        compiler_params=_params(("parallel", "arbitrary"),
                                2 * (tm * k + k * tn + tm * tn) * 2 + 2 * tm * tn * 4 + 2 * 2**20),
        name="inproj",
    )(h, w, colscale)


def _alibi_tables(nh, tq, tk):
    bf = lambda a: np.asarray(a, np.float64).astype(ml_dtypes.bfloat16).astype(np.float64)
    slope = _alibi_slopes(nh) * math.log2(math.e)
    hi = bf(slope)
    lo = bf(slope - hi)
    augq = np.zeros((nh, tq, LANES), np.float64)
    augk = np.zeros((nh, 2, tk, LANES), np.float64)
    qpos, kpos = np.arange(tq), np.arange(tk)
    for h in range(nh):
        qcols = [qpos // 16, qpos // 16, qpos % 16, qpos % 16,
                 16 * hi[h] + 0 * qpos, 16 * lo[h] + 0 * qpos, hi[h] + 0 * qpos, lo[h] + 0 * qpos]
        kcols = [-16 * hi[h] + 0 * kpos, -16 * lo[h] + 0 * kpos, -hi[h] + 0 * kpos, -lo[h] + 0 * kpos,
                 kpos // 16, kpos // 16, kpos % 16, kpos % 16]
        for c in range(8):
            for base in (0, DIFF_HALF_DIM):
                augq[h, :, base + c] = qcols[c]
                augk[h, 0, :, base + c] = kcols[c]
                augk[h, 1, :, base + c] = -kcols[c]
    nslope = np.broadcast_to(-(hi + lo)[:, None, None], (nh, 1, LANES))
    return (jnp.asarray(augq, BF16), jnp.asarray(augk, BF16), jnp.asarray(nslope, F32))


def _diff_attn_kernel(q_ref, qall_ref, k_ref, v_ref, augq_ref, augk_ref, ns_ref, sl_ref, lp_ref, gain_ref, o_ref,
                      vt_sc, st0, st1, p0, p1, al0, al1, m_sc, acc_sc, qmax_sm, kmax_sm, *, tile, lambda_init, skip):
    qi = pl.program_id(2)
    t = k_ref.shape[1]
    n = t // tile
    kb = min(tile, DIFF_KEY_BLOCK)
    groups = tile // LANES

    @pl.when(qi == 0)
    def _():
        ones_row = jnp.where(lax.broadcasted_iota(jnp.int32, (DIFF_PAD_ROWS, tile), 0) == 0, 1.0, 0.0)
        for c in range(n):
            vt_sc[c, :LANES, :] = v_ref[0, c * tile:(c + 1) * tile, :].astype(F32).T.astype(BF16)
            vt_sc[c, LANES:, :] = ones_row.astype(BF16)
            if skip:
                kf = k_ref[0, c * tile:(c + 1) * tile, :].astype(F32)
                kmax_sm[c] = jnp.max(jnp.sqrt(jnp.max(jnp.sum(kf * kf, axis=1, keepdims=True), axis=0, keepdims=True)))
                qf = qall_ref[0, c * tile:(c + 1) * tile, :].astype(F32)
                qmax_sm[c] = jnp.max(jnp.sqrt(jnp.max(jnp.sum(qf * qf, axis=1, keepdims=True), axis=0, keepdims=True)))

    q = q_ref[0]
    lane = lax.broadcasted_iota(jnp.int32, q.shape, 1)
    low = lane < DIFF_HALF_DIM
    aq = augq_ref[0]
    qxt = [jnp.where(low, q, aq).astype(F32).T.astype(BF16),
           jnp.where(low, aq, q).astype(F32).T.astype(BF16)]
    nslope = ns_ref[0][:1, :1]
    m_sc[...] = jnp.full(m_sc.shape, MASK_VALUE, F32)
    acc_sc[...] = jnp.zeros(acc_sc.shape, F32)
    st_sc, p_sc, al_sc = (st0, st1), (p0, p1), (al0, al1)

    if skip:
        bq = qmax_sm[qi]
        slope = sl_ref[pl.program_id(1)]
        floor_m = -(bq * kmax_sm[qi]) - DIFF_SKIP_MARGIN

        def dead(j):
            gap = ((jnp.abs(qi - j) - 1) * tile + 1).astype(F32)
            return bq * kmax_sm[j] - slope * gap < floor_m

        lo = jnp.int32(0)
        for j in range(n - 1):
            lo = lo + ((lo == j) & (j < qi) & dead(j)).astype(jnp.int32)
        hi = jnp.int32(n - 1)
        for j in range(n - 1, 0, -1):
            hi = hi - ((hi == j) & (j > qi) & dead(j)).astype(jnp.int32)
        odd = ((hi - lo + 1 - n) & 1) == 1
        lo, hi = jnp.where(odd & (lo > 0), lo - 1, lo), jnp.where(odd & (lo == 0), hi + 1, hi)
    else:
        lo, hi = 0, n - 1
    cnt = hi - lo + 1

    def chunk_at(pos):
        nxt = lo + pos - 1
        return jnp.where(pos == 0, qi, nxt + (nxt >= qi).astype(jnp.int32))

    def scores(j, side):
        rows = pl.ds(pl.multiple_of(j * tile, tile), tile)
        k, ak = k_ref[0, rows, :], augk_ref[0, side]
        return jnp.concatenate([jnp.dot(jnp.where(low, k, ak), qxt[0], preferred_element_type=F32),
                                jnp.dot(jnp.where(low, ak, k), qxt[1], preferred_element_type=F32)],
                               axis=1)

    def softmax(j, buf):
        offset = nslope * (jnp.abs(qi - j) * tile).astype(F32)
        for g in range(2 * groups):
            cols = slice(g * LANES, (g + 1) * LANES)
            m_old = m_sc[:, cols]
            mx = m_old
            for r in range(tile // kb):
                s = st_sc[buf][r * kb:(r + 1) * kb, cols]
                mx = jnp.maximum(mx, jnp.max(s, axis=0, keepdims=True) + offset)
            al_sc[buf][:, cols] = jnp.exp2(m_old - mx)
            m_sc[:, cols] = mx
            sub = mx - offset
            for r in range(tile // kb):
                rows = slice(r * kb, (r + 1) * kb)
                p_sc[buf][rows, cols] = jnp.exp2(st_sc[buf][rows, cols] - sub).astype(BF16)

    def accumulate(j, buf):
        acc_sc[...] = al_sc[buf][...] * acc_sc[...] + jnp.dot(vt_sc[j], p_sc[buf][...], preferred_element_type=F32)

    p_sc[1][...] = jnp.zeros(p_sc[1].shape, BF16)
    al_sc[1][...] = jnp.ones(al_sc[1].shape, F32)
    st_sc[0][...] = jnp.minimum(scores(qi, 0), scores(qi, 1))

    def step(pos, buf):
        nxt = chunk_at(pos + 1)
        st_sc[1 - buf][...] = scores(nxt, (nxt > qi).astype(jnp.int32))
        softmax(chunk_at(pos), buf)
        accumulate(chunk_at(jnp.maximum(pos - 1, 0)), 1 - buf)

    def body(i, carry):
        step(2 * i, 0)
        step(2 * i + 1, 1)
        return carry

    lax.fori_loop(0, (cnt - 1) // 2, body, 0)
    if (n - 1) % 2:
        step(cnt - 2, (n - 2) & 1)
    last = (n - 1) & 1
    softmax(chunk_at(cnt - 1), last)
    if n > 1:
        accumulate(chunk_at(jnp.maximum(cnt - 2, 0)), 1 - last)
    accumulate(chunk_at(cnt - 1), last)

    lp = lp_ref[...]
    lam = (jnp.exp(jnp.sum(lp[0:1] * lp[1:2], axis=1, keepdims=True))
           - jnp.exp(jnp.sum(lp[2:3] * lp[3:4], axis=1, keepdims=True)) + lambda_init)
    on = acc_sc[:LANES, :] / acc_sc[LANES:LANES + 1, :]
    o = on[:, :tile] - lam * on[:, tile:]
    o = o * lax.rsqrt(jnp.mean(o * o, axis=0, keepdims=True) + EPS) * (1.0 - lambda_init)
    o_ref[0] = (o.T * gain_ref[...]).astype(o_ref.dtype)


def _diff_attention(p, lam_params, subln_gain, *, b, t, nh, lambda_init, tile=512):
    tile = min(tile, t)
    nq = t // tile
    augq, augk, nslopes = _alibi_tables(nh, tile, tile)
    skip = nq >= DIFF_SKIP_MIN_CHUNKS
    return pl.pallas_call(
        functools.partial(_diff_attn_kernel, tile=tile, lambda_init=lambda_init, skip=skip),
        grid=(b, nh, nq),
        in_specs=[pl.BlockSpec((1, tile, LANES), lambda bi, h, qi: (h, bi * nq + qi, 0)),
                  pl.BlockSpec((1, t, LANES), lambda bi, h, qi: (h, bi, 0)),
                  pl.BlockSpec((1, t, LANES), lambda bi, h, qi: (nh + h, bi, 0)),
                  pl.BlockSpec((1, t, LANES), lambda bi, h, qi: (2 * nh + h, bi, 0)),
                  pl.BlockSpec((1, tile, LANES), lambda bi, h, qi: (h, 0, 0)),
                  pl.BlockSpec((1, 2, tile, LANES), lambda bi, h, qi: (h, 0, 0, 0)),
                  pl.BlockSpec((1, 1, LANES), lambda bi, h, qi: (h, 0, 0)),
                  pl.BlockSpec(memory_space=pltpu.SMEM),
                  pl.BlockSpec(lam_params.shape, lambda bi, h, qi: (0, 0)),
                  pl.BlockSpec((1, LANES), lambda bi, h, qi: (0, 0))],
        out_specs=pl.BlockSpec((1, tile, LANES), lambda bi, h, qi: (bi, qi, h)),
        out_shape=jax.ShapeDtypeStruct((b, t, nh * LANES), BF16),
        scratch_shapes=[pltpu.VMEM((t // tile, LANES + DIFF_PAD_ROWS, tile), BF16),
                        pltpu.VMEM((tile, 2 * tile), F32), pltpu.VMEM((tile, 2 * tile), F32),
                        pltpu.VMEM((tile, 2 * tile), BF16), pltpu.VMEM((tile, 2 * tile), BF16),
                        pltpu.VMEM((1, 2 * tile), F32), pltpu.VMEM((1, 2 * tile), F32),
                        pltpu.VMEM((1, 2 * tile), F32),
                        pltpu.VMEM((LANES + DIFF_PAD_ROWS, 2 * tile), F32),
                        pltpu.SMEM((t // tile,), F32), pltpu.SMEM((t // tile,), F32)],
        compiler_params=_params(("parallel", "parallel", "arbitrary"),
                                7 * t * LANES * 2 + 2 * tile * tile * 20 + 4 * LANES * tile * 4 + 8 * 2**20),
        name="diff_attn",
    )(p, p, p, p, augq, augk, nslopes, -nslopes[:, 0, 0], lam_params, subln_gain.reshape(1, LANES))


def _dil_attn_kernel(q_ref, k_ref, v_ref, ns_ref, o_ref, qf, kf, vf, qg, kg, vg, qr, kr, vr, ob, lb, bt, *, blk):
    t = q_ref.shape[1]
    nslope = ns_ref[0][:, :1]
    srcs = {1: (q_ref.at[0], k_ref.at[0], v_ref.at[0])}
    if any(g > 1 for _, g in DIL_CONFIGS):
        qf[...] = q_ref[0].astype(F32)
        kf[...] = k_ref[0].astype(F32)
        vf[...] = v_ref[0].astype(F32)
    f32_src = {1: (qf, kf, vf)}

    for n, (window, g) in enumerate(DIL_CONFIGS):
        radius = window // (2 * g)
        ln = t // g
        wk = min(blk + 2 * radius, ln)
        nblk = ln // blk
        if g not in srcs:
            gp = max(d for d in f32_src if g % d == 0)
            step, lp = g // gp, t // gp
            keep = any(d > g and d % g == 0 for _, d in DIL_CONFIGS)
            assert not (keep and gp > 1), "one f32 staging level"
            dst16 = (qr, kr, vr)
            dst32 = (qg, kg, vg) if keep else (None, None, None)
            for a_src, a16, a32 in zip(f32_src[gp], dst16, dst32):
                for r in range(gp):
                    for s in range(step):
                        rho = r + gp * s
                        rows = a_src[pl.ds(r * lp + s, ln, stride=step), :]
                        a16[rho * ln:(rho + 1) * ln, :] = rows.astype(BF16)
                        if keep:
                            a32[rho * ln:(rho + 1) * ln, :] = rows
            srcs[g] = dst16
            if keep:
                f32_src[g] = dst32
        qs, ks, vs = srcs[g]
        rel0 = (lax.broadcasted_iota(jnp.int32, (blk, wk), 1)
                - lax.broadcasted_iota(jnp.int32, (blk, wk), 0))
        for idx, shift in enumerate((0, -radius, blk - wk)):
            ar = jnp.abs(rel0 + shift)
            bt[idx, :, :wk] = jnp.where(ar <= radius, (nslope * float(g)) * ar.astype(F32), MASK_VALUE)

        def body(i, carry, n=n, g=g, radius=radius, ln=ln, wk=wk, nblk=nblk, qs=qs, ks=ks, vs=vs):
            rho = i // nblk
            bi = i % nblk
            l0 = bi * blk
            ws = jnp.clip(l0 - radius, 0, ln - wk)
            base = rho * ln
            qb = qs[pl.ds(pl.multiple_of(base + l0, blk), blk), :]
            kw = ks[pl.ds(pl.multiple_of(base + ws, radius), wk), :]
            vw = vs[pl.ds(pl.multiple_of(base + ws, radius), wk), :]
            place = jnp.where(bi == 0, 0, jnp.where(bi == nblk - 1, 2, 1))
            s = lax.dot_general(qb, kw, NT_DIMS, preferred_element_type=F32) + bt[place, :, :wk]
            mx = jnp.max(s, axis=1, keepdims=True)
            e = jnp.exp2(s - mx)
            den = jnp.sum(e, axis=1, keepdims=True)
            o = jnp.dot(e.astype(BF16), vw, preferred_element_type=F32) / den
            lse = mx + jnp.log2(den)
            rows = pl.ds(l0 * g + rho, blk, stride=g) if g > 1 else pl.ds(pl.multiple_of(l0, blk), blk)
            ob[n, rows, :] = o
            lb[n, rows, :] = jnp.broadcast_to(lse, (blk, LANES))
            return carry

        lax.fori_loop(0, g * nblk, body, 0, unroll=min(DIL_UNROLL, g * nblk))

    chunk = min(512, t)

    def merge(i, carry):
        rows = pl.ds(pl.multiple_of(i * chunk, chunk), chunk)
        l0, l1, l2 = lb[0, rows, :], lb[1, rows, :], lb[2, rows, :]
        mx = jnp.maximum(jnp.maximum(l0, l1), l2)
        w0, w1, w2 = jnp.exp2(l0 - mx), jnp.exp2(l1 - mx), jnp.exp2(l2 - mx)
        o = (w0 * ob[0, rows, :] + w1 * ob[1, rows, :] + w2 * ob[2, rows, :]) / (w0 + w1 + w2)
        o_ref[0, rows, :] = o.astype(o_ref.dtype)
        return carry

    lax.fori_loop(0, t // chunk, merge, 0)


def _dil_attention(p, nslopes, *, b, t, nh, blk=128):
    nbr = len(DIL_CONFIGS)
    return pl.pallas_call(
        functools.partial(_dil_attn_kernel, blk=blk),
        grid=(b, nh),
        in_specs=[pl.BlockSpec((1, t, LANES), lambda bi, h: (3 * nh + h, bi, 0)),
                  pl.BlockSpec((1, t, LANES), lambda bi, h: (4 * nh + h, bi, 0)),
                  pl.BlockSpec((1, t, LANES), lambda bi, h: (5 * nh + h, bi, 0)),
                  pl.BlockSpec((1, 1, LANES), lambda bi, h: (h, 0, 0))],
        out_specs=pl.BlockSpec((1, t, LANES), lambda bi, h: (bi, 0, h)),
        out_shape=jax.ShapeDtypeStruct((b, t, nh * LANES), BF16),
        scratch_shapes=[pltpu.VMEM((t, LANES), F32)] * 6 + [pltpu.VMEM((t, LANES), BF16)] * 3
                       + [pltpu.VMEM((nbr, t, LANES), F32)] * 2 + [pltpu.VMEM((3, blk, 2 * blk), F32)],
        compiler_params=_params(("parallel", "parallel"),
                                8 * t * LANES * 2 + 3 * t * LANES * 10 + 2 * nbr * t * LANES * 4 + 8 * 2**20),
        name="dil_attn",
    )(p, p, p, nslopes)


def _outproj_kernel(ma_ref, md_ref, wt_ref, wb_ref, x_ref, g_ref, o_ref):
    acc = (jnp.dot(ma_ref[...], wt_ref[...], preferred_element_type=F32)
           + jnp.dot(md_ref[...], wb_ref[...], preferred_element_type=F32))
    o_ref[...] = x_ref[...] + g_ref[0] * acc


def _outproj(ma, md, w_out, x2d, gate, *, t, tm=1024, tn=512):
    m, kh = ma.shape
    n = w_out.shape[1]
    tm, tn = min(tm, t), min(tn, n)
    return pl.pallas_call(
        _outproj_kernel,
        grid=(m // tm, n // tn),
        in_specs=[pl.BlockSpec((tm, kh), lambda i, j: (i, 0)),
                  pl.BlockSpec((tm, kh), lambda i, j: (i, 0)),
                  pl.BlockSpec((kh, tn), lambda i, j: (0, j)),
                  pl.BlockSpec((kh, tn), lambda i, j: (1, j)),
                  pl.BlockSpec((tm, tn), lambda i, j: (i, j)),
                  pl.BlockSpec((1, 1, tn), lambda i, j: ((i * tm) // t, 0, j))],
        out_specs=pl.BlockSpec((tm, tn), lambda i, j: (i, j)),
        out_shape=jax.ShapeDtypeStruct((m, n), F32),
        compiler_params=_params(("parallel", "arbitrary"),
                                2 * (2 * tm * kh + 2 * kh * tn) * 2 + 5 * tm * tn * 4 + 2 * 2**20),
        name="outproj",
    )(ma, md, w_out, w_out, x2d, gate)


def _take_top_rows(s, rows_ref, count):
    for r in range(count):
        mx = jnp.max(s, axis=0, keepdims=True)
        rows_ref[r:r + 1, :] = mx
        s = jnp.where(s >= mx, -jnp.inf, s)


def _router_kernel(h_ref, wq_ref, keys_ref, thr_ref, s1_ref, a_ref, b_ref, ra_ref, rb_ref, rc_ref):
    k = PEER_TOPK
    sub = keys_ref.shape[2]
    qt = jnp.dot(wq_ref[...], h_ref[...], preferred_element_type=F32)
    s0 = jnp.dot(keys_ref[0, 0], qt[:sub].astype(BF16), preferred_element_type=F32)
    s1 = jnp.dot(keys_ref[0, 1], qt[sub:].astype(BF16), preferred_element_type=F32)
    _take_top_rows(s0, ra_ref, k)
    _take_top_rows(s1, rb_ref, k)
    ra, rb = ra_ref[...], rb_ref[...]
    row8 = lax.broadcasted_iota(jnp.int32, (8, ra.shape[1]), 0)
    pieces = [ra[0:1] + rb, ra[1:2] + rb[0:8]]
    for r in range(2, 8):
        pieces.append(jnp.where(row8 < k // (r + 1), ra[r:r + 1] + rb[0:8], -jnp.inf))
    pieces.append(ra[8:16] + rb[0:1])
    cand = jnp.concatenate(pieces, axis=0)
    _take_top_rows(cand, rc_ref, k)
    tau = rc_ref[k - 1:k, :]
    top = ra[0:1] + rb[0:1]
    z = jnp.sum(jnp.where(cand >= tau, jnp.exp(cand - top), 0.0), axis=0, keepdims=True)
    thr = jnp.full(s0.shape, jnp.inf, F32)
    for r in range(k):
        row_thr = jnp.min(jnp.where(ra[r:r + 1] + rb >= tau, rb, jnp.inf), axis=0, keepdims=True)
        thr = jnp.where(s0 == ra[r:r + 1], row_thr, thr)
    thr_ref[0] = thr
    s1_ref[0] = s1
    a_ref[0] = jnp.exp(s0 - ra[0:1]) / z
    b_ref[0] = jnp.exp(s1 - rb[0:1])


def _router(ht, wq_t, keys, *, tm=1024):
    d, m = ht.shape
    heads, _, n_keys, sub = keys.shape
    tm = min(tm, m)
    big = jax.ShapeDtypeStruct((heads, n_keys, m), F32)
    big_spec = pl.BlockSpec((1, n_keys, tm), lambda i, h: (h, 0, i))
    return pl.pallas_call(
        _router_kernel,
        grid=(m // tm, heads),
        in_specs=[pl.BlockSpec((d, tm), lambda i, h: (0, i)),
                  pl.BlockSpec((2 * sub, d), lambda i, h: (h, 0)),
                  pl.BlockSpec((1, 2, n_keys, sub), lambda i, h: (h, 0, 0, 0))],
        out_specs=[big_spec, big_spec, big_spec, big_spec],
        out_shape=[big, big, big, big],
        scratch_shapes=[pltpu.VMEM((PEER_TOPK, tm), F32)] * 3,
        compiler_params=_params(("parallel", "arbitrary"),
                                2 * d * tm * 2 + 2 * 2 * sub * d * 2 + 8 * n_keys * tm * 4 + 16 * 2**20),
        name="peer_router",
    )(ht, wq_t, keys)


def _expert_table_kernel(x_ref, q_ref, inv_ref, nrm_ref, *, transpose):
    x = x_ref[...]
    amax = jnp.maximum(jnp.max(jnp.abs(x), axis=1, keepdims=True), TINY)
    xs = x * (F8_TARGET / amax)
    q_ref[...] = (xs.T if transpose else xs).astype(q_ref.dtype)
    inv_ref[...] = jnp.broadcast_to(amax * (1.0 / F8_TARGET), inv_ref.shape)
    nrm_ref[...] = jnp.broadcast_to(jnp.sqrt(jnp.sum(x * x, axis=1, keepdims=True)), nrm_ref.shape)


def _expert_table(x, *, transpose, te=256):
    n_exp, d = x.shape
    side = jax.ShapeDtypeStruct((n_exp, LANES), F32)
    side_spec = pl.BlockSpec((te, LANES), lambda e: (e, 0))
    return pl.pallas_call(
        functools.partial(_expert_table_kernel, transpose=transpose),
        grid=(n_exp // te,),
        in_specs=[pl.BlockSpec((te, d), lambda e: (e, 0))],
        out_specs=[pl.BlockSpec((d, te), lambda e: (0, e)) if transpose else pl.BlockSpec((te, d), lambda e: (e, 0)),
                   side_spec, side_spec],
        out_shape=[jax.ShapeDtypeStruct((d, n_exp) if transpose else (n_exp, d), F8), side, side],
        compiler_params=_params(("parallel",), 6 * te * d * 4 + 4 * 2**20),
        name="expert_table_t" if transpose else "expert_table",
    )(x)


def _peer_dense_kernel(h_ref, u_ref, vt_ref, ui_ref, vi_ref, sc_ref, thr_ref, s1_ref, a_ref, b_ref, o_ref, g_sc):
    e = pl.program_id(1)
    heads, n_keys, tm = s1_ref.shape
    te = u_ref.shape[0]
    sub = min(te, PEER_SUBTILE)

    @pl.when(e == 0)
    def _():
        o_ref[...] = jnp.zeros(o_ref.shape, F32)

    for ii in range(te // n_keys):
        i = e * (te // n_keys) + ii
        gates = jnp.zeros((n_keys, tm), F32)
        for h in range(heads):
            gates = gates + (jnp.where(s1_ref[h] >= thr_ref[h, pl.ds(i, 1), :], b_ref[h], 0.0)
                             * a_ref[h, pl.ds(i, 1), :])
        g_sc[ii * n_keys:(ii + 1) * n_keys, :] = gates
    nsub = te // sub
    act_scale, w_scale = sc_ref[0:1, :], sc_ref[1:2, :]
    across = lambda col: jnp.concatenate([col] * (tm // LANES), axis=1)
    acts = [jnp.dot(u_ref[t * sub:(t + 1) * sub, :], h_ref[...], preferred_element_type=F32)
            * across(ui_ref[t * sub:(t + 1) * sub, :]) * act_scale for t in range(nsub)]
    for t in range(nsub):
        gel = 0.5 * acts[t] * (1.0 + lax.erf(acts[t] * (1.0 / math.sqrt(2.0))))
        w = (g_sc[t * sub:(t + 1) * sub, :] * gel * across(vi_ref[t * sub:(t + 1) * sub, :])
             * w_scale).astype(vt_ref.dtype)
        o_ref[...] += jnp.dot(vt_ref[:, t * sub:(t + 1) * sub], w, preferred_element_type=F32)


def _peer_dense(ht, u, vt, u_inv, v_inv, scales, thr, s1, a, bf, *, tm=512, te=1024):
    d, m = ht.shape
    n_exp = u.shape[0]
    heads, n_keys, _ = s1.shape
    tm = min(tm, m)
    once = pl.Buffered(1)
    tok = lambda i, e: (0, 0, i)
    big_spec = pl.BlockSpec((heads, n_keys, tm), tok, pipeline_mode=once)
    return pl.pallas_call(
        _peer_dense_kernel,
        grid=(m // tm, n_exp // te),
        in_specs=[pl.BlockSpec((d, tm), lambda i, e: (0, i), pipeline_mode=once),
                  pl.BlockSpec((te, d), lambda i, e: (e, 0)),
                  pl.BlockSpec((d, te), lambda i, e: (0, e)),
                  pl.BlockSpec((te, LANES), lambda i, e: (e, 0)),
                  pl.BlockSpec((te, LANES), lambda i, e: (e, 0)),
                  pl.BlockSpec((2, tm), lambda i, e: (0, i), pipeline_mode=once),
                  big_spec, big_spec, big_spec, big_spec],
        out_specs=pl.BlockSpec((d, tm), lambda i, e: (0, i)),
        out_shape=jax.ShapeDtypeStruct((d, m), F32),
        scratch_shapes=[pltpu.VMEM((te, tm), F32)],
        compiler_params=_params(("parallel", "arbitrary"),
                                d * tm * 2 + 4 * te * d * 2 + 2 * d * tm * 4 + 4 * heads * n_keys * tm * 4
                                + 6 * te * tm * 4 + d * tm * 4),
        name="peer_dense",
    )(ht, u, vt, u_inv, v_inv, scales, thr, s1, a, bf)


def _final_kernel(x_ref, pt_ref, ps_ref, g_ref, gain_ref, o_ref):
    x = x_ref[...] + g_ref[0] * (pt_ref[...] * ps_ref[...]).T
    ms = jnp.mean(x * x, axis=-1, keepdims=True)
    o_ref[...] = x * lax.rsqrt(ms + EPS) * gain_ref[...]


def _final(x1, pt, pscale, gate, gain, *, t, tm=256):
    m, d = x1.shape
    tm = min(tm, t)
    return pl.pallas_call(
        _final_kernel,
        grid=(m // tm,),
        in_specs=[pl.BlockSpec((tm, d), lambda i: (i, 0)),
                  pl.BlockSpec((d, tm), lambda i: (0, i)),
                  pl.BlockSpec((1, tm), lambda i: (0, i)),
                  pl.BlockSpec((1, 1, d), lambda i: ((i * tm) // t, 0, 0)),
                  pl.BlockSpec((1, d), lambda i: (0, 0))],
        out_specs=pl.BlockSpec((tm, d), lambda i: (i, 0)),
        out_shape=jax.ShapeDtypeStruct((m, d), F32),
        compiler_params=_params(("parallel",), 8 * tm * d * 4 + 4 * 2**20),
        name="final_norm",
    )(x1, pt, pscale, gate, gain.reshape(1, d))


def _trunk(x, ada, w, depth_index=0):
    b, t, d = x.shape
    nh = d // (2 * LANES)
    lambda_init = 0.8 - 0.6 * math.exp(-0.3 * depth_index)
    sh1, sc1, g1, sh2, sc2, g2 = [a.reshape(b, 1, d) for a in jnp.split(ada, 6, axis=-1)]

    h = _norm_mod(x, w["norm1_gain"], sc1, sh1, transpose=False)
    p = _inproj(h, w["w_in"], w["colscale"])
    ma = _diff_attention(p, w["diff_lambda"], w["diff_subln_gain"], b=b, t=t, nh=nh, lambda_init=lambda_init)
    md = _dil_attention(p, w["dil_nslopes"], b=b, t=t, nh=nh)
    x1 = _outproj(ma.reshape(b * t, nh * LANES), md.reshape(b * t, nh * LANES), w["w_out"],
                  x.reshape(b * t, d), g1, t=t)

    ht, ht8, stat = _norm_mod(x1.reshape(b, t, d), w["norm2_gain"], sc2, sh2, transpose=True)
    thr, s1, a, bf = _router(ht, w["wq_t"], w["sub_keys"])
    w_scale = F8_TARGET / jnp.maximum(stat[1:2] * w["uv_bound"], TINY)
    pt = _peer_dense(ht8, w["peer_u"], w["peer_vt"], w["u_inv"], w["v_inv"],
                     jnp.concatenate([stat[0:1], w_scale], axis=0), thr, s1, a, bf)
    y = _final(x1, pt, 1.0 / w_scale, g2, w["final_gain"], t=t)
    return y.reshape(b, t, d)


def kernel(x_prompt, x_sample, c_prompt, c_sample, norm1_gain, norm2_gain, w_ada, b_ada, w_in, diff_lambda,
           diff_subln_gain, w_out, peer_w_query, peer_sub_keys, peer_u, peer_v, final_gain):
    d = x_prompt.shape[-1]
    nh = d // (2 * LANES)
    bp, bs = c_prompt.shape[0], c_sample.shape[0]
    assert norm1_gain.shape[0] == 1, "single-layer trunk"
    assert peer_sub_keys.shape[3] == LANES and peer_sub_keys.shape[4] == LANES

    rows = -(-(bp + bs) // 8) * 8
    c_all = jnp.concatenate([c_prompt, c_sample, jnp.zeros((rows - bp - bs, d), F32)], axis=0)
    ada = _ada(c_all, w_ada[0], b_ada[0])

    colscale = np.ones((1, w_in.shape[-1]), np.float32)
    colscale[0, :nh * LANES] = DIFF_HALF_DIM ** -0.5 * math.log2(math.e)
    colscale[0, 3 * nh * LANES:4 * nh * LANES] = LANES ** -0.5 * math.log2(math.e)
    lane_bcast = lambda v: jnp.asarray(np.broadcast_to(np.asarray(v, np.float32)[:, None, None], (nh, 1, LANES)))
    w = dict(
        norm1_gain=norm1_gain[0], norm2_gain=norm2_gain[0], final_gain=final_gain,
        w_in=w_in[0].astype(BF16), colscale=jnp.asarray(colscale),
        dil_nslopes=lane_bcast(-_alibi_slopes(nh) * math.log2(math.e)),
        diff_lambda=diff_lambda[0], diff_subln_gain=diff_subln_gain[0],
        w_out=w_out[0].astype(BF16),
        wq_t=peer_w_query[0].T.astype(BF16), sub_keys=peer_sub_keys[0].astype(BF16),
    )
    peer_u8, u_inv, u_nrm = _expert_table(peer_u[0], transpose=False)
    peer_vt8, v_inv, _ = _expert_table(peer_v[0], transpose=True)
    w.update(peer_u=peer_u8, peer_vt=peer_vt8, u_inv=u_inv, v_inv=v_inv,
             uv_bound=jnp.max(u_nrm[:, 0]) * jnp.max(v_inv[:, 0]))
    y_prompt = _trunk(x_prompt, ada[:bp], w)
    y_sample = _trunk(x_sample, ada[bp:bp + bs], w)
    return (y_prompt, y_sample)
```

```python
import functools
import math

import ml_dtypes
import numpy as np
import jax
import jax.numpy as jnp
from jax import lax
from jax.experimental import pallas as pl
from jax.experimental.pallas import tpu as pltpu

F32 = jnp.float32
BF16 = jnp.bfloat16
F8 = jnp.float8_e4m3fn
F8_TARGET = 240.0
TINY = 1e-30
EPS = 1e-6
MASK_VALUE = -1e30
LANES = 128
VMEM_CAP_BYTES = 60000 * 1024
DIFF_HALF_DIM = 64
DIFF_KEY_BLOCK = 256
DIFF_SKIP_MIN_CHUNKS = 8
DIFF_SKIP_MARGIN = 140.0
DIFF_PAD_ROWS = 16
DIL_CONFIGS = ((128, 1), (512, 4), (2048, 16))
DIL_UNROLL = 32
PEER_TOPK = 16
PEER_SUBTILE = 256
NT_DIMS = (((1,), (1,)), ((), ()))


def _params(semantics, vmem_bytes):
    return pltpu.CompilerParams(
        dimension_semantics=semantics,
        vmem_limit_bytes=int(min(max(vmem_bytes, 16 * 2**20), VMEM_CAP_BYTES)))


def _alibi_slopes(n):
    return 2.0 ** (-8.0 * np.arange(1, n + 1) / n)


def _ada_kernel(c_ref, w_ref, b_ref, o_ref):
    c = c_ref[...]
    a = (c / (1.0 + jnp.exp(-c))).astype(BF16)
    o_ref[...] = jnp.dot(a, w_ref[...].astype(BF16), preferred_element_type=F32) + b_ref[...]


def _ada(c_all, w_ada, b_ada):
    rows, d = c_all.shape
    n = w_ada.shape[1]
    tn = 512
    return pl.pallas_call(
        _ada_kernel,
        grid=(n // tn,),
        in_specs=[pl.BlockSpec((rows, d), lambda j: (0, 0)),
                  pl.BlockSpec((d, tn), lambda j: (0, j)),
                  pl.BlockSpec((1, tn), lambda j: (0, j))],
        out_specs=pl.BlockSpec((rows, tn), lambda j: (0, j)),
        out_shape=jax.ShapeDtypeStruct((rows, n), F32),
        compiler_params=_params(("arbitrary",), 2 * d * tn * 4 + d * tn * 2 + 4 * 2**20),
        name="ada",
    )(c_all, w_ada, b_ada.reshape(1, n))


def _norm_mod_kernel(x_ref, gain_ref, sc_ref, sh_ref, o_ref, *rest, transpose):
    x = x_ref[0]
    ms = jnp.mean(x * x, axis=-1, keepdims=True)
    y = x * lax.rsqrt(ms + EPS) * gain_ref[...]
    h = y * (1.0 + sc_ref[0]) + sh_ref[0]
    if transpose:
        q_ref, stat_ref = rest
        ht = h.T
        o_ref[...] = ht.astype(o_ref.dtype)
        amax = jnp.maximum(jnp.max(jnp.abs(ht), axis=0, keepdims=True), TINY)
        q_ref[...] = (ht * (F8_TARGET / amax)).astype(q_ref.dtype)
        stat_ref[0:1, :] = amax * (1.0 / F8_TARGET)
        stat_ref[1:2, :] = jnp.sqrt(jnp.sum(ht * ht, axis=0, keepdims=True))
    else:
        o_ref[...] = h.astype(o_ref.dtype)


def _norm_mod(x, gain, sc, sh, *, transpose, tm=256):
    b, t, d = x.shape
    m = b * t
    tm = min(tm, t)
    nt = t // tm
    if transpose:
        col = lambda bi, i: (0, bi * nt + i)
        out_spec = [pl.BlockSpec((d, tm), col), pl.BlockSpec((d, tm), col), pl.BlockSpec((2, tm), col)]
        out_shape = [jax.ShapeDtypeStruct((d, m), BF16), jax.ShapeDtypeStruct((d, m), F8),
                     jax.ShapeDtypeStruct((2, m), F32)]
    else:
        out_spec = pl.BlockSpec((tm, d), lambda bi, i: (bi * nt + i, 0))
        out_shape = jax.ShapeDtypeStruct((m, d), BF16)
    return pl.pallas_call(
        functools.partial(_norm_mod_kernel, transpose=transpose),
        grid=(b, nt),
        in_specs=[pl.BlockSpec((1, tm, d), lambda bi, i: (bi, i, 0)),
                  pl.BlockSpec((1, d), lambda bi, i: (0, 0)),
                  pl.BlockSpec((1, 1, d), lambda bi, i: (bi, 0, 0)),
                  pl.BlockSpec((1, 1, d), lambda bi, i: (bi, 0, 0))],
        out_specs=out_spec,
        out_shape=out_shape,
        compiler_params=_params(("parallel", "parallel"), 8 * tm * d * 4 + 4 * 2**20),
        name="norm_mod_t" if transpose else "norm_mod",
    )(x, gain.reshape(1, d), sc, sh)


def _inproj_kernel(a_ref, w_ref, cs_ref, o_ref):
    acc = jnp.dot(a_ref[...], w_ref[...], preferred_element_type=F32) * cs_ref[...]
    for c in range(o_ref.shape[0]):
        o_ref[c] = acc[:, c * LANES:(c + 1) * LANES].astype(o_ref.dtype)


def _inproj(h, w, colscale, *, tm=1024, tn=1024):
    m, k = h.shape
    n = w.shape[1]
    tm, tn = min(tm, m), min(tn, n)
    nb = tn // LANES
    return pl.pallas_call(
        _inproj_kernel,
        grid=(m // tm, n // tn),
        in_specs=[pl.BlockSpec((tm, k), lambda i, j: (i, 0)),
                  pl.BlockSpec((k, tn), lambda i, j: (0, j)),
                  pl.BlockSpec((1, tn), lambda i, j: (0, j))],
        out_specs=pl.BlockSpec((nb, tm, LANES), lambda i, j: (j, i, 0)),
        out_shape=jax.ShapeDtypeStruct((n // LANES, m, LANES), BF16),
        compiler_params=_params(("parallel", "arbitrary"),
                                2 * (tm * k + k * tn + tm * tn) * 2 + 2 * tm * tn * 4 + 2 * 2**20),
        name="inproj",
    )(h, w, colscale)


def _alibi_tables(nh, tq, tk):
    bf = lambda a: np.asarray(a, np.float64).astype(ml_dtypes.bfloat16).astype(np.float64)
    slope = _alibi_slopes(nh) * math.log2(math.e)
    hi = bf(slope)
    lo = bf(slope - hi)
    augq = np.zeros((nh, tq, LANES), np.float64)
    augk = np.zeros((nh, 2, tk, LANES), np.float64)
    qpos, kpos = np.arange(tq), np.arange(tk)
    for h in range(nh):
        qcols = [qpos // 16, qpos // 16, qpos % 16, qpos % 16,
                 16 * hi[h] + 0 * qpos, 16 * lo[h] + 0 * qpos, hi[h] + 0 * qpos, lo[h] + 0 * qpos]
        kcols = [-16 * hi[h] + 0 * kpos, -16 * lo[h] + 0 * kpos, -hi[h] + 0 * kpos, -lo[h] + 0 * kpos,
                 kpos // 16, kpos // 16, kpos % 16, kpos % 16]
        for c in range(8):
            for base in (0, DIFF_HALF_DIM):
                augq[h, :, base + c] = qcols[c]
                augk[h, 0, :, base + c] = kcols[c]
                augk[h, 1, :, base + c] = -kcols[c]
    nslope = np.broadcast_to(-(hi + lo)[:, None, None], (nh, 1, LANES))
    return (jnp.asarray(augq, BF16), jnp.asarray(augk, BF16), jnp.asarray(nslope, F32))


def _diff_attn_kernel(q_ref, qall_ref, k_ref, v_ref, augq_ref, augk_ref, ns_ref, sl_ref, lp_ref, gain_ref, o_ref,
                      vt_sc, st0, st1, p0, p1, al0, al1, m_sc, acc_sc, qmax_sm, kmax_sm, *, tile, lambda_init, skip):
    qi = pl.program_id(2)
    t = k_ref.shape[1]
    n = t // tile
    kb = min(tile, DIFF_KEY_BLOCK)
    groups = tile // LANES

    @pl.when(qi == 0)
    def _():
        ones_row = jnp.where(lax.broadcasted_iota(jnp.int32, (DIFF_PAD_ROWS, tile), 0) == 0, 1.0, 0.0)
        for c in range(n):
            vt_sc[c, :LANES, :] = v_ref[0, c * tile:(c + 1) * tile, :].astype(F32).T.astype(BF16)
            vt_sc[c, LANES:, :] = ones_row.astype(BF16)
            if skip:
                kf = k_ref[0, c * tile:(c + 1) * tile, :].astype(F32)
                kmax_sm[c] = jnp.max(jnp.sqrt(jnp.max(jnp.sum(kf * kf, axis=1, keepdims=True), axis=0, keepdims=True)))
                qf = qall_ref[0, c * tile:(c + 1) * tile, :].astype(F32)
                qmax_sm[c] = jnp.max(jnp.sqrt(jnp.max(jnp.sum(qf * qf, axis=1, keepdims=True), axis=0, keepdims=True)))

    q = q_ref[0]
    lane = lax.broadcasted_iota(jnp.int32, q.shape, 1)
    low = lane < DIFF_HALF_DIM
    aq = augq_ref[0]
    qxt = [jnp.where(low, q, aq).astype(F32).T.astype(BF16),
           jnp.where(low, aq, q).astype(F32).T.astype(BF16)]
    nslope = ns_ref[0][:1, :1]
    m_sc[...] = jnp.full(m_sc.shape, MASK_VALUE, F32)
    acc_sc[...] = jnp.zeros(acc_sc.shape, F32)
    st_sc, p_sc, al_sc = (st0, st1), (p0, p1), (al0, al1)

    if skip:
        bq = qmax_sm[qi]
        slope = sl_ref[pl.program_id(1)]
        floor_m = -(bq * kmax_sm[qi]) - DIFF_SKIP_MARGIN

        def dead(j):
            gap = ((jnp.abs(qi - j) - 1) * tile + 1).astype(F32)
            return bq * kmax_sm[j] - slope * gap < floor_m

        lo = jnp.int32(0)
        for j in range(n - 1):
            lo = lo + ((lo == j) & (j < qi) & dead(j)).astype(jnp.int32)
        hi = jnp.int32(n - 1)
        for j in range(n - 1, 0, -1):
            hi = hi - ((hi == j) & (j > qi) & dead(j)).astype(jnp.int32)
        odd = ((hi - lo + 1 - n) & 1) == 1
        lo, hi = jnp.where(odd & (lo > 0), lo - 1, lo), jnp.where(odd & (lo == 0), hi + 1, hi)
    else:
        lo, hi = 0, n - 1
    cnt = hi - lo + 1

    def chunk_at(pos):
        nxt = lo + pos - 1
        return jnp.where(pos == 0, qi, nxt + (nxt >= qi).astype(jnp.int32))

    def scores(j, side):
        rows = pl.ds(pl.multiple_of(j * tile, tile), tile)
        k, ak = k_ref[0, rows, :], augk_ref[0, side]
        return jnp.concatenate([jnp.dot(jnp.where(low, k, ak), qxt[0], preferred_element_type=F32),
                                jnp.dot(jnp.where(low, ak, k), qxt[1], preferred_element_type=F32)],
                               axis=1)

    def softmax(j, buf):
        offset = nslope * (jnp.abs(qi - j) * tile).astype(F32)
        for g in range(2 * groups):
            cols = slice(g * LANES, (g + 1) * LANES)
            m_old = m_sc[:, cols]
            mx = m_old
            for r in range(tile // kb):
                s = st_sc[buf][r * kb:(r + 1) * kb, cols]
                mx = jnp.maximum(mx, jnp.max(s, axis=0, keepdims=True) + offset)
            al_sc[buf][:, cols] = jnp.exp2(m_old - mx)
            m_sc[:, cols] = mx
            sub = mx - offset
            for r in range(tile // kb):
                rows = slice(r * kb, (r + 1) * kb)
                p_sc[buf][rows, cols] = jnp.exp2(st_sc[buf][rows, cols] - sub).astype(BF16)

    def accumulate(j, buf):
        acc_sc[...] = al_sc[buf][...] * acc_sc[...] + jnp.dot(vt_sc[j], p_sc[buf][...], preferred_element_type=F32)

    p_sc[1][...] = jnp.zeros(p_sc[1].shape, BF16)
    al_sc[1][...] = jnp.ones(al_sc[1].shape, F32)
    st_sc[0][...] = jnp.minimum(scores(qi, 0), scores(qi, 1))

    def step(pos, buf):
        nxt = chunk_at(pos + 1)
        st_sc[1 - buf][...] = scores(nxt, (nxt > qi).astype(jnp.int32))
        softmax(chunk_at(pos), buf)
        accumulate(chunk_at(jnp.maximum(pos - 1, 0)), 1 - buf)

    def body(i, carry):
        step(2 * i, 0)
        step(2 * i + 1, 1)
        return carry

    lax.fori_loop(0, (cnt - 1) // 2, body, 0)
    if (n - 1) % 2:
        step(cnt - 2, (n - 2) & 1)
    last = (n - 1) & 1
    softmax(chunk_at(cnt - 1), last)
    if n > 1:
        accumulate(chunk_at(jnp.maximum(cnt - 2, 0)), 1 - last)
    accumulate(chunk_at(cnt - 1), last)

    lp = lp_ref[...]
    lam = (jnp.exp(jnp.sum(lp[0:1] * lp[1:2], axis=1, keepdims=True))
           - jnp.exp(jnp.sum(lp[2:3] * lp[3:4], axis=1, keepdims=True)) + lambda_init)
    on = acc_sc[:LANES, :] / acc_sc[LANES:LANES + 1, :]
    o = on[:, :tile] - lam * on[:, tile:]
    o = o * lax.rsqrt(jnp.mean(o * o, axis=0, keepdims=True) + EPS) * (1.0 - lambda_init)
    o_ref[0] = (o.T * gain_ref[...]).astype(o_ref.dtype)


def _diff_attention(p, lam_params, subln_gain, *, b, t, nh, lambda_init, tile=512):
    tile = min(tile, t)
    nq = t // tile
    augq, augk, nslopes = _alibi_tables(nh, tile, tile)
    skip = nq >= DIFF_SKIP_MIN_CHUNKS
    return pl.pallas_call(
        functools.partial(_diff_attn_kernel, tile=tile, lambda_init=lambda_init, skip=skip),
        grid=(b, nh, nq),
        in_specs=[pl.BlockSpec((1, tile, LANES), lambda bi, h, qi: (h, bi * nq + qi, 0)),
                  pl.BlockSpec((1, t, LANES), lambda bi, h, qi: (h, bi, 0)),
                  pl.BlockSpec((1, t, LANES), lambda bi, h, qi: (nh + h, bi, 0)),
                  pl.BlockSpec((1, t, LANES), lambda bi, h, qi: (2 * nh + h, bi, 0)),
                  pl.BlockSpec((1, tile, LANES), lambda bi, h, qi: (h, 0, 0)),
                  pl.BlockSpec((1, 2, tile, LANES), lambda bi, h, qi: (h, 0, 0, 0)),
                  pl.BlockSpec((1, 1, LANES), lambda bi, h, qi: (h, 0, 0)),
                  pl.BlockSpec(memory_space=pltpu.SMEM),
                  pl.BlockSpec(lam_params.shape, lambda bi, h, qi: (0, 0)),
                  pl.BlockSpec((1, LANES), lambda bi, h, qi: (0, 0))],
        out_specs=pl.BlockSpec((1, tile, LANES), lambda bi, h, qi: (bi, qi, h)),
        out_shape=jax.ShapeDtypeStruct((b, t, nh * LANES), BF16),
        scratch_shapes=[pltpu.VMEM((t // tile, LANES + DIFF_PAD_ROWS, tile), BF16),
                        pltpu.VMEM((tile, 2 * tile), F32), pltpu.VMEM((tile, 2 * tile), F32),
                        pltpu.VMEM((tile, 2 * tile), BF16), pltpu.VMEM((tile, 2 * tile), BF16),
                        pltpu.VMEM((1, 2 * tile), F32), pltpu.VMEM((1, 2 * tile), F32),
                        pltpu.VMEM((1, 2 * tile), F32),
                        pltpu.VMEM((LANES + DIFF_PAD_ROWS, 2 * tile), F32),
                        pltpu.SMEM((t // tile,), F32), pltpu.SMEM((t // tile,), F32)],
        compiler_params=_params(("parallel", "parallel", "arbitrary"),
                                7 * t * LANES * 2 + 2 * tile * tile * 20 + 4 * LANES * tile * 4 + 8 * 2**20),
        name="diff_attn",
    )(p, p, p, p, augq, augk, nslopes, -nslopes[:, 0, 0], lam_params, subln_gain.reshape(1, LANES))


def _dil_attn_kernel(q_ref, k_ref, v_ref, ns_ref, o_ref, qf, kf, vf, qg, kg, vg, qr, kr, vr, ob, lb, bt, *, blk):
    t = q_ref.shape[1]
    nslope = ns_ref[0][:, :1]
    srcs = {1: (q_ref.at[0], k_ref.at[0], v_ref.at[0])}
    if any(g > 1 for _, g in DIL_CONFIGS):
        qf[...] = q_ref[0].astype(F32)
        kf[...] = k_ref[0].astype(F32)
        vf[...] = v_ref[0].astype(F32)
    f32_src = {1: (qf, kf, vf)}

    for n, (window, g) in enumerate(DIL_CONFIGS):
        radius = window // (2 * g)
        ln = t // g
        wk = min(blk + 2 * radius, ln)
        nblk = ln // blk
        if g not in srcs:
            gp = max(d for d in f32_src if g % d == 0)
            step, lp = g // gp, t // gp
            keep = any(d > g and d % g == 0 for _, d in DIL_CONFIGS)
            assert not (keep and gp > 1), "one f32 staging level"
            dst16 = (qr, kr, vr)
            dst32 = (qg, kg, vg) if keep else (None, None, None)
            for a_src, a16, a32 in zip(f32_src[gp], dst16, dst32):
                for r in range(gp):
                    for s in range(step):
                        rho = r + gp * s
                        rows = a_src[pl.ds(r * lp + s, ln, stride=step), :]
                        a16[rho * ln:(rho + 1) * ln, :] = rows.astype(BF16)
                        if keep:
                            a32[rho * ln:(rho + 1) * ln, :] = rows
            srcs[g] = dst16
            if keep:
                f32_src[g] = dst32
        qs, ks, vs = srcs[g]
        rel0 = (lax.broadcasted_iota(jnp.int32, (blk, wk), 1)
                - lax.broadcasted_iota(jnp.int32, (blk, wk), 0))
        for idx, shift in enumerate((0, -radius, blk - wk)):
            ar = jnp.abs(rel0 + shift)
            bt[idx, :, :wk] = jnp.where(ar <= radius, (nslope * float(g)) * ar.astype(F32), MASK_VALUE)

        def body(i, carry, n=n, g=g, radius=radius, ln=ln, wk=wk, nblk=nblk, qs=qs, ks=ks, vs=vs):
            rho = i // nblk
            bi = i % nblk
            l0 = bi * blk
            ws = jnp.clip(l0 - radius, 0, ln - wk)
            base = rho * ln
            qb = qs[pl.ds(pl.multiple_of(base + l0, blk), blk), :]
            kw = ks[pl.ds(pl.multiple_of(base + ws, radius), wk), :]
            vw = vs[pl.ds(pl.multiple_of(base + ws, radius), wk), :]
            place = jnp.where(bi == 0, 0, jnp.where(bi == nblk - 1, 2, 1))
            s = lax.dot_general(qb, kw, NT_DIMS, preferred_element_type=F32) + bt[place, :, :wk]
            mx = jnp.max(s, axis=1, keepdims=True)
            e = jnp.exp2(s - mx)
            den = jnp.sum(e, axis=1, keepdims=True)
            o = jnp.dot(e.astype(BF16), vw, preferred_element_type=F32) / den
            lse = mx + jnp.log2(den)
            rows = pl.ds(l0 * g + rho, blk, stride=g) if g > 1 else pl.ds(pl.multiple_of(l0, blk), blk)
            ob[n, rows, :] = o
            lb[n, rows, :] = jnp.broadcast_to(lse, (blk, LANES))
            return carry

        lax.fori_loop(0, g * nblk, body, 0, unroll=min(DIL_UNROLL, g * nblk))

    chunk = min(512, t)

    def merge(i, carry):
        rows = pl.ds(pl.multiple_of(i * chunk, chunk), chunk)
        l0, l1, l2 = lb[0, rows, :], lb[1, rows, :], lb[2, rows, :]
        mx = jnp.maximum(jnp.maximum(l0, l1), l2)
        w0, w1, w2 = jnp.exp2(l0 - mx), jnp.exp2(l1 - mx), jnp.exp2(l2 - mx)
        o = (w0 * ob[0, rows, :] + w1 * ob[1, rows, :] + w2 * ob[2, rows, :]) / (w0 + w1 + w2)
        o_ref[0, rows, :] = o.astype(o_ref.dtype)
        return carry

    lax.fori_loop(0, t // chunk, merge, 0)


def _dil_attention(p, nslopes, *, b, t, nh, blk=128):
    nbr = len(DIL_CONFIGS)
    return pl.pallas_call(
        functools.partial(_dil_attn_kernel, blk=blk),
        grid=(b, nh),
        in_specs=[pl.BlockSpec((1, t, LANES), lambda bi, h: (3 * nh + h, bi, 0)),
                  pl.BlockSpec((1, t, LANES), lambda bi, h: (4 * nh + h, bi, 0)),
                  pl.BlockSpec((1, t, LANES), lambda bi, h: (5 * nh + h, bi, 0)),
                  pl.BlockSpec((1, 1, LANES), lambda bi, h: (h, 0, 0))],
        out_specs=pl.BlockSpec((1, t, LANES), lambda bi, h: (bi, 0, h)),
        out_shape=jax.ShapeDtypeStruct((b, t, nh * LANES), BF16),
        scratch_shapes=[pltpu.VMEM((t, LANES), F32)] * 6 + [pltpu.VMEM((t, LANES), BF16)] * 3
                       + [pltpu.VMEM((nbr, t, LANES), F32)] * 2 + [pltpu.VMEM((3, blk, 2 * blk), F32)],
        compiler_params=_params(("parallel", "parallel"),
                                8 * t * LANES * 2 + 3 * t * LANES * 10 + 2 * nbr * t * LANES * 4
                                + DIL_UNROLL * blk * 2 * blk * 16 + 8 * 2**20),
        name="dil_attn",
    )(p, p, p, nslopes)


def _outproj_kernel(ma_ref, md_ref, wt_ref, wb_ref, x_ref, g_ref, o_ref):
    acc = (jnp.dot(ma_ref[...], wt_ref[...], preferred_element_type=F32)
           + jnp.dot(md_ref[...], wb_ref[...], preferred_element_type=F32))
    o_ref[...] = x_ref[...] + g_ref[0] * acc


def _outproj(ma, md, w_out, x2d, gate, *, t, tm=1024, tn=512):
    m, kh = ma.shape
    n = w_out.shape[1]
    tm, tn = min(tm, t), min(tn, n)
    return pl.pallas_call(
        _outproj_kernel,
        grid=(m // tm, n // tn),
        in_specs=[pl.BlockSpec((tm, kh), lambda i, j: (i, 0)),
                  pl.BlockSpec((tm, kh), lambda i, j: (i, 0)),
                  pl.BlockSpec((kh, tn), lambda i, j: (0, j)),
                  pl.BlockSpec((kh, tn), lambda i, j: (1, j)),
                  pl.BlockSpec((tm, tn), lambda i, j: (i, j)),
                  pl.BlockSpec((1, 1, tn), lambda i, j: ((i * tm) // t, 0, j))],
        out_specs=pl.BlockSpec((tm, tn), lambda i, j: (i, j)),
        out_shape=jax.ShapeDtypeStruct((m, n), F32),
        compiler_params=_params(("parallel", "arbitrary"),
                                2 * (2 * tm * kh + 2 * kh * tn) * 2 + 5 * tm * tn * 4 + 2 * 2**20),
        name="outproj",
    )(ma, md, w_out, w_out, x2d, gate)


def _take_top_rows(s, rows_ref, count):
    for r in range(count):
        mx = jnp.max(s, axis=0, keepdims=True)
        rows_ref[r:r + 1, :] = mx
        s = jnp.where(s >= mx, -jnp.inf, s)


def _router_kernel(h_ref, wq_ref, keys_ref, thr_ref, s1_ref, a_ref, b_ref, ra_ref, rb_ref, rc_ref):
    k = PEER_TOPK
    sub = keys_ref.shape[2]
    qt = jnp.dot(wq_ref[...], h_ref[...], preferred_element_type=F32)
    s0 = jnp.dot(keys_ref[0, 0], qt[:sub].astype(BF16), preferred_element_type=F32)
    s1 = jnp.dot(keys_ref[0, 1], qt[sub:].astype(BF16), preferred_element_type=F32)
    _take_top_rows(s0, ra_ref, k)
    _take_top_rows(s1, rb_ref, k)
    ra, rb = ra_ref[...], rb_ref[...]
    row8 = lax.broadcasted_iota(jnp.int32, (8, ra.shape[1]), 0)
    pieces = [ra[0:1] + rb, ra[1:2] + rb[0:8]]
    for r in range(2, 8):
        pieces.append(jnp.where(row8 < k // (r + 1), ra[r:r + 1] + rb[0:8], -jnp.inf))
    pieces.append(ra[8:16] + rb[0:1])
    cand = jnp.concatenate(pieces, axis=0)
    _take_top_rows(cand, rc_ref, k)
    tau = rc_ref[k - 1:k, :]
    top = ra[0:1] + rb[0:1]
    z = jnp.sum(jnp.where(cand >= tau, jnp.exp(cand - top), 0.0), axis=0, keepdims=True)
    thr = jnp.full(s0.shape, jnp.inf, F32)
    for r in range(k):
        row_thr = jnp.min(jnp.where(ra[r:r + 1] + rb >= tau, rb, jnp.inf), axis=0, keepdims=True)
        thr = jnp.where(s0 == ra[r:r + 1], row_thr, thr)
    thr_ref[0] = thr
    s1_ref[0] = s1
    a_ref[0] = jnp.exp(s0 - ra[0:1]) / z
    b_ref[0] = jnp.exp(s1 - rb[0:1])


def _router(ht, wq_t, keys, *, tm=1024):
    d, m = ht.shape
    heads, _, n_keys, sub = keys.shape
    tm = min(tm, m)
    big = jax.ShapeDtypeStruct((heads, n_keys, m), F32)
    big_spec = pl.BlockSpec((1, n_keys, tm), lambda i, h: (h, 0, i))
    return pl.pallas_call(
        _router_kernel,
        grid=(m // tm, heads),
        in_specs=[pl.BlockSpec((d, tm), lambda i, h: (0, i)),
                  pl.BlockSpec((2 * sub, d), lambda i, h: (h, 0)),
                  pl.BlockSpec((1, 2, n_keys, sub), lambda i, h: (h, 0, 0, 0))],
        out_specs=[big_spec, big_spec, big_spec, big_spec],
        out_shape=[big, big, big, big],
        scratch_shapes=[pltpu.VMEM((PEER_TOPK, tm), F32)] * 3,
        compiler_params=_params(("parallel", "arbitrary"),
                                2 * d * tm * 2 + 2 * 2 * sub * d * 2 + 8 * n_keys * tm * 4 + 16 * 2**20),
        name="peer_router",
    )(ht, wq_t, keys)


def _expert_table_kernel(x_ref, q_ref, inv_ref, nrm_ref, *, transpose):
    x = x_ref[...]
    amax = jnp.maximum(jnp.max(jnp.abs(x), axis=1, keepdims=True), TINY)
    xs = x * (F8_TARGET / amax)
    q_ref[...] = (xs.T if transpose else xs).astype(q_ref.dtype)
    inv_ref[...] = jnp.broadcast_to(amax * (1.0 / F8_TARGET), inv_ref.shape)
    nrm_ref[...] = jnp.broadcast_to(jnp.sqrt(jnp.sum(x * x, axis=1, keepdims=True)), nrm_ref.shape)


def _expert_table(x, *, transpose, te=256):
    n_exp, d = x.shape
    side = jax.ShapeDtypeStruct((n_exp, LANES), F32)
    side_spec = pl.BlockSpec((te, LANES), lambda e: (e, 0))
    return pl.pallas_call(
        functools.partial(_expert_table_kernel, transpose=transpose),
        grid=(n_exp // te,),
        in_specs=[pl.BlockSpec((te, d), lambda e: (e, 0))],
        out_specs=[pl.BlockSpec((d, te), lambda e: (0, e)) if transpose else pl.BlockSpec((te, d), lambda e: (e, 0)),
                   side_spec, side_spec],
        out_shape=[jax.ShapeDtypeStruct((d, n_exp) if transpose else (n_exp, d), F8), side, side],
        compiler_params=_params(("parallel",), 6 * te * d * 4 + 4 * 2**20),
        name="expert_table_t" if transpose else "expert_table",
    )(x)


def _peer_dense_kernel(h_ref, u_ref, vt_ref, ui_ref, vi_ref, sc_ref, thr_ref, s1_ref, a_ref, b_ref, o_ref, g_sc):
    e = pl.program_id(1)
    heads, n_keys, tm = s1_ref.shape
    te = u_ref.shape[0]
    sub = min(te, PEER_SUBTILE)

    @pl.when(e == 0)
    def _():
        o_ref[...] = jnp.zeros(o_ref.shape, F32)

    for ii in range(te // n_keys):
        i = e * (te // n_keys) + ii
        gates = jnp.zeros((n_keys, tm), F32)
        for h in range(heads):
            gates = gates + (jnp.where(s1_ref[h] >= thr_ref[h, pl.ds(i, 1), :], b_ref[h], 0.0)
                             * a_ref[h, pl.ds(i, 1), :])
        g_sc[ii * n_keys:(ii + 1) * n_keys, :] = gates
    nsub = te // sub
    act_scale, w_scale = sc_ref[0:1, :], sc_ref[1:2, :]
    across = lambda col: jnp.concatenate([col] * (tm // LANES), axis=1)
    acts = [jnp.dot(u_ref[t * sub:(t + 1) * sub, :], h_ref[...], preferred_element_type=F32)
            * across(ui_ref[t * sub:(t + 1) * sub, :]) * act_scale for t in range(nsub)]
    for t in range(nsub):
        gel = 0.5 * acts[t] * (1.0 + lax.erf(acts[t] * (1.0 / math.sqrt(2.0))))
        w = (g_sc[t * sub:(t + 1) * sub, :] * gel * across(vi_ref[t * sub:(t + 1) * sub, :])
             * w_scale).astype(vt_ref.dtype)
        o_ref[...] += jnp.dot(vt_ref[:, t * sub:(t + 1) * sub], w, preferred_element_type=F32)


def _peer_dense(ht, u, vt, u_inv, v_inv, scales, thr, s1, a, bf, *, tm=512, te=1024):
    d, m = ht.shape
    n_exp = u.shape[0]
    heads, n_keys, _ = s1.shape
    tm = min(tm, m)
    once = pl.Buffered(1)
    tok = lambda i, e: (0, 0, i)
    big_spec = pl.BlockSpec((heads, n_keys, tm), tok, pipeline_mode=once)
    return pl.pallas_call(
        _peer_dense_kernel,
        grid=(m // tm, n_exp // te),
        in_specs=[pl.BlockSpec((d, tm), lambda i, e: (0, i), pipeline_mode=once),
                  pl.BlockSpec((te, d), lambda i, e: (e, 0)),
                  pl.BlockSpec((d, te), lambda i, e: (0, e)),
                  pl.BlockSpec((te, LANES), lambda i, e: (e, 0)),
                  pl.BlockSpec((te, LANES), lambda i, e: (e, 0)),
                  pl.BlockSpec((2, tm), lambda i, e: (0, i), pipeline_mode=once),
                  big_spec, big_spec, big_spec, big_spec],
        out_specs=pl.BlockSpec((d, tm), lambda i, e: (0, i)),
        out_shape=jax.ShapeDtypeStruct((d, m), F32),
        scratch_shapes=[pltpu.VMEM((te, tm), F32)],
        compiler_params=_params(("parallel", "arbitrary"),
                                d * tm * 2 + 4 * te * d * 2 + 2 * d * tm * 4 + 4 * heads * n_keys * tm * 4
                                + 6 * te * tm * 4 + d * tm * 4),
        name="peer_dense",
    )(ht, u, vt, u_inv, v_inv, scales, thr, s1, a, bf)


def _final_kernel(x_ref, pt_ref, ps_ref, g_ref, gain_ref, o_ref):
    x = x_ref[...] + g_ref[0] * (pt_ref[...] * ps_ref[...]).T
    ms = jnp.mean(x * x, axis=-1, keepdims=True)
    o_ref[...] = x * lax.rsqrt(ms + EPS) * gain_ref[...]


def _final(x1, pt, pscale, gate, gain, *, t, tm=256):
    m, d = x1.shape
    tm = min(tm, t)
    return pl.pallas_call(
        _final_kernel,
        grid=(m // tm,),
        in_specs=[pl.BlockSpec((tm, d), lambda i: (i, 0)),
                  pl.BlockSpec((d, tm), lambda i: (0, i)),
                  pl.BlockSpec((1, tm), lambda i: (0, i)),
                  pl.BlockSpec((1, 1, d), lambda i: ((i * tm) // t, 0, 0)),
                  pl.BlockSpec((1, d), lambda i: (0, 0))],
        out_specs=pl.BlockSpec((tm, d), lambda i: (i, 0)),
        out_shape=jax.ShapeDtypeStruct((m, d), F32),
        compiler_params=_params(("parallel",), 8 * tm * d * 4 + 4 * 2**20),
        name="final_norm",
    )(x1, pt, pscale, gate, gain.reshape(1, d))


def _trunk(x, ada, w, depth_index=0):
    b, t, d = x.shape
    nh = d // (2 * LANES)
    lambda_init = 0.8 - 0.6 * math.exp(-0.3 * depth_index)
    sh1, sc1, g1, sh2, sc2, g2 = [a.reshape(b, 1, d) for a in jnp.split(ada, 6, axis=-1)]

    h = _norm_mod(x, w["norm1_gain"], sc1, sh1, transpose=False)
    p = _inproj(h, w["w_in"], w["colscale"])
    ma = _diff_attention(p, w["diff_lambda"], w["diff_subln_gain"], b=b, t=t, nh=nh, lambda_init=lambda_init)
    md = _dil_attention(p, w["dil_nslopes"], b=b, t=t, nh=nh)
    x1 = _outproj(ma.reshape(b * t, nh * LANES), md.reshape(b * t, nh * LANES), w["w_out"],
                  x.reshape(b * t, d), g1, t=t)

    ht, ht8, stat = _norm_mod(x1.reshape(b, t, d), w["norm2_gain"], sc2, sh2, transpose=True)
    thr, s1, a, bf = _router(ht, w["wq_t"], w["sub_keys"])
    w_scale = F8_TARGET / jnp.maximum(stat[1:2] * w["uv_bound"], TINY)
    pt = _peer_dense(ht8, w["peer_u"], w["peer_vt"], w["u_inv"], w["v_inv"],
                     jnp.concatenate([stat[0:1], w_scale], axis=0), thr, s1, a, bf)
    y = _final(x1, pt, 1.0 / w_scale, g2, w["final_gain"], t=t)
    return y.reshape(b, t, d)


def kernel(x_prompt, x_sample, c_prompt, c_sample, norm1_gain, norm2_gain, w_ada, b_ada, w_in, diff_lambda,
           diff_subln_gain, w_out, peer_w_query, peer_sub_keys, peer_u, peer_v, final_gain):
    d = x_prompt.shape[-1]
    nh = d // (2 * LANES)
    bp, bs = c_prompt.shape[0], c_sample.shape[0]
    assert norm1_gain.shape[0] == 1, "single-layer trunk"
    assert peer_sub_keys.shape[3] == LANES and peer_sub_keys.shape[4] == LANES

    rows = -(-(bp + bs) // 8) * 8
    c_all = jnp.concatenate([c_prompt, c_sample, jnp.zeros((rows - bp - bs, d), F32)], axis=0)
    ada = _ada(c_all, w_ada[0], b_ada[0])

    colscale = np.ones((1, w_in.shape[-1]), np.float32)
    colscale[0, :nh * LANES] = DIFF_HALF_DIM ** -0.5 * math.log2(math.e)
    colscale[0, 3 * nh * LANES:4 * nh * LANES] = LANES ** -0.5 * math.log2(math.e)
    lane_bcast = lambda v: jnp.asarray(np.broadcast_to(np.asarray(v, np.float32)[:, None, None], (nh, 1, LANES)))
    w = dict(
        norm1_gain=norm1_gain[0], norm2_gain=norm2_gain[0], final_gain=final_gain,
        w_in=w_in[0].astype(BF16), colscale=jnp.asarray(colscale),
        dil_nslopes=lane_bcast(-_alibi_slopes(nh) * math.log2(math.e)),
        diff_lambda=diff_lambda[0], diff_subln_gain=diff_subln_gain[0],
        w_out=w_out[0].astype(BF16),
        wq_t=peer_w_query[0].T.astype(BF16), sub_keys=peer_sub_keys[0].astype(BF16),
    )
    peer_u8, u_inv, u_nrm = _expert_table(peer_u[0], transpose=False)
    peer_vt8, v_inv, _ = _expert_table(peer_v[0], transpose=True)
    w.update(peer_u=peer_u8, peer_vt=peer_vt8, u_inv=u_inv, v_inv=v_inv,
             uv_bound=jnp.max(u_nrm[:, 0]) * jnp.max(v_inv[:, 0]))
    y_prompt = _trunk(x_prompt, ada[:bp], w)
    y_sample = _trunk(x_sample, ada[bp:bp + bs], w)
    return (y_prompt, y_sample)
```

```python
import functools
import math

import ml_dtypes
import numpy as np
import jax
import jax.numpy as jnp
from jax import lax
from jax.experimental import pallas as pl
from jax.experimental.pallas import tpu as pltpu

F32 = jnp.float32
BF16 = jnp.bfloat16
F8 = jnp.float8_e4m3fn
F8_TARGET = 240.0
TINY = 1e-30
EPS = 1e-6
MASK_VALUE = -1e30
LANES = 128
VMEM_CAP_BYTES = 60000 * 1024
DIFF_HALF_DIM = 64
DIFF_KEY_BLOCK = 256
DIFF_SKIP_MIN_CHUNKS = 8
DIFF_SKIP_MARGIN = 140.0
DIFF_PAD_ROWS = 16
DIL_CONFIGS = ((128, 1), (512, 4), (2048, 16))
DIL_UNROLL = 32
PEER_TOPK = 16
PEER_SUBTILE = 256
NT_DIMS = (((1,), (1,)), ((), ()))


def _params(semantics, vmem_bytes):
    return pltpu.CompilerParams(
        dimension_semantics=semantics,
        vmem_limit_bytes=int(min(max(vmem_bytes, 16 * 2**20), VMEM_CAP_BYTES)))


def _alibi_slopes(n):
    return 2.0 ** (-8.0 * np.arange(1, n + 1) / n)


def _ada_kernel(c_ref, w_ref, b_ref, o_ref):
    c = c_ref[...]
    a = (c / (1.0 + jnp.exp(-c))).astype(BF16)
    o_ref[...] = jnp.dot(a, w_ref[...].astype(BF16), preferred_element_type=F32) + b_ref[...]


def _ada(c_all, w_ada, b_ada):
    rows, d = c_all.shape
    n = w_ada.shape[1]
    tn = 512
    return pl.pallas_call(
        _ada_kernel,
        grid=(n // tn,),
        in_specs=[pl.BlockSpec((rows, d), lambda j: (0, 0)),
                  pl.BlockSpec((d, tn), lambda j: (0, j)),
                  pl.BlockSpec((1, tn), lambda j: (0, j))],
        out_specs=pl.BlockSpec((rows, tn), lambda j: (0, j)),
        out_shape=jax.ShapeDtypeStruct((rows, n), F32),
        compiler_params=_params(("arbitrary",), 2 * d * tn * 4 + d * tn * 2 + 4 * 2**20),
        name="ada",
    )(c_all, w_ada, b_ada.reshape(1, n))


def _norm_mod_kernel(x_ref, gain_ref, sc_ref, sh_ref, o_ref, *rest, transpose):
    x = x_ref[0]
    ms = jnp.mean(x * x, axis=-1, keepdims=True)
    y = x * lax.rsqrt(ms + EPS) * gain_ref[...]
    h = y * (1.0 + sc_ref[0]) + sh_ref[0]
    if transpose:
        q_ref, stat_ref = rest
        ht = h.T
        o_ref[...] = ht.astype(o_ref.dtype)
        amax = jnp.maximum(jnp.max(jnp.abs(ht), axis=0, keepdims=True), TINY)
        q_ref[...] = (ht * (F8_TARGET / amax)).astype(q_ref.dtype)
        stat_ref[0:1, :] = amax * (1.0 / F8_TARGET)
        stat_ref[1:2, :] = jnp.sqrt(jnp.sum(ht * ht, axis=0, keepdims=True))
    else:
        o_ref[...] = h.astype(o_ref.dtype)


def _norm_mod(x, gain, sc, sh, *, transpose, tm=256):
    b, t, d = x.shape
    m = b * t
    tm = min(tm, t)
    nt = t // tm
    if transpose:
        col = lambda bi, i: (0, bi * nt + i)
        out_spec = [pl.BlockSpec((d, tm), col), pl.BlockSpec((d, tm), col), pl.BlockSpec((2, tm), col)]
        out_shape = [jax.ShapeDtypeStruct((d, m), BF16), jax.ShapeDtypeStruct((d, m), F8),
                     jax.ShapeDtypeStruct((2, m), F32)]
    else:
        out_spec = pl.BlockSpec((tm, d), lambda bi, i: (bi * nt + i, 0))
        out_shape = jax.ShapeDtypeStruct((m, d), BF16)
    return pl.pallas_call(
        functools.partial(_norm_mod_kernel, transpose=transpose),
        grid=(b, nt),
        in_specs=[pl.BlockSpec((1, tm, d), lambda bi, i: (bi, i, 0)),
                  pl.BlockSpec((1, d), lambda bi, i: (0, 0)),
                  pl.BlockSpec((1, 1, d), lambda bi, i: (bi, 0, 0)),
                  pl.BlockSpec((1, 1, d), lambda bi, i: (bi, 0, 0))],
        out_specs=out_spec,
        out_shape=out_shape,
        compiler_params=_params(("parallel", "parallel"), 8 * tm * d * 4 + 4 * 2**20),
        name="norm_mod_t" if transpose else "norm_mod",
    )(x, gain.reshape(1, d), sc, sh)


def _inproj_kernel(a_ref, w_ref, cs_ref, o_ref):
    acc = jnp.dot(a_ref[...], w_ref[...], preferred_element_type=F32) * cs_ref[...]
    for c in range(o_ref.shape[0]):
        o_ref[c] = acc[:, c * LANES:(c + 1) * LANES].astype(o_ref.dtype)


def _inproj(h, w, colscale, *, tm=1024, tn=1024):
    m, k = h.shape
    n = w.shape[1]
    tm = min(tm, m)
    tn = max(c for c in (tn, tn // 2, tn // 4, LANES) if n % c == 0)
    assert m % tm == 0
    nb = tn // LANES
    return pl.pallas_call(
        _inproj_kernel,
        grid=(m // tm, n // tn),
        in_specs=[pl.BlockSpec((tm, k), lambda i, j: (i, 0)),
                  pl.BlockSpec((k, tn), lambda i, j: (0, j)),
                  pl.BlockSpec((1, tn), lambda i, j: (0, j))],
        out_specs=pl.BlockSpec((nb, tm, LANES), lambda i, j: (j, i, 0)),
        out_shape=jax.ShapeDtypeStruct((n // LANES, m, LANES), BF16),
        compiler_params=_params(("parallel", "arbitrary"),
                                2 * (tm * k + k * tn + tm * tn) * 2 + 2 * tm * tn * 4 + 2 * 2**20),
        name="inproj",
    )(h, w, colscale)


def _alibi_tables(nh, tq, tk):
    bf = lambda a: np.asarray(a, np.float64).astype(ml_dtypes.bfloat16).astype(np.float64)
    slope = _alibi_slopes(nh) * math.log2(math.e)
    hi = bf(slope)
    lo = bf(slope - hi)
    augq = np.zeros((nh, tq, LANES), np.float64)
    augk = np.zeros((nh, 2, tk, LANES), np.float64)
    qpos, kpos = np.arange(tq), np.arange(tk)
    for h in range(nh):
        qcols = [qpos // 16, qpos // 16, qpos % 16, qpos % 16,
                 16 * hi[h] + 0 * qpos, 16 * lo[h] + 0 * qpos, hi[h] + 0 * qpos, lo[h] + 0 * qpos]
        kcols = [-16 * hi[h] + 0 * kpos, -16 * lo[h] + 0 * kpos, -hi[h] + 0 * kpos, -lo[h] + 0 * kpos,
                 kpos // 16, kpos // 16, kpos % 16, kpos % 16]
        for c in range(8):
            for base in (0, DIFF_HALF_DIM):
                augq[h, :, base + c] = qcols[c]
                augk[h, 0, :, base + c] = kcols[c]
                augk[h, 1, :, base + c] = -kcols[c]
    nslope = np.broadcast_to(-(hi + lo)[:, None, None], (nh, 1, LANES))
    return (jnp.asarray(augq, BF16), jnp.asarray(augk, BF16), jnp.asarray(nslope, F32))


def _diff_attn_kernel(q_ref, qall_ref, k_ref, v_ref, augq_ref, augk_ref, ns_ref, sl_ref, lp_ref, gain_ref, o_ref,
                      vt_sc, st0, st1, p0, p1, al0, al1, m_sc, acc_sc, qmax_sm, kmax_sm, *, tile, lambda_init, skip):
    qi = pl.program_id(2)
    t = k_ref.shape[1]
    n = t // tile
    kb = min(tile, DIFF_KEY_BLOCK)
    groups = tile // LANES

    @pl.when(qi == 0)
    def _():
        ones_row = jnp.where(lax.broadcasted_iota(jnp.int32, (DIFF_PAD_ROWS, tile), 0) == 0, 1.0, 0.0)
        for c in range(n):
            vt_sc[c, :LANES, :] = v_ref[0, c * tile:(c + 1) * tile, :].astype(F32).T.astype(BF16)
            vt_sc[c, LANES:, :] = ones_row.astype(BF16)
            if skip:
                kf = k_ref[0, c * tile:(c + 1) * tile, :].astype(F32)
                kmax_sm[c] = jnp.max(jnp.sqrt(jnp.max(jnp.sum(kf * kf, axis=1, keepdims=True), axis=0, keepdims=True)))
                qf = qall_ref[0, c * tile:(c + 1) * tile, :].astype(F32)
                qmax_sm[c] = jnp.max(jnp.sqrt(jnp.max(jnp.sum(qf * qf, axis=1, keepdims=True), axis=0, keepdims=True)))

    q = q_ref[0]
    lane = lax.broadcasted_iota(jnp.int32, q.shape, 1)
    low = lane < DIFF_HALF_DIM
    aq = augq_ref[0]
    qxt = [jnp.where(low, q, aq).astype(F32).T.astype(BF16),
           jnp.where(low, aq, q).astype(F32).T.astype(BF16)]
    nslope = ns_ref[0][:1, :1]
    m_sc[...] = jnp.full(m_sc.shape, MASK_VALUE, F32)
    acc_sc[...] = jnp.zeros(acc_sc.shape, F32)
    st_sc, p_sc, al_sc = (st0, st1), (p0, p1), (al0, al1)

    if skip:
        bq = qmax_sm[qi]
        slope = sl_ref[pl.program_id(1)]
        floor_m = -(bq * kmax_sm[qi]) - DIFF_SKIP_MARGIN

        def dead(j):
            gap = ((jnp.abs(qi - j) - 1) * tile + 1).astype(F32)
            return bq * kmax_sm[j] - slope * gap < floor_m

        lo = jnp.int32(0)
        for j in range(n - 1):
            lo = lo + ((lo == j) & (j < qi) & dead(j)).astype(jnp.int32)
        hi = jnp.int32(n - 1)
        for j in range(n - 1, 0, -1):
            hi = hi - ((hi == j) & (j > qi) & dead(j)).astype(jnp.int32)
        odd = ((hi - lo + 1 - n) & 1) == 1
        lo, hi = jnp.where(odd & (lo > 0), lo - 1, lo), jnp.where(odd & (lo == 0), hi + 1, hi)
    else:
        lo, hi = 0, n - 1
    cnt = hi - lo + 1

    def chunk_at(pos):
        nxt = lo + pos - 1
        return jnp.where(pos == 0, qi, nxt + (nxt >= qi).astype(jnp.int32))

    def scores(j, side):
        rows = pl.ds(pl.multiple_of(j * tile, tile), tile)
        k, ak = k_ref[0, rows, :], augk_ref[0, side]
        return jnp.concatenate([jnp.dot(jnp.where(low, k, ak), qxt[0], preferred_element_type=F32),
                                jnp.dot(jnp.where(low, ak, k), qxt[1], preferred_element_type=F32)],
                               axis=1)

    def softmax(j, buf):
        offset = nslope * (jnp.abs(qi - j) * tile).astype(F32)
        for g in range(2 * groups):
            cols = slice(g * LANES, (g + 1) * LANES)
            m_old = m_sc[:, cols]
            mx = m_old
            for r in range(tile // kb):
                s = st_sc[buf][r * kb:(r + 1) * kb, cols]
                mx = jnp.maximum(mx, jnp.max(s, axis=0, keepdims=True) + offset)
            al_sc[buf][:, cols] = jnp.exp2(m_old - mx)
            m_sc[:, cols] = mx
            sub = mx - offset
            for r in range(tile // kb):
                rows = slice(r * kb, (r + 1) * kb)
                p_sc[buf][rows, cols] = jnp.exp2(st_sc[buf][rows, cols] - sub).astype(BF16)

    def accumulate(j, buf):
        acc_sc[...] = al_sc[buf][...] * acc_sc[...] + jnp.dot(vt_sc[j], p_sc[buf][...], preferred_element_type=F32)

    p_sc[1][...] = jnp.zeros(p_sc[1].shape, BF16)
    al_sc[1][...] = jnp.ones(al_sc[1].shape, F32)
    st_sc[0][...] = jnp.minimum(scores(qi, 0), scores(qi, 1))

    def step(pos, buf):
        nxt = chunk_at(pos + 1)
        st_sc[1 - buf][...] = scores(nxt, (nxt > qi).astype(jnp.int32))
        softmax(chunk_at(pos), buf)
        accumulate(chunk_at(jnp.maximum(pos - 1, 0)), 1 - buf)

    def body(i, carry):
        step(2 * i, 0)
        step(2 * i + 1, 1)
        return carry

    lax.fori_loop(0, (cnt - 1) // 2, body, 0)
    if (n - 1) % 2:
        step(cnt - 2, (n - 2) & 1)
    last = (n - 1) & 1
    softmax(chunk_at(cnt - 1), last)
    if n > 1:
        accumulate(chunk_at(jnp.maximum(cnt - 2, 0)), 1 - last)
    accumulate(chunk_at(cnt - 1), last)

    lp = lp_ref[...]
    lam = (jnp.exp(jnp.sum(lp[0:1] * lp[1:2], axis=1, keepdims=True))
           - jnp.exp(jnp.sum(lp[2:3] * lp[3:4], axis=1, keepdims=True)) + lambda_init)
    on = acc_sc[:LANES, :] / acc_sc[LANES:LANES + 1, :]
    o = on[:, :tile] - lam * on[:, tile:]
    o = o * lax.rsqrt(jnp.mean(o * o, axis=0, keepdims=True) + EPS) * (1.0 - lambda_init)
    o_ref[0] = (o.T * gain_ref[...]).astype(o_ref.dtype)


def _diff_attention(p, lam_params, subln_gain, *, b, t, nh, lambda_init, tile=512):
    tile = min(tile, t)
    nq = t // tile
    augq, augk, nslopes = _alibi_tables(nh, tile, tile)
    skip = nq >= DIFF_SKIP_MIN_CHUNKS
    return pl.pallas_call(
        functools.partial(_diff_attn_kernel, tile=tile, lambda_init=lambda_init, skip=skip),
        grid=(b, nh, nq),
        in_specs=[pl.BlockSpec((1, tile, LANES), lambda bi, h, qi: (h, bi * nq + qi, 0)),
                  pl.BlockSpec((1, t, LANES), lambda bi, h, qi: (h, bi, 0)),
                  pl.BlockSpec((1, t, LANES), lambda bi, h, qi: (nh + h, bi, 0)),
                  pl.BlockSpec((1, t, LANES), lambda bi, h, qi: (2 * nh + h, bi, 0)),
                  pl.BlockSpec((1, tile, LANES), lambda bi, h, qi: (h, 0, 0)),
                  pl.BlockSpec((1, 2, tile, LANES), lambda bi, h, qi: (h, 0, 0, 0)),
                  pl.BlockSpec((1, 1, LANES), lambda bi, h, qi: (h, 0, 0)),
                  pl.BlockSpec(memory_space=pltpu.SMEM),
                  pl.BlockSpec(lam_params.shape, lambda bi, h, qi: (0, 0)),
                  pl.BlockSpec((1, LANES), lambda bi, h, qi: (0, 0))],
        out_specs=pl.BlockSpec((1, tile, LANES), lambda bi, h, qi: (bi, qi, h)),
        out_shape=jax.ShapeDtypeStruct((b, t, nh * LANES), BF16),
        scratch_shapes=[pltpu.VMEM((t // tile, LANES + DIFF_PAD_ROWS, tile), BF16),
                        pltpu.VMEM((tile, 2 * tile), F32), pltpu.VMEM((tile, 2 * tile), F32),
                        pltpu.VMEM((tile, 2 * tile), BF16), pltpu.VMEM((tile, 2 * tile), BF16),
                        pltpu.VMEM((1, 2 * tile), F32), pltpu.VMEM((1, 2 * tile), F32),
                        pltpu.VMEM((1, 2 * tile), F32),
                        pltpu.VMEM((LANES + DIFF_PAD_ROWS, 2 * tile), F32),
                        pltpu.SMEM((t // tile,), F32), pltpu.SMEM((t // tile,), F32)],
        compiler_params=_params(("parallel", "parallel", "arbitrary"),
                                7 * t * LANES * 2 + 2 * tile * tile * 20 + 4 * LANES * tile * 4 + 8 * 2**20),
        name="diff_attn",
    )(p, p, p, p, augq, augk, nslopes, -nslopes[:, 0, 0], lam_params, subln_gain.reshape(1, LANES))


def _dil_attn_kernel(q_ref, k_ref, v_ref, ns_ref, o_ref, qf, kf, vf, qg, kg, vg, qr, kr, vr, ob, lb, bt, *, blk):
    t = q_ref.shape[1]
    nslope = ns_ref[0][:, :1]
    srcs = {1: (q_ref.at[0], k_ref.at[0], v_ref.at[0])}
    if any(g > 1 for _, g in DIL_CONFIGS):
        qf[...] = q_ref[0].astype(F32)
        kf[...] = k_ref[0].astype(F32)
        vf[...] = v_ref[0].astype(F32)
    f32_src = {1: (qf, kf, vf)}

    for n, (window, g) in enumerate(DIL_CONFIGS):
        radius = window // (2 * g)
        ln = t // g
        wk = min(blk + 2 * radius, ln)
        nblk = ln // blk
        if g not in srcs:
            gp = max(d for d in f32_src if g % d == 0)
            step, lp = g // gp, t // gp
            keep = any(d > g and d % g == 0 for _, d in DIL_CONFIGS)
            assert not (keep and gp > 1), "one f32 staging level"
            dst16 = (qr, kr, vr)
            dst32 = (qg, kg, vg) if keep else (None, None, None)
            for a_src, a16, a32 in zip(f32_src[gp], dst16, dst32):
                for r in range(gp):
                    for s in range(step):
                        rho = r + gp * s
                        rows = a_src[pl.ds(r * lp + s, ln, stride=step), :]
                        a16[rho * ln:(rho + 1) * ln, :] = rows.astype(BF16)
                        if keep:
                            a32[rho * ln:(rho + 1) * ln, :] = rows
            srcs[g] = dst16
            if keep:
                f32_src[g] = dst32
        qs, ks, vs = srcs[g]
        rel0 = (lax.broadcasted_iota(jnp.int32, (blk, wk), 1)
                - lax.broadcasted_iota(jnp.int32, (blk, wk), 0))
        for idx, shift in enumerate((0, -radius, blk - wk)):
            ar = jnp.abs(rel0 + shift)
            bt[idx, :, :wk] = jnp.where(ar <= radius, (nslope * float(g)) * ar.astype(F32), MASK_VALUE)

        def body(i, carry, n=n, g=g, radius=radius, ln=ln, wk=wk, nblk=nblk, qs=qs, ks=ks, vs=vs):
            rho = i // nblk
            bi = i % nblk
            l0 = bi * blk
            ws = jnp.clip(l0 - radius, 0, ln - wk)
            base = rho * ln
            qb = qs[pl.ds(pl.multiple_of(base + l0, blk), blk), :]
            kw = ks[pl.ds(pl.multiple_of(base + ws, radius), wk), :]
            vw = vs[pl.ds(pl.multiple_of(base + ws, radius), wk), :]
            place = jnp.where(bi == 0, 0, jnp.where(bi == nblk - 1, 2, 1))
            s = lax.dot_general(qb, kw, NT_DIMS, preferred_element_type=F32) + bt[place, :, :wk]
            mx = jnp.max(s, axis=1, keepdims=True)
            e = jnp.exp2(s - mx)
            den = jnp.sum(e, axis=1, keepdims=True)
            o = jnp.dot(e.astype(BF16), vw, preferred_element_type=F32) / den
            lse = mx + jnp.log2(den)
            rows = pl.ds(l0 * g + rho, blk, stride=g) if g > 1 else pl.ds(pl.multiple_of(l0, blk), blk)
            ob[n, rows, :] = o
            lb[n, rows, :] = jnp.broadcast_to(lse, (blk, LANES))
            return carry

        lax.fori_loop(0, g * nblk, body, 0, unroll=min(DIL_UNROLL, g * nblk))

    chunk = min(512, t)

    def merge(i, carry):
        rows = pl.ds(pl.multiple_of(i * chunk, chunk), chunk)
        l0, l1, l2 = lb[0, rows, :], lb[1, rows, :], lb[2, rows, :]
        mx = jnp.maximum(jnp.maximum(l0, l1), l2)
        w0, w1, w2 = jnp.exp2(l0 - mx), jnp.exp2(l1 - mx), jnp.exp2(l2 - mx)
        o = (w0 * ob[0, rows, :] + w1 * ob[1, rows, :] + w2 * ob[2, rows, :]) / (w0 + w1 + w2)
        o_ref[0, rows, :] = o.astype(o_ref.dtype)
        return carry

    lax.fori_loop(0, t // chunk, merge, 0)


def _dil_attention(p, nslopes, *, b, t, nh, blk=128):
    nbr = len(DIL_CONFIGS)
    return pl.pallas_call(
        functools.partial(_dil_attn_kernel, blk=blk),
        grid=(b, nh),
        in_specs=[pl.BlockSpec((1, t, LANES), lambda bi, h: (3 * nh + h, bi, 0)),
                  pl.BlockSpec((1, t, LANES), lambda bi, h: (4 * nh + h, bi, 0)),
                  pl.BlockSpec((1, t, LANES), lambda bi, h: (5 * nh + h, bi, 0)),
                  pl.BlockSpec((1, 1, LANES), lambda bi, h: (h, 0, 0))],
        out_specs=pl.BlockSpec((1, t, LANES), lambda bi, h: (bi, 0, h)),
        out_shape=jax.ShapeDtypeStruct((b, t, nh * LANES), BF16),
        scratch_shapes=[pltpu.VMEM((t, LANES), F32)] * 6 + [pltpu.VMEM((t, LANES), BF16)] * 3
                       + [pltpu.VMEM((nbr, t, LANES), F32)] * 2 + [pltpu.VMEM((3, blk, 2 * blk), F32)],
        compiler_params=_params(("parallel", "parallel"),
                                8 * t * LANES * 2 + 3 * t * LANES * 10 + 2 * nbr * t * LANES * 4
                                + DIL_UNROLL * blk * 2 * blk * 16 + 8 * 2**20),
        name="dil_attn",
    )(p, p, p, nslopes)


def _outproj_kernel(ma_ref, md_ref, wt_ref, wb_ref, x_ref, g_ref, o_ref):
    acc = (jnp.dot(ma_ref[...], wt_ref[...], preferred_element_type=F32)
           + jnp.dot(md_ref[...], wb_ref[...], preferred_element_type=F32))
    o_ref[...] = x_ref[...] + g_ref[0] * acc


def _outproj(ma, md, w_out, x2d, gate, *, t, tm=1024, tn=512):
    m, kh = ma.shape
    n = w_out.shape[1]
    tm, tn = min(tm, t), min(tn, n)
    return pl.pallas_call(
        _outproj_kernel,
        grid=(m // tm, n // tn),
        in_specs=[pl.BlockSpec((tm, kh), lambda i, j: (i, 0)),
                  pl.BlockSpec((tm, kh), lambda i, j: (i, 0)),
                  pl.BlockSpec((kh, tn), lambda i, j: (0, j)),
                  pl.BlockSpec((kh, tn), lambda i, j: (1, j)),
                  pl.BlockSpec((tm, tn), lambda i, j: (i, j)),
                  pl.BlockSpec((1, 1, tn), lambda i, j: ((i * tm) // t, 0, j))],
        out_specs=pl.BlockSpec((tm, tn), lambda i, j: (i, j)),
        out_shape=jax.ShapeDtypeStruct((m, n), F32),
        compiler_params=_params(("parallel", "arbitrary"),
                                2 * (2 * tm * kh + 2 * kh * tn) * 2 + 5 * tm * tn * 4 + 2 * 2**20),
        name="outproj",
    )(ma, md, w_out, w_out, x2d, gate)


def _take_top_rows(s, rows_ref, count):
    for r in range(count):
        mx = jnp.max(s, axis=0, keepdims=True)
        rows_ref[r:r + 1, :] = mx
        s = jnp.where(s >= mx, -jnp.inf, s)


def _router_kernel(h_ref, wq_ref, keys_ref, thr_ref, s1_ref, a_ref, b_ref, ra_ref, rb_ref, rc_ref):
    k = PEER_TOPK
    sub = keys_ref.shape[2]
    qt = jnp.dot(wq_ref[...], h_ref[...], preferred_element_type=F32)
    s0 = jnp.dot(keys_ref[0, 0], qt[:sub].astype(BF16), preferred_element_type=F32)
    s1 = jnp.dot(keys_ref[0, 1], qt[sub:].astype(BF16), preferred_element_type=F32)
    _take_top_rows(s0, ra_ref, k)
    _take_top_rows(s1, rb_ref, k)
    ra, rb = ra_ref[...], rb_ref[...]
    row8 = lax.broadcasted_iota(jnp.int32, (8, ra.shape[1]), 0)
    pieces = [ra[0:1] + rb, ra[1:2] + rb[0:8]]
    for r in range(2, 8):
        pieces.append(jnp.where(row8 < k // (r + 1), ra[r:r + 1] + rb[0:8], -jnp.inf))
    pieces.append(ra[8:16] + rb[0:1])
    cand = jnp.concatenate(pieces, axis=0)
    _take_top_rows(cand, rc_ref, k)
    tau = rc_ref[k - 1:k, :]
    top = ra[0:1] + rb[0:1]
    z = jnp.sum(jnp.where(cand >= tau, jnp.exp(cand - top), 0.0), axis=0, keepdims=True)
    thr = jnp.full(s0.shape, jnp.inf, F32)
    for r in range(k):
        row_thr = jnp.min(jnp.where(ra[r:r + 1] + rb >= tau, rb, jnp.inf), axis=0, keepdims=True)
        thr = jnp.where(s0 == ra[r:r + 1], row_thr, thr)
    thr_ref[0] = thr
    s1_ref[0] = s1
    a_ref[0] = jnp.exp(s0 - ra[0:1]) / z
    b_ref[0] = jnp.exp(s1 - rb[0:1])


def _router(ht, wq_t, keys, *, tm=1024):
    d, m = ht.shape
    heads, _, n_keys, sub = keys.shape
    tm = min(tm, m)
    big = jax.ShapeDtypeStruct((heads, n_keys, m), F32)
    big_spec = pl.BlockSpec((1, n_keys, tm), lambda i, h: (h, 0, i))
    return pl.pallas_call(
        _router_kernel,
        grid=(m // tm, heads),
        in_specs=[pl.BlockSpec((d, tm), lambda i, h: (0, i)),
                  pl.BlockSpec((2 * sub, d), lambda i, h: (h, 0)),
                  pl.BlockSpec((1, 2, n_keys, sub), lambda i, h: (h, 0, 0, 0))],
        out_specs=[big_spec, big_spec, big_spec, big_spec],
        out_shape=[big, big, big, big],
        scratch_shapes=[pltpu.VMEM((PEER_TOPK, tm), F32)] * 3,
        compiler_params=_params(("parallel", "arbitrary"),
                                2 * d * tm * 2 + 2 * 2 * sub * d * 2 + 8 * n_keys * tm * 4 + 16 * 2**20),
        name="peer_router",
    )(ht, wq_t, keys)


def _expert_table_kernel(x_ref, q_ref, inv_ref, nrm_ref, *, transpose):
    x = x_ref[...]
    amax = jnp.maximum(jnp.max(jnp.abs(x), axis=1, keepdims=True), TINY)
    xs = x * (F8_TARGET / amax)
    q_ref[...] = (xs.T if transpose else xs).astype(q_ref.dtype)
    inv_ref[...] = jnp.broadcast_to(amax * (1.0 / F8_TARGET), inv_ref.shape)
    nrm_ref[...] = jnp.broadcast_to(jnp.sqrt(jnp.sum(x * x, axis=1, keepdims=True)), nrm_ref.shape)


def _expert_table(x, *, transpose, te=256):
    n_exp, d = x.shape
    side = jax.ShapeDtypeStruct((n_exp, LANES), F32)
    side_spec = pl.BlockSpec((te, LANES), lambda e: (e, 0))
    return pl.pallas_call(
        functools.partial(_expert_table_kernel, transpose=transpose),
        grid=(n_exp // te,),
        in_specs=[pl.BlockSpec((te, d), lambda e: (e, 0))],
        out_specs=[pl.BlockSpec((d, te), lambda e: (0, e)) if transpose else pl.BlockSpec((te, d), lambda e: (e, 0)),
                   side_spec, side_spec],
        out_shape=[jax.ShapeDtypeStruct((d, n_exp) if transpose else (n_exp, d), F8), side, side],
        compiler_params=_params(("parallel",), 6 * te * d * 4 + 4 * 2**20),
        name="expert_table_t" if transpose else "expert_table",
    )(x)


def _peer_dense_kernel(h_ref, u_ref, vt_ref, ui_ref, vi_ref, sc_ref, thr_ref, s1_ref, a_ref, b_ref, o_ref,
                       g_sc, acc_sc):
    e = pl.program_id(1)
    heads, n_keys, tm = s1_ref.shape
    te = u_ref.shape[0]
    sub = min(te, PEER_SUBTILE)

    @pl.when(e == 0)
    def _():
        acc_sc[...] = jnp.zeros(acc_sc.shape, F32)

    for ii in range(te // n_keys):
        i = e * (te // n_keys) + ii
        gates = jnp.zeros((n_keys, tm), F32)
        for h in range(heads):
            gates = gates + (jnp.where(s1_ref[h] >= thr_ref[h, pl.ds(i, 1), :], b_ref[h], 0.0)
                             * a_ref[h, pl.ds(i, 1), :])
        g_sc[ii * n_keys:(ii + 1) * n_keys, :] = gates
    nsub = te // sub
    act_scale, w_scale = sc_ref[0:1, :], sc_ref[1:2, :]
    across = lambda col: jnp.concatenate([col] * (tm // LANES), axis=1)
    acts = [jnp.dot(u_ref[t * sub:(t + 1) * sub, :], h_ref[...], preferred_element_type=F32)
            * across(ui_ref[t * sub:(t + 1) * sub, :]) * act_scale for t in range(nsub)]
    for t in range(nsub):
        gel = 0.5 * acts[t] * (1.0 + lax.erf(acts[t] * (1.0 / math.sqrt(2.0))))
        w = (g_sc[t * sub:(t + 1) * sub, :] * gel * across(vi_ref[t * sub:(t + 1) * sub, :])
             * w_scale).astype(vt_ref.dtype)
        acc_sc[...] += jnp.dot(vt_ref[:, t * sub:(t + 1) * sub], w, preferred_element_type=F32)

    @pl.when(e == pl.num_programs(1) - 1)
    def _():
        o_ref[...] = acc_sc[...].astype(o_ref.dtype)


def _peer_dense(ht, u, vt, u_inv, v_inv, scales, thr, s1, a, bf, *, tm=512, te=1024):
    d, m = ht.shape
    n_exp = u.shape[0]
    heads, n_keys, _ = s1.shape
    tm = min(tm, m)
    once = pl.Buffered(1)
    tok = lambda i, e: (0, 0, i)
    big_spec = pl.BlockSpec((heads, n_keys, tm), tok, pipeline_mode=once)
    return pl.pallas_call(
        _peer_dense_kernel,
        grid=(m // tm, n_exp // te),
        in_specs=[pl.BlockSpec((d, tm), lambda i, e: (0, i), pipeline_mode=once),
                  pl.BlockSpec((te, d), lambda i, e: (e, 0)),
                  pl.BlockSpec((d, te), lambda i, e: (0, e)),
                  pl.BlockSpec((te, LANES), lambda i, e: (e, 0)),
                  pl.BlockSpec((te, LANES), lambda i, e: (e, 0)),
                  pl.BlockSpec((2, tm), lambda i, e: (0, i), pipeline_mode=once),
                  big_spec, big_spec, big_spec, big_spec],
        out_specs=pl.BlockSpec((d, tm), lambda i, e: (0, i)),
        out_shape=jax.ShapeDtypeStruct((d, m), BF16),
        scratch_shapes=[pltpu.VMEM((te, tm), F32), pltpu.VMEM((d, tm), F32)],
        compiler_params=_params(("parallel", "arbitrary"),
                                d * tm * 2 + 4 * te * d * 2 + 2 * d * tm * 4 + 4 * heads * n_keys * tm * 4
                                + 6 * te * tm * 4 + d * tm * 4),
        name="peer_dense",
    )(ht, u, vt, u_inv, v_inv, scales, thr, s1, a, bf)


def _final_kernel(x_ref, pt_ref, ps_ref, g_ref, gain_ref, o_ref):
    x = x_ref[...] + g_ref[0] * (pt_ref[...].astype(F32) * ps_ref[...]).T
    ms = jnp.mean(x * x, axis=-1, keepdims=True)
    o_ref[...] = x * lax.rsqrt(ms + EPS) * gain_ref[...]


def _final(x1, pt, pscale, gate, gain, *, t, tm=256):
    m, d = x1.shape
    tm = min(tm, t)
    return pl.pallas_call(
        _final_kernel,
        grid=(m // tm,),
        in_specs=[pl.BlockSpec((tm, d), lambda i: (i, 0)),
                  pl.BlockSpec((d, tm), lambda i: (0, i)),
                  pl.BlockSpec((1, tm), lambda i: (0, i)),
                  pl.BlockSpec((1, 1, d), lambda i: ((i * tm) // t, 0, 0)),
                  pl.BlockSpec((1, d), lambda i: (0, 0))],
        out_specs=pl.BlockSpec((tm, d), lambda i: (i, 0)),
        out_shape=jax.ShapeDtypeStruct((m, d), F32),
        compiler_params=_params(("parallel",), 8 * tm * d * 4 + 4 * 2**20),
        name="final_norm",
    )(x1, pt, pscale, gate, gain.reshape(1, d))


def _trunk(x, ada, w, depth_index=0):
    b, t, d = x.shape
    nh = d // (2 * LANES)
    lambda_init = 0.8 - 0.6 * math.exp(-0.3 * depth_index)
    sh1, sc1, g1, sh2, sc2, g2 = [a.reshape(b, 1, d) for a in jnp.split(ada, 6, axis=-1)]

    h = _norm_mod(x, w["norm1_gain"], sc1, sh1, transpose=False)
    p = _inproj(h, w["w_in"], w["colscale"])
    ma = _diff_attention(p, w["diff_lambda"], w["diff_subln_gain"], b=b, t=t, nh=nh, lambda_init=lambda_init)
    md = _dil_attention(p, w["dil_nslopes"], b=b, t=t, nh=nh)
    x1 = _outproj(ma.reshape(b * t, nh * LANES), md.reshape(b * t, nh * LANES), w["w_out"],
                  x.reshape(b * t, d), g1, t=t)

    ht, ht8, stat = _norm_mod(x1.reshape(b, t, d), w["norm2_gain"], sc2, sh2, transpose=True)
    thr, s1, a, bf = _router(ht, w["wq_t"], w["sub_keys"])
    w_scale = F8_TARGET / jnp.maximum(stat[1:2] * w["uv_bound"], TINY)
    pt = _peer_dense(ht8, w["peer_u"], w["peer_vt"], w["u_inv"], w["v_inv"],
                     jnp.concatenate([stat[0:1], w_scale], axis=0), thr, s1, a, bf)
    y = _final(x1, pt, 1.0 / w_scale, g2, w["final_gain"], t=t)
    return y.reshape(b, t, d)


def kernel(x_prompt, x_sample, c_prompt, c_sample, norm1_gain, norm2_gain, w_ada, b_ada, w_in, diff_lambda,
           diff_subln_gain, w_out, peer_w_query, peer_sub_keys, peer_u, peer_v, final_gain):
    d = x_prompt.shape[-1]
    nh = d // (2 * LANES)
    bp, bs = c_prompt.shape[0], c_sample.shape[0]
    assert norm1_gain.shape[0] == 1, "single-layer trunk"
    assert peer_sub_keys.shape[3] == LANES and peer_sub_keys.shape[4] == LANES

    rows = -(-(bp + bs) // 8) * 8
    c_all = jnp.concatenate([c_prompt, c_sample, jnp.zeros((rows - bp - bs, d), F32)], axis=0)
    ada = _ada(c_all, w_ada[0], b_ada[0])

    colscale = np.ones((1, w_in.shape[-1]), np.float32)
    colscale[0, :nh * LANES] = DIFF_HALF_DIM ** -0.5 * math.log2(math.e)
    colscale[0, 3 * nh * LANES:4 * nh * LANES] = LANES ** -0.5 * math.log2(math.e)
    lane_bcast = lambda v: jnp.asarray(np.broadcast_to(np.asarray(v, np.float32)[:, None, None], (nh, 1, LANES)))
    w = dict(
        norm1_gain=norm1_gain[0], norm2_gain=norm2_gain[0], final_gain=final_gain,
        w_in=w_in[0].astype(BF16), colscale=jnp.asarray(colscale),
        dil_nslopes=lane_bcast(-_alibi_slopes(nh) * math.log2(math.e)),
        diff_lambda=diff_lambda[0], diff_subln_gain=diff_subln_gain[0],
        w_out=w_out[0].astype(BF16),
        wq_t=peer_w_query[0].T.astype(BF16), sub_keys=peer_sub_keys[0].astype(BF16),
    )
    peer_u8, u_inv, u_nrm = _expert_table(peer_u[0], transpose=False)
    peer_vt8, v_inv, _ = _expert_table(peer_v[0], transpose=True)
    w.update(peer_u=peer_u8, peer_vt=peer_vt8, u_inv=u_inv, v_inv=v_inv,
             uv_bound=jnp.max(u_nrm[:, 0]) * jnp.max(v_inv[:, 0]))
    y_prompt = _trunk(x_prompt, ada[:bp], w)
    y_sample = _trunk(x_sample, ada[bp:bp + bs], w)
    return (y_prompt, y_sample)
```

```python
import functools
import math

import ml_dtypes
import numpy as np
import jax
import jax.numpy as jnp
from jax import lax
from jax.experimental import pallas as pl
from jax.experimental.pallas import tpu as pltpu

F32 = jnp.float32
BF16 = jnp.bfloat16
F8 = jnp.float8_e4m3fn
F8_TARGET = 240.0
TINY = 1e-30
EPS = 1e-6
MASK_VALUE = -1e30
LANES = 128
VMEM_CAP_BYTES = 60000 * 1024
DIFF_HALF_DIM = 64
DIFF_KEY_BLOCK = 256
DIFF_SKIP_MIN_CHUNKS = 8
DIFF_SKIP_MARGIN = 140.0
DIFF_PAD_ROWS = 16
DIL_CONFIGS = ((128, 1), (512, 4), (2048, 16))
DIL_UNROLL = 32
PEER_TOPK = 16
PEER_SUBTILE = 256
NT_DIMS = (((1,), (1,)), ((), ()))


def _params(semantics, vmem_bytes):
    return pltpu.CompilerParams(
        dimension_semantics=semantics,
        vmem_limit_bytes=int(min(max(vmem_bytes, 16 * 2**20), VMEM_CAP_BYTES)))


def _alibi_slopes(n):
    return 2.0 ** (-8.0 * np.arange(1, n + 1) / n)


def _ada_kernel(c_ref, w_ref, b_ref, o_ref):
    c = c_ref[...]
    a = (c / (1.0 + jnp.exp(-c))).astype(BF16)
    o_ref[...] = jnp.dot(a, w_ref[...].astype(BF16), preferred_element_type=F32) + b_ref[...]


def _ada(c_all, w_ada, b_ada):
    rows, d = c_all.shape
    n = w_ada.shape[1]
    tn = 512
    return pl.pallas_call(
        _ada_kernel,
        grid=(n // tn,),
        in_specs=[pl.BlockSpec((rows, d), lambda j: (0, 0)),
                  pl.BlockSpec((d, tn), lambda j: (0, j)),
                  pl.BlockSpec((1, tn), lambda j: (0, j))],
        out_specs=pl.BlockSpec((rows, tn), lambda j: (0, j)),
        out_shape=jax.ShapeDtypeStruct((rows, n), F32),
        compiler_params=_params(("arbitrary",), 2 * d * tn * 4 + d * tn * 2 + 4 * 2**20),
        name="ada",
    )(c_all, w_ada, b_ada.reshape(1, n))


def _norm_mod_kernel(x_ref, gain_ref, sc_ref, sh_ref, o_ref, *rest, transpose):
    x = x_ref[0]
    ms = jnp.mean(x * x, axis=-1, keepdims=True)
    y = x * lax.rsqrt(ms + EPS) * gain_ref[...]
    h = y * (1.0 + sc_ref[0]) + sh_ref[0]
    if transpose:
        q_ref, stat_ref = rest
        ht = h.T
        o_ref[...] = ht.astype(o_ref.dtype)
        amax = jnp.maximum(jnp.max(jnp.abs(ht), axis=0, keepdims=True), TINY)
        q_ref[...] = (ht * (F8_TARGET / amax)).astype(q_ref.dtype)
        stat_ref[0:1, :] = amax * (1.0 / F8_TARGET)
        stat_ref[1:2, :] = jnp.sqrt(jnp.sum(ht * ht, axis=0, keepdims=True))
    else:
        o_ref[...] = h.astype(o_ref.dtype)


def _norm_mod(x, gain, sc, sh, *, transpose, tm=256):
    b, t, d = x.shape
    m = b * t
    tm = min(tm, t)
    nt = t // tm
    if transpose:
        col = lambda bi, i: (0, bi * nt + i)
        out_spec = [pl.BlockSpec((d, tm), col), pl.BlockSpec((d, tm), col), pl.BlockSpec((2, tm), col)]
        out_shape = [jax.ShapeDtypeStruct((d, m), BF16), jax.ShapeDtypeStruct((d, m), F8),
                     jax.ShapeDtypeStruct((2, m), F32)]
    else:
        out_spec = pl.BlockSpec((tm, d), lambda bi, i: (bi * nt + i, 0))
        out_shape = jax.ShapeDtypeStruct((m, d), BF16)
    return pl.pallas_call(
        functools.partial(_norm_mod_kernel, transpose=transpose),
        grid=(b, nt),
        in_specs=[pl.BlockSpec((1, tm, d), lambda bi, i: (bi, i, 0)),
                  pl.BlockSpec((1, d), lambda bi, i: (0, 0)),
                  pl.BlockSpec((1, 1, d), lambda bi, i: (bi, 0, 0)),
                  pl.BlockSpec((1, 1, d), lambda bi, i: (bi, 0, 0))],
        out_specs=out_spec,
        out_shape=out_shape,
        compiler_params=_params(("parallel", "parallel"), 8 * tm * d * 4 + 4 * 2**20),
        name="norm_mod_t" if transpose else "norm_mod",
    )(x, gain.reshape(1, d), sc, sh)


def _inproj_kernel(a_ref, w_ref, cs_ref, o_ref):
    acc = jnp.dot(a_ref[...], w_ref[...], preferred_element_type=F32) * cs_ref[...]
    for c in range(o_ref.shape[0]):
        o_ref[c] = acc[:, c * LANES:(c + 1) * LANES].astype(o_ref.dtype)


def _inproj(h, w, colscale, *, tm=1024, tn=1024):
    m, k = h.shape
    n = w.shape[1]
    tm = min(tm, m)
    tn = max(c for c in (tn, tn // 2, tn // 4, LANES) if n % c == 0)
    assert m % tm == 0
    nb = tn // LANES
    return pl.pallas_call(
        _inproj_kernel,
        grid=(m // tm, n // tn),
        in_specs=[pl.BlockSpec((tm, k), lambda i, j: (i, 0)),
                  pl.BlockSpec((k, tn), lambda i, j: (0, j)),
                  pl.BlockSpec((1, tn), lambda i, j: (0, j))],
        out_specs=pl.BlockSpec((nb, tm, LANES), lambda i, j: (j, i, 0)),
        out_shape=jax.ShapeDtypeStruct((n // LANES, m, LANES), BF16),
        compiler_params=_params(("parallel", "arbitrary"),
                                2 * (tm * k + k * tn + tm * tn) * 2 + 2 * tm * tn * 4 + 2 * 2**20),
        name="inproj",
    )(h, w, colscale)


def _alibi_tables(nh, tq, tk):
    bf = lambda a: np.asarray(a, np.float64).astype(ml_dtypes.bfloat16).astype(np.float64)
    slope = _alibi_slopes(nh) * math.log2(math.e)
    hi = bf(slope)
    lo = bf(slope - hi)
    augq = np.zeros((nh, tq, LANES), np.float64)
    augk = np.zeros((nh, 2, tk, LANES), np.float64)
    qpos, kpos = np.arange(tq), np.arange(tk)
    for h in range(nh):
        qcols = [qpos // 16, qpos // 16, qpos % 16, qpos % 16,
                 16 * hi[h] + 0 * qpos, 16 * lo[h] + 0 * qpos, hi[h] + 0 * qpos, lo[h] + 0 * qpos]
        kcols = [-16 * hi[h] + 0 * kpos, -16 * lo[h] + 0 * kpos, -hi[h] + 0 * kpos, -lo[h] + 0 * kpos,
                 kpos // 16, kpos // 16, kpos % 16, kpos % 16]
        for c in range(8):
            for base in (0, DIFF_HALF_DIM):
                augq[h, :, base + c] = qcols[c]
                augk[h, 0, :, base + c] = kcols[c]
                augk[h, 1, :, base + c] = -kcols[c]
    nslope = np.broadcast_to(-(hi + lo)[:, None, None], (nh, 1, LANES))
    return (jnp.asarray(augq, BF16), jnp.asarray(augk, BF16), jnp.asarray(nslope, F32))


def _diff_attn_kernel(q_ref, qall_ref, k_ref, v_ref, augq_ref, augk_ref, ns_ref, sl_ref, lp_ref, gain_ref, o_ref,
                      vt_sc, st0, st1, p0, p1, al0, al1, m_sc, acc_sc, qmax_sm, kmax_sm, *, tile, lambda_init, skip):
    qi = pl.program_id(2)
    t = k_ref.shape[1]
    n = t // tile
    kb = min(tile, DIFF_KEY_BLOCK)
    groups = tile // LANES

    @pl.when(qi == 0)
    def _():
        ones_row = jnp.where(lax.broadcasted_iota(jnp.int32, (DIFF_PAD_ROWS, tile), 0) == 0, 1.0, 0.0)
        for c in range(n):
            vt_sc[c, :LANES, :] = v_ref[0, c * tile:(c + 1) * tile, :].astype(F32).T.astype(BF16)
            vt_sc[c, LANES:, :] = ones_row.astype(BF16)
            if skip:
                kf = k_ref[0, c * tile:(c + 1) * tile, :].astype(F32)
                kmax_sm[c] = jnp.max(jnp.sqrt(jnp.max(jnp.sum(kf * kf, axis=1, keepdims=True), axis=0, keepdims=True)))
                qf = qall_ref[0, c * tile:(c + 1) * tile, :].astype(F32)
                qmax_sm[c] = jnp.max(jnp.sqrt(jnp.max(jnp.sum(qf * qf, axis=1, keepdims=True), axis=0, keepdims=True)))

    q = q_ref[0]
    lane = lax.broadcasted_iota(jnp.int32, q.shape, 1)
    low = lane < DIFF_HALF_DIM
    aq = augq_ref[0]
    qxt = [jnp.where(low, q, aq).astype(F32).T.astype(BF16),
           jnp.where(low, aq, q).astype(F32).T.astype(BF16)]
    nslope = ns_ref[0][:1, :1]
    m_sc[...] = jnp.full(m_sc.shape, MASK_VALUE, F32)
    acc_sc[...] = jnp.zeros(acc_sc.shape, F32)
    st_sc, p_sc, al_sc = (st0, st1), (p0, p1), (al0, al1)

    if skip:
        bq = qmax_sm[qi]
        slope = sl_ref[pl.program_id(1)]
        floor_m = -(bq * kmax_sm[qi]) - DIFF_SKIP_MARGIN

        def dead(j):
            gap = ((jnp.abs(qi - j) - 1) * tile + 1).astype(F32)
            return bq * kmax_sm[j] - slope * gap < floor_m

        lo = jnp.int32(0)
        for j in range(n - 1):
            lo = lo + ((lo == j) & (j < qi) & dead(j)).astype(jnp.int32)
        hi = jnp.int32(n - 1)
        for j in range(n - 1, 0, -1):
            hi = hi - ((hi == j) & (j > qi) & dead(j)).astype(jnp.int32)
        odd = ((hi - lo + 1 - n) & 1) == 1
        lo, hi = jnp.where(odd & (lo > 0), lo - 1, lo), jnp.where(odd & (lo == 0), hi + 1, hi)
    else:
        lo, hi = 0, n - 1
    cnt = hi - lo + 1

    def chunk_at(pos):
        nxt = lo + pos - 1
        return jnp.where(pos == 0, qi, nxt + (nxt >= qi).astype(jnp.int32))

    def scores(j, side):
        rows = pl.ds(pl.multiple_of(j * tile, tile), tile)
        k, ak = k_ref[0, rows, :], augk_ref[0, side]
        return jnp.concatenate([jnp.dot(jnp.where(low, k, ak), qxt[0], preferred_element_type=F32),
                                jnp.dot(jnp.where(low, ak, k), qxt[1], preferred_element_type=F32)],
                               axis=1)

    def softmax(j, buf):
        offset = nslope * (jnp.abs(qi - j) * tile).astype(F32)
        for g in range(2 * groups):
            cols = slice(g * LANES, (g + 1) * LANES)
            m_old = m_sc[:, cols]
            mx = m_old
            for r in range(tile // kb):
                s = st_sc[buf][r * kb:(r + 1) * kb, cols]
                mx = jnp.maximum(mx, jnp.max(s, axis=0, keepdims=True) + offset)
            al_sc[buf][:, cols] = jnp.exp2(m_old - mx)
            m_sc[:, cols] = mx
            sub = mx - offset
            for r in range(tile // kb):
                rows = slice(r * kb, (r + 1) * kb)
                p_sc[buf][rows, cols] = jnp.exp2(st_sc[buf][rows, cols] - sub).astype(BF16)

    def accumulate(j, buf):
        acc_sc[...] = al_sc[buf][...] * acc_sc[...] + jnp.dot(vt_sc[j], p_sc[buf][...], preferred_element_type=F32)

    p_sc[1][...] = jnp.zeros(p_sc[1].shape, BF16)
    al_sc[1][...] = jnp.ones(al_sc[1].shape, F32)
    st_sc[0][...] = jnp.minimum(scores(qi, 0), scores(qi, 1))

    def step(pos, buf):
        nxt = chunk_at(pos + 1)
        st_sc[1 - buf][...] = scores(nxt, (nxt > qi).astype(jnp.int32))
        softmax(chunk_at(pos), buf)
        accumulate(chunk_at(jnp.maximum(pos - 1, 0)), 1 - buf)

    def body(i, carry):
        step(2 * i, 0)
        step(2 * i + 1, 1)
        return carry

    lax.fori_loop(0, (cnt - 1) // 2, body, 0)
    if (n - 1) % 2:
        step(cnt - 2, (n - 2) & 1)
    last = (n - 1) & 1
    softmax(chunk_at(cnt - 1), last)
    if n > 1:
        accumulate(chunk_at(jnp.maximum(cnt - 2, 0)), 1 - last)
    accumulate(chunk_at(cnt - 1), last)

    lp = lp_ref[...]
    lam = (jnp.exp(jnp.sum(lp[0:1] * lp[1:2], axis=1, keepdims=True))
           - jnp.exp(jnp.sum(lp[2:3] * lp[3:4], axis=1, keepdims=True)) + lambda_init)
    on = acc_sc[:LANES, :] / acc_sc[LANES:LANES + 1, :]
    o = on[:, :tile] - lam * on[:, tile:]
    o = o * lax.rsqrt(jnp.mean(o * o, axis=0, keepdims=True) + EPS) * (1.0 - lambda_init)
    o_ref[0] = (o.T * gain_ref[...]).astype(o_ref.dtype)


def _diff_attention(p, lam_params, subln_gain, *, b, t, nh, lambda_init, tile=512):
    tile = min(tile, t)
    assert t % tile == 0 and tile % DIFF_KEY_BLOCK == 0
    nq = t // tile
    augq, augk, nslopes = _alibi_tables(nh, tile, tile)
    skip = nq >= DIFF_SKIP_MIN_CHUNKS
    return pl.pallas_call(
        functools.partial(_diff_attn_kernel, tile=tile, lambda_init=lambda_init, skip=skip),
        grid=(b, nh, nq),
        in_specs=[pl.BlockSpec((1, tile, LANES), lambda bi, h, qi: (h, bi * nq + qi, 0)),
                  pl.BlockSpec((1, t, LANES), lambda bi, h, qi: (h, bi, 0)),
                  pl.BlockSpec((1, t, LANES), lambda bi, h, qi: (nh + h, bi, 0)),
                  pl.BlockSpec((1, t, LANES), lambda bi, h, qi: (2 * nh + h, bi, 0)),
                  pl.BlockSpec((1, tile, LANES), lambda bi, h, qi: (h, 0, 0)),
                  pl.BlockSpec((1, 2, tile, LANES), lambda bi, h, qi: (h, 0, 0, 0)),
                  pl.BlockSpec((1, 1, LANES), lambda bi, h, qi: (h, 0, 0)),
                  pl.BlockSpec(memory_space=pltpu.SMEM),
                  pl.BlockSpec(lam_params.shape, lambda bi, h, qi: (0, 0)),
                  pl.BlockSpec((1, LANES), lambda bi, h, qi: (0, 0))],
        out_specs=pl.BlockSpec((1, tile, LANES), lambda bi, h, qi: (bi, qi, h)),
        out_shape=jax.ShapeDtypeStruct((b, t, nh * LANES), BF16),
        scratch_shapes=[pltpu.VMEM((t // tile, LANES + DIFF_PAD_ROWS, tile), BF16),
                        pltpu.VMEM((tile, 2 * tile), F32), pltpu.VMEM((tile, 2 * tile), F32),
                        pltpu.VMEM((tile, 2 * tile), BF16), pltpu.VMEM((tile, 2 * tile), BF16),
                        pltpu.VMEM((1, 2 * tile), F32), pltpu.VMEM((1, 2 * tile), F32),
                        pltpu.VMEM((1, 2 * tile), F32),
                        pltpu.VMEM((LANES + DIFF_PAD_ROWS, 2 * tile), F32),
                        pltpu.SMEM((t // tile,), F32), pltpu.SMEM((t // tile,), F32)],
        compiler_params=_params(("parallel", "parallel", "arbitrary"),
                                7 * t * LANES * 2 + 2 * tile * tile * 20 + 4 * LANES * tile * 4 + 8 * 2**20),
        name="diff_attn",
    )(p, p, p, p, augq, augk, nslopes, -nslopes[:, 0, 0], lam_params, subln_gain.reshape(1, LANES))


def _dil_attn_kernel(q_ref, k_ref, v_ref, ns_ref, o_ref, qf, kf, vf, qg, kg, vg, qr, kr, vr, ob, lb, bt, *, blk):
    t = q_ref.shape[1]

    @pl.when(pl.program_id(1) == 0)
    def _():
        nslope = ns_ref[0][:, :1]
        for n, (window, g) in enumerate(DIL_CONFIGS):
            radius = window // (2 * g)
            wk = min(blk + 2 * radius, t // g)
            rel0 = (lax.broadcasted_iota(jnp.int32, (blk, wk), 1)
                    - lax.broadcasted_iota(jnp.int32, (blk, wk), 0))
            for idx, shift in enumerate((0, -radius, blk - wk)):
                ar = jnp.abs(rel0 + shift)
                bt[n, idx, :, :wk] = jnp.where(ar <= radius, (nslope * float(g)) * ar.astype(F32), MASK_VALUE)

    srcs = {1: (q_ref.at[0], k_ref.at[0], v_ref.at[0])}
    if any(g > 1 for _, g in DIL_CONFIGS):
        qf[...] = q_ref[0].astype(F32)
        kf[...] = k_ref[0].astype(F32)
        vf[...] = v_ref[0].astype(F32)
    f32_src = {1: (qf, kf, vf)}

    for n, (window, g) in enumerate(DIL_CONFIGS):
        radius = window // (2 * g)
        ln = t // g
        wk = min(blk + 2 * radius, ln)
        nblk = ln // blk
        if g not in srcs:
            gp = max(d for d in f32_src if g % d == 0)
            step, lp = g // gp, t // gp
            keep = any(d > g and d % g == 0 for _, d in DIL_CONFIGS)
            assert not (keep and gp > 1), "one f32 staging level"
            dst16 = (qr, kr, vr)
            dst32 = (qg, kg, vg) if keep else (None, None, None)
            for a_src, a16, a32 in zip(f32_src[gp], dst16, dst32):
                for r in range(gp):
                    for s in range(step):
                        rho = r + gp * s
                        rows = a_src[pl.ds(r * lp + s, ln, stride=step), :]
                        a16[rho * ln:(rho + 1) * ln, :] = rows.astype(BF16)
                        if keep:
                            a32[rho * ln:(rho + 1) * ln, :] = rows
            srcs[g] = dst16
            if keep:
                f32_src[g] = dst32
        qs, ks, vs = srcs[g]
        def body(i, carry, n=n, g=g, radius=radius, ln=ln, wk=wk, nblk=nblk, qs=qs, ks=ks, vs=vs):
            rho = i // nblk
            bi = i % nblk
            l0 = bi * blk
            ws = jnp.clip(l0 - radius, 0, ln - wk)
            base = rho * ln
            qb = qs[pl.ds(pl.multiple_of(base + l0, blk), blk), :]
            kw = ks[pl.ds(pl.multiple_of(base + ws, radius), wk), :]
            vw = vs[pl.ds(pl.multiple_of(base + ws, radius), wk), :]
            place = jnp.where(bi == 0, 0, jnp.where(bi == nblk - 1, 2, 1))
            s = lax.dot_general(qb, kw, NT_DIMS, preferred_element_type=F32) + bt[n, place, :, :wk]
            mx = jnp.max(s, axis=1, keepdims=True)
            e = jnp.exp2(s - mx)
            den = jnp.sum(e, axis=1, keepdims=True)
            o = jnp.dot(e.astype(BF16), vw, preferred_element_type=F32) / den
            lse = mx + jnp.log2(den)
            rows = pl.ds(l0 * g + rho, blk, stride=g) if g > 1 else pl.ds(pl.multiple_of(l0, blk), blk)
            ob[n, rows, :] = o
            lb[n, rows, :] = jnp.broadcast_to(lse, (blk, LANES))
            return carry

        lax.fori_loop(0, g * nblk, body, 0, unroll=min(DIL_UNROLL, g * nblk))

    chunk = min(512, t)

    def merge(i, carry):
        rows = pl.ds(pl.multiple_of(i * chunk, chunk), chunk)
        l0, l1, l2 = lb[0, rows, :], lb[1, rows, :], lb[2, rows, :]
        mx = jnp.maximum(jnp.maximum(l0, l1), l2)
        w0, w1, w2 = jnp.exp2(l0 - mx), jnp.exp2(l1 - mx), jnp.exp2(l2 - mx)
        o = (w0 * ob[0, rows, :] + w1 * ob[1, rows, :] + w2 * ob[2, rows, :]) / (w0 + w1 + w2)
        o_ref[0, rows, :] = o.astype(o_ref.dtype)
        return carry

    lax.fori_loop(0, t // chunk, merge, 0)


def _dil_attention(p, nslopes, *, b, t, nh, blk=128):
    nbr = len(DIL_CONFIGS)
    assert all(t % (g * blk) == 0 and window == 2 * g * (blk // 2) for window, g in DIL_CONFIGS)
    return pl.pallas_call(
        functools.partial(_dil_attn_kernel, blk=blk),
        grid=(nh, b),
        in_specs=[pl.BlockSpec((1, t, LANES), lambda h, bi: (3 * nh + h, bi, 0)),
                  pl.BlockSpec((1, t, LANES), lambda h, bi: (4 * nh + h, bi, 0)),
                  pl.BlockSpec((1, t, LANES), lambda h, bi: (5 * nh + h, bi, 0)),
                  pl.BlockSpec((1, 1, LANES), lambda h, bi: (h, 0, 0))],
        out_specs=pl.BlockSpec((1, t, LANES), lambda h, bi: (bi, 0, h)),
        out_shape=jax.ShapeDtypeStruct((b, t, nh * LANES), BF16),
        scratch_shapes=[pltpu.VMEM((t, LANES), F32)] * 6 + [pltpu.VMEM((t, LANES), BF16)] * 3
                       + [pltpu.VMEM((nbr, t, LANES), F32)] * 2 + [pltpu.VMEM((nbr, 3, blk, 2 * blk), F32)],
        compiler_params=_params(("parallel", "arbitrary"),
                                8 * t * LANES * 2 + 3 * t * LANES * 10 + 2 * nbr * t * LANES * 4
                                + DIL_UNROLL * blk * 2 * blk * 16 + 8 * 2**20),
        name="dil_attn",
    )(p, p, p, nslopes)


def _outproj_kernel(ma_ref, md_ref, wt_ref, wb_ref, x_ref, g_ref, o_ref):
    acc = (jnp.dot(ma_ref[...], wt_ref[...], preferred_element_type=F32)
           + jnp.dot(md_ref[...], wb_ref[...], preferred_element_type=F32))
    o_ref[...] = x_ref[...] + g_ref[0] * acc


def _outproj(ma, md, w_out, x2d, gate, *, t, tm=1024, tn=512):
    m, kh = ma.shape
    n = w_out.shape[1]
    tm, tn = min(tm, t), min(tn, n)
    return pl.pallas_call(
        _outproj_kernel,
        grid=(m // tm, n // tn),
        in_specs=[pl.BlockSpec((tm, kh), lambda i, j: (i, 0)),
                  pl.BlockSpec((tm, kh), lambda i, j: (i, 0)),
                  pl.BlockSpec((kh, tn), lambda i, j: (0, j)),
                  pl.BlockSpec((kh, tn), lambda i, j: (1, j)),
                  pl.BlockSpec((tm, tn), lambda i, j: (i, j)),
                  pl.BlockSpec((1, 1, tn), lambda i, j: ((i * tm) // t, 0, j))],
        out_specs=pl.BlockSpec((tm, tn), lambda i, j: (i, j)),
        out_shape=jax.ShapeDtypeStruct((m, n), F32),
        compiler_params=_params(("parallel", "arbitrary"),
                                2 * (2 * tm * kh + 2 * kh * tn) * 2 + 5 * tm * tn * 4 + 2 * 2**20),
        name="outproj",
    )(ma, md, w_out, w_out, x2d, gate)


def _take_top_rows(s, rows_ref, count):
    for r in range(count):
        mx = jnp.max(s, axis=0, keepdims=True)
        rows_ref[r:r + 1, :] = mx
        s = jnp.where(s >= mx, -jnp.inf, s)


def _router_kernel(h_ref, wq_ref, keys_ref, thr_ref, s1_ref, a_ref, b_ref, ra_ref, rb_ref, rc_ref):
    k = PEER_TOPK
    sub = keys_ref.shape[2]
    qt = jnp.dot(wq_ref[...], h_ref[...], preferred_element_type=F32)
    s0 = jnp.dot(keys_ref[0, 0], qt[:sub].astype(BF16), preferred_element_type=F32)
    s1 = jnp.dot(keys_ref[0, 1], qt[sub:].astype(BF16), preferred_element_type=F32)
    _take_top_rows(s0, ra_ref, k)
    _take_top_rows(s1, rb_ref, k)
    ra, rb = ra_ref[...], rb_ref[...]
    row8 = lax.broadcasted_iota(jnp.int32, (8, ra.shape[1]), 0)
    pieces = [ra[0:1] + rb, ra[1:2] + rb[0:8]]
    for r in range(2, 8):
        pieces.append(jnp.where(row8 < k // (r + 1), ra[r:r + 1] + rb[0:8], -jnp.inf))
    pieces.append(ra[8:16] + rb[0:1])
    cand = jnp.concatenate(pieces, axis=0)
    _take_top_rows(cand, rc_ref, k)
    tau = rc_ref[k - 1:k, :]
    top = ra[0:1] + rb[0:1]
    z = jnp.sum(jnp.where(cand >= tau, jnp.exp(cand - top), 0.0), axis=0, keepdims=True)
    thr = jnp.full(s0.shape, jnp.inf, F32)
    for r in range(k):
        row_thr = jnp.min(jnp.where(ra[r:r + 1] + rb >= tau, rb, jnp.inf), axis=0, keepdims=True)
        thr = jnp.where(s0 == ra[r:r + 1], row_thr, thr)
    thr_ref[0] = thr
    s1_ref[0] = s1
    a_ref[0] = jnp.exp(s0 - ra[0:1]) / z
    b_ref[0] = jnp.exp(s1 - rb[0:1])


def _router(ht, wq_t, keys, *, tm=1024):
    d, m = ht.shape
    heads, _, n_keys, sub = keys.shape
    tm = min(tm, m)
    big = jax.ShapeDtypeStruct((heads, n_keys, m), F32)
    big_spec = pl.BlockSpec((1, n_keys, tm), lambda i, h: (h, 0, i))
    return pl.pallas_call(
        _router_kernel,
        grid=(m // tm, heads),
        in_specs=[pl.BlockSpec((d, tm), lambda i, h: (0, i)),
                  pl.BlockSpec((2 * sub, d), lambda i, h: (h, 0)),
                  pl.BlockSpec((1, 2, n_keys, sub), lambda i, h: (h, 0, 0, 0))],
        out_specs=[big_spec, big_spec, big_spec, big_spec],
        out_shape=[big, big, big, big],
        scratch_shapes=[pltpu.VMEM((PEER_TOPK, tm), F32)] * 3,
        compiler_params=_params(("parallel", "arbitrary"),
                                2 * d * tm * 2 + 2 * 2 * sub * d * 2 + 8 * n_keys * tm * 4 + 16 * 2**20),
        name="peer_router",
    )(ht, wq_t, keys)


def _expert_table_kernel(x_ref, q_ref, inv_ref, nrm_ref, *, transpose):
    x = x_ref[...]
    amax = jnp.maximum(jnp.max(jnp.abs(x), axis=1, keepdims=True), TINY)
    xs = x * (F8_TARGET / amax)
    q_ref[...] = (xs.T if transpose else xs).astype(q_ref.dtype)
    inv_ref[...] = jnp.broadcast_to(amax * (1.0 / F8_TARGET), inv_ref.shape)
    nrm_ref[...] = jnp.broadcast_to(jnp.sqrt(jnp.sum(x * x, axis=1, keepdims=True)), nrm_ref.shape)


def _expert_table(x, *, transpose, te=256):
    n_exp, d = x.shape
    side = jax.ShapeDtypeStruct((n_exp, LANES), F32)
    side_spec = pl.BlockSpec((te, LANES), lambda e: (e, 0))
    return pl.pallas_call(
        functools.partial(_expert_table_kernel, transpose=transpose),
        grid=(n_exp // te,),
        in_specs=[pl.BlockSpec((te, d), lambda e: (e, 0))],
        out_specs=[pl.BlockSpec((d, te), lambda e: (0, e)) if transpose else pl.BlockSpec((te, d), lambda e: (e, 0)),
                   side_spec, side_spec],
        out_shape=[jax.ShapeDtypeStruct((d, n_exp) if transpose else (n_exp, d), F8), side, side],
        compiler_params=_params(("parallel",), 6 * te * d * 4 + 4 * 2**20),
        name="expert_table_t" if transpose else "expert_table",
    )(x)


def _peer_dense_kernel(h_ref, u_ref, vt_ref, ui_ref, vi_ref, sc_ref, thr_ref, s1_ref, a_ref, b_ref, o_ref,
                       g_sc, acc_sc):
    e = pl.program_id(1)
    heads, n_keys, tm = s1_ref.shape
    te = u_ref.shape[0]
    sub = min(te, PEER_SUBTILE)

    @pl.when(e == 0)
    def _():
        acc_sc[...] = jnp.zeros(acc_sc.shape, F32)

    for ii in range(te // n_keys):
        i = e * (te // n_keys) + ii
        gates = jnp.zeros((n_keys, tm), F32)
        for h in range(heads):
            gates = gates + (jnp.where(s1_ref[h] >= thr_ref[h, pl.ds(i, 1), :], b_ref[h], 0.0)
                             * a_ref[h, pl.ds(i, 1), :])
        g_sc[ii * n_keys:(ii + 1) * n_keys, :] = gates
    nsub = te // sub
    act_scale, w_scale = sc_ref[0:1, :], sc_ref[1:2, :]
    across = lambda col: jnp.concatenate([col] * (tm // LANES), axis=1)
    acts = [jnp.dot(u_ref[t * sub:(t + 1) * sub, :], h_ref[...], preferred_element_type=F32)
            * across(ui_ref[t * sub:(t + 1) * sub, :]) * act_scale for t in range(nsub)]
    for t in range(nsub):
        gel = 0.5 * acts[t] * (1.0 + lax.erf(acts[t] * (1.0 / math.sqrt(2.0))))
        w = (g_sc[t * sub:(t + 1) * sub, :] * gel * across(vi_ref[t * sub:(t + 1) * sub, :])
             * w_scale).astype(vt_ref.dtype)
        acc_sc[...] += jnp.dot(vt_ref[:, t * sub:(t + 1) * sub], w, preferred_element_type=F32)

    @pl.when(e == pl.num_programs(1) - 1)
    def _():
        o_ref[...] = acc_sc[...].astype(o_ref.dtype)


def _peer_dense(ht, u, vt, u_inv, v_inv, scales, thr, s1, a, bf, *, tm=512, te=1024):
    d, m = ht.shape
    n_exp = u.shape[0]
    heads, n_keys, _ = s1.shape
    tm = min(tm, m)
    assert m % tm == 0 and n_exp % te == 0 and te % PEER_SUBTILE == 0 and PEER_SUBTILE % n_keys == 0
    once = pl.Buffered(1)
    tok = lambda i, e: (0, 0, i)
    big_spec = pl.BlockSpec((heads, n_keys, tm), tok, pipeline_mode=once)
    return pl.pallas_call(
        _peer_dense_kernel,
        grid=(m // tm, n_exp // te),
        in_specs=[pl.BlockSpec((d, tm), lambda i, e: (0, i), pipeline_mode=once),
                  pl.BlockSpec((te, d), lambda i, e: (e, 0)),
                  pl.BlockSpec((d, te), lambda i, e: (0, e)),
                  pl.BlockSpec((te, LANES), lambda i, e: (e, 0)),
                  pl.BlockSpec((te, LANES), lambda i, e: (e, 0)),
                  pl.BlockSpec((2, tm), lambda i, e: (0, i), pipeline_mode=once),
                  big_spec, big_spec, big_spec, big_spec],
        out_specs=pl.BlockSpec((d, tm), lambda i, e: (0, i)),
        out_shape=jax.ShapeDtypeStruct((d, m), BF16),
        scratch_shapes=[pltpu.VMEM((te, tm), F32), pltpu.VMEM((d, tm), F32)],
        compiler_params=_params(("parallel", "arbitrary"),
                                d * tm * 2 + 4 * te * d * 2 + 2 * d * tm * 4 + 4 * heads * n_keys * tm * 4
                                + 6 * te * tm * 4 + d * tm * 4),
        name="peer_dense",
    )(ht, u, vt, u_inv, v_inv, scales, thr, s1, a, bf)


def _final_kernel(x_ref, pt_ref, ps_ref, g_ref, gain_ref, o_ref):
    x = x_ref[...] + g_ref[0] * (pt_ref[...].astype(F32) * ps_ref[...]).T
    ms = jnp.mean(x * x, axis=-1, keepdims=True)
    o_ref[...] = x * lax.rsqrt(ms + EPS) * gain_ref[...]


def _final(x1, pt, pscale, gate, gain, *, t, tm=256):
    m, d = x1.shape
    tm = min(tm, t)
    return pl.pallas_call(
        _final_kernel,
        grid=(m // tm,),
        in_specs=[pl.BlockSpec((tm, d), lambda i: (i, 0)),
                  pl.BlockSpec((d, tm), lambda i: (0, i)),
                  pl.BlockSpec((1, tm), lambda i: (0, i)),
                  pl.BlockSpec((1, 1, d), lambda i: ((i * tm) // t, 0, 0)),
                  pl.BlockSpec((1, d), lambda i: (0, 0))],
        out_specs=pl.BlockSpec((tm, d), lambda i: (i, 0)),
        out_shape=jax.ShapeDtypeStruct((m, d), F32),
        compiler_params=_params(("parallel",), 8 * tm * d * 4 + 4 * 2**20),
        name="final_norm",
    )(x1, pt, pscale, gate, gain.reshape(1, d))


def _trunk(x, ada, w, depth_index=0):
    b, t, d = x.shape
    nh = d // (2 * LANES)
    lambda_init = 0.8 - 0.6 * math.exp(-0.3 * depth_index)
    sh1, sc1, g1, sh2, sc2, g2 = [a.reshape(b, 1, d) for a in jnp.split(ada, 6, axis=-1)]

    h = _norm_mod(x, w["norm1_gain"], sc1, sh1, transpose=False)
    p = _inproj(h, w["w_in"], w["colscale"])
    ma = _diff_attention(p, w["diff_lambda"], w["diff_subln_gain"], b=b, t=t, nh=nh, lambda_init=lambda_init)
    md = _dil_attention(p, w["dil_nslopes"], b=b, t=t, nh=nh)
    x1 = _outproj(ma.reshape(b * t, nh * LANES), md.reshape(b * t, nh * LANES), w["w_out"],
                  x.reshape(b * t, d), g1, t=t)

    ht, ht8, stat = _norm_mod(x1.reshape(b, t, d), w["norm2_gain"], sc2, sh2, transpose=True)
    thr, s1, a, bf = _router(ht, w["wq_t"], w["sub_keys"])
    w_scale = F8_TARGET / jnp.maximum(stat[1:2] * w["uv_bound"], TINY)
    pt = _peer_dense(ht8, w["peer_u"], w["peer_vt"], w["u_inv"], w["v_inv"],
                     jnp.concatenate([stat[0:1], w_scale], axis=0), thr, s1, a, bf)
    y = _final(x1, pt, 1.0 / w_scale, g2, w["final_gain"], t=t)
    return y.reshape(b, t, d)


def kernel(x_prompt, x_sample, c_prompt, c_sample, norm1_gain, norm2_gain, w_ada, b_ada, w_in, diff_lambda,
           diff_subln_gain, w_out, peer_w_query, peer_sub_keys, peer_u, peer_v, final_gain):
    d = x_prompt.shape[-1]
    nh = d // (2 * LANES)
    bp, bs = c_prompt.shape[0], c_sample.shape[0]
    assert norm1_gain.shape[0] == 1, "single-layer trunk"
    assert peer_sub_keys.shape[3] == LANES and peer_sub_keys.shape[4] == LANES

    rows = -(-(bp + bs) // 8) * 8
    c_all = jnp.concatenate([c_prompt, c_sample, jnp.zeros((rows - bp - bs, d), F32)], axis=0)
    ada = _ada(c_all, w_ada[0], b_ada[0])

    colscale = np.ones((1, w_in.shape[-1]), np.float32)
    colscale[0, :nh * LANES] = DIFF_HALF_DIM ** -0.5 * math.log2(math.e)
    colscale[0, 3 * nh * LANES:4 * nh * LANES] = LANES ** -0.5 * math.log2(math.e)
    lane_bcast = lambda v: jnp.asarray(np.broadcast_to(np.asarray(v, np.float32)[:, None, None], (nh, 1, LANES)))
    w = dict(
        norm1_gain=norm1_gain[0], norm2_gain=norm2_gain[0], final_gain=final_gain,
        w_in=w_in[0].astype(BF16), colscale=jnp.asarray(colscale),
        dil_nslopes=lane_bcast(-_alibi_slopes(nh) * math.log2(math.e)),
        diff_lambda=diff_lambda[0], diff_subln_gain=diff_subln_gain[0],
        w_out=w_out[0].astype(BF16),
        wq_t=peer_w_query[0].T.astype(BF16), sub_keys=peer_sub_keys[0].astype(BF16),
    )
    peer_u8, u_inv, u_nrm = _expert_table(peer_u[0], transpose=False)
    peer_vt8, v_inv, _ = _expert_table(peer_v[0], transpose=True)
    w.update(peer_u=peer_u8, peer_vt=peer_vt8, u_inv=u_inv, v_inv=v_inv,
             uv_bound=jnp.max(u_nrm[:, 0]) * jnp.max(v_inv[:, 0]))
    y_prompt = _trunk(x_prompt, ada[:bp], w)
    y_sample = _trunk(x_sample, ada[bp:bp + bs], w)
    return (y_prompt, y_sample)
```

```python
import functools
import math

import ml_dtypes
import numpy as np
import jax
import jax.numpy as jnp
from jax import lax
from jax.experimental import pallas as pl
from jax.experimental.pallas import tpu as pltpu

F32 = jnp.float32
BF16 = jnp.bfloat16
F8 = jnp.float8_e4m3fn
F8_TARGET = 240.0
TINY = 1e-30
EPS = 1e-6
MASK_VALUE = -1e30
LANES = 128
VMEM_CAP_BYTES = 60000 * 1024
DIFF_HALF_DIM = 64
DIFF_KEY_BLOCK = 256
DIFF_SKIP_MIN_CHUNKS = 8
DIFF_SKIP_MARGIN = 140.0
DIFF_PAD_ROWS = 16
DIL_CONFIGS = ((128, 1), (512, 4), (2048, 16))
DIL_UNROLL = 32
PEER_TOPK = 16
PEER_SUBTILE = 256
NT_DIMS = (((1,), (1,)), ((), ()))


def _params(semantics, vmem_bytes):
    return pltpu.CompilerParams(
        dimension_semantics=semantics,
        vmem_limit_bytes=int(min(max(vmem_bytes, 16 * 2**20), VMEM_CAP_BYTES)))


def _alibi_slopes(n):
    return 2.0 ** (-8.0 * np.arange(1, n + 1) / n)


def _ada_kernel(c_ref, w_ref, b_ref, o_ref):
    c = c_ref[...]
    a = (c / (1.0 + jnp.exp(-c))).astype(BF16)
    o_ref[...] = jnp.dot(a, w_ref[...].astype(BF16), preferred_element_type=F32) + b_ref[...]


def _ada(c_all, w_ada, b_ada):
    rows, d = c_all.shape
    n = w_ada.shape[1]
    tn = 512
    return pl.pallas_call(
        _ada_kernel,
        grid=(n // tn,),
        in_specs=[pl.BlockSpec((rows, d), lambda j: (0, 0)),
                  pl.BlockSpec((d, tn), lambda j: (0, j)),
                  pl.BlockSpec((1, tn), lambda j: (0, j))],
        out_specs=pl.BlockSpec((rows, tn), lambda j: (0, j)),
        out_shape=jax.ShapeDtypeStruct((rows, n), F32),
        compiler_params=_params(("arbitrary",), 2 * d * tn * 4 + d * tn * 2 + 4 * 2**20),
        name="ada",
    )(c_all, w_ada, b_ada.reshape(1, n))


def _norm_mod_kernel(x_ref, gain_ref, sc_ref, sh_ref, o_ref, *rest, transpose):
    x = x_ref[0]
    ms = jnp.mean(x * x, axis=-1, keepdims=True)
    y = x * lax.rsqrt(ms + EPS) * gain_ref[...]
    h = y * (1.0 + sc_ref[0]) + sh_ref[0]
    if transpose:
        q_ref, stat_ref = rest
        ht = h.T
        o_ref[...] = ht.astype(o_ref.dtype)
        amax = jnp.maximum(jnp.max(jnp.abs(ht), axis=0, keepdims=True), TINY)
        q_ref[...] = (ht * (F8_TARGET / amax)).astype(q_ref.dtype)
        stat_ref[0:1, :] = amax * (1.0 / F8_TARGET)
        stat_ref[1:2, :] = jnp.sqrt(jnp.sum(ht * ht, axis=0, keepdims=True))
    else:
        o_ref[...] = h.astype(o_ref.dtype)


def _norm_mod(x, gain, sc, sh, *, transpose, tm=256):
    b, t, d = x.shape
    m = b * t
    tm = min(tm, t)
    nt = t // tm
    if transpose:
        col = lambda bi, i: (0, bi * nt + i)
        out_spec = [pl.BlockSpec((d, tm), col), pl.BlockSpec((d, tm), col), pl.BlockSpec((2, tm), col)]
        out_shape = [jax.ShapeDtypeStruct((d, m), BF16), jax.ShapeDtypeStruct((d, m), F8),
                     jax.ShapeDtypeStruct((2, m), F32)]
    else:
        out_spec = pl.BlockSpec((tm, d), lambda bi, i: (bi * nt + i, 0))
        out_shape = jax.ShapeDtypeStruct((m, d), BF16)
    return pl.pallas_call(
        functools.partial(_norm_mod_kernel, transpose=transpose),
        grid=(b, nt),
        in_specs=[pl.BlockSpec((1, tm, d), lambda bi, i: (bi, i, 0)),
                  pl.BlockSpec((1, d), lambda bi, i: (0, 0)),
                  pl.BlockSpec((1, 1, d), lambda bi, i: (bi, 0, 0)),
                  pl.BlockSpec((1, 1, d), lambda bi, i: (bi, 0, 0))],
        out_specs=out_spec,
        out_shape=out_shape,
        compiler_params=_params(("parallel", "parallel"), 8 * tm * d * 4 + 4 * 2**20),
        name="norm_mod_t" if transpose else "norm_mod",
    )(x, gain.reshape(1, d), sc, sh)


def _inproj_kernel(a_ref, w_ref, cs_ref, o_ref):
    acc = jnp.dot(a_ref[...], w_ref[...], preferred_element_type=F32) * cs_ref[...]
    for c in range(o_ref.shape[0]):
        o_ref[c] = acc[:, c * LANES:(c + 1) * LANES].astype(o_ref.dtype)


def _inproj(h, w, colscale, *, tm=1024, tn=1024):
    m, k = h.shape
    n = w.shape[1]
    tm = min(tm, m)
    tn = max(c for c in (tn, tn // 2, tn // 4, LANES) if n % c == 0)
    assert m % tm == 0
    nb = tn // LANES
    return pl.pallas_call(
        _inproj_kernel,
        grid=(m // tm, n // tn),
        in_specs=[pl.BlockSpec((tm, k), lambda i, j: (i, 0)),
                  pl.BlockSpec((k, tn), lambda i, j: (0, j)),
                  pl.BlockSpec((1, tn), lambda i, j: (0, j))],
        out_specs=pl.BlockSpec((nb, tm, LANES), lambda i, j: (j, i, 0)),
        out_shape=jax.ShapeDtypeStruct((n // LANES, m, LANES), BF16),
        compiler_params=_params(("parallel", "arbitrary"),
                                2 * (tm * k + k * tn + tm * tn) * 2 + 2 * tm * tn * 4 + 2 * 2**20),
        name="inproj",
    )(h, w, colscale)


def _alibi_tables(nh, tq, tk):
    bf = lambda a: np.asarray(a, np.float64).astype(ml_dtypes.bfloat16).astype(np.float64)
    slope = _alibi_slopes(nh) * math.log2(math.e)
    hi = bf(slope)
    lo = bf(slope - hi)
    augq = np.zeros((nh, tq, LANES), np.float64)
    augk = np.zeros((nh, 2, tk, LANES), np.float64)
    qpos, kpos = np.arange(tq), np.arange(tk)
    for h in range(nh):
        qcols = [qpos // 16, qpos // 16, qpos % 16, qpos % 16,
                 16 * hi[h] + 0 * qpos, 16 * lo[h] + 0 * qpos, hi[h] + 0 * qpos, lo[h] + 0 * qpos]
        kcols = [-16 * hi[h] + 0 * kpos, -16 * lo[h] + 0 * kpos, -hi[h] + 0 * kpos, -lo[h] + 0 * kpos,
                 kpos // 16, kpos // 16, kpos % 16, kpos % 16]
        for c in range(8):
            for base in (0, DIFF_HALF_DIM):
                augq[h, :, base + c] = qcols[c]
                augk[h, 0, :, base + c] = kcols[c]
                augk[h, 1, :, base + c] = -kcols[c]
    nslope = np.broadcast_to(-(hi + lo)[:, None, None], (nh, 1, LANES))
    return (jnp.asarray(augq, BF16), jnp.asarray(augk, BF16), jnp.asarray(nslope, F32))


def _diff_attn_kernel(q_ref, qall_ref, k_ref, v_ref, augq_ref, augk_ref, ns_ref, sl_ref, lp_ref, gain_ref, o_ref,
                      vt_sc, st0, st1, p0, p1, al0, al1, m_sc, acc_sc, qmax_sm, kmax_sm, *, tile, lambda_init, skip):
    qi = pl.program_id(2)
    t = k_ref.shape[1]
    n = t // tile
    kb = min(tile, DIFF_KEY_BLOCK)
    groups = tile // LANES

    @pl.when(qi == 0)
    def _():
        ones_row = jnp.where(lax.broadcasted_iota(jnp.int32, (DIFF_PAD_ROWS, tile), 0) == 0, 1.0, 0.0)
        for c in range(n):
            vt_sc[c, :LANES, :] = v_ref[0, c * tile:(c + 1) * tile, :].astype(F32).T.astype(BF16)
            vt_sc[c, LANES:, :] = ones_row.astype(BF16)
            if skip:
                kf = k_ref[0, c * tile:(c + 1) * tile, :].astype(F32)
                kmax_sm[c] = jnp.max(jnp.sqrt(jnp.max(jnp.sum(kf * kf, axis=1, keepdims=True), axis=0, keepdims=True)))
                qf = qall_ref[0, c * tile:(c + 1) * tile, :].astype(F32)
                qmax_sm[c] = jnp.max(jnp.sqrt(jnp.max(jnp.sum(qf * qf, axis=1, keepdims=True), axis=0, keepdims=True)))

    q = q_ref[0]
    lane = lax.broadcasted_iota(jnp.int32, q.shape, 1)
    low = lane < DIFF_HALF_DIM
    aq = augq_ref[0]
    qxt = [jnp.where(low, q, aq).astype(F32).T.astype(BF16),
           jnp.where(low, aq, q).astype(F32).T.astype(BF16)]
    nslope = ns_ref[0][:1, :1]
    m_sc[...] = jnp.full(m_sc.shape, MASK_VALUE, F32)
    acc_sc[...] = jnp.zeros(acc_sc.shape, F32)
    st_sc, p_sc, al_sc = (st0, st1), (p0, p1), (al0, al1)

    if skip:
        bq = qmax_sm[qi]
        slope = sl_ref[pl.program_id(1)]
        floor_m = -(bq * kmax_sm[qi]) - DIFF_SKIP_MARGIN

        def dead(j):
            gap = ((jnp.abs(qi - j) - 1) * tile + 1).astype(F32)
            return bq * kmax_sm[j] - slope * gap < floor_m

        lo = jnp.int32(0)
        for j in range(n - 1):
            lo = lo + ((lo == j) & (j < qi) & dead(j)).astype(jnp.int32)
        hi = jnp.int32(n - 1)
        for j in range(n - 1, 0, -1):
            hi = hi - ((hi == j) & (j > qi) & dead(j)).astype(jnp.int32)
        odd = ((hi - lo + 1 - n) & 1) == 1
        lo, hi = jnp.where(odd & (lo > 0), lo - 1, lo), jnp.where(odd & (lo == 0), hi + 1, hi)
    else:
        lo, hi = 0, n - 1
    cnt = hi - lo + 1

    def chunk_at(pos):
        nxt = lo + pos - 1
        return jnp.where(pos == 0, qi, nxt + (nxt >= qi).astype(jnp.int32))

    def scores(j, side):
        rows = pl.ds(pl.multiple_of(j * tile, tile), tile)
        k, ak = k_ref[0, rows, :], augk_ref[0, side]
        return jnp.concatenate([jnp.dot(jnp.where(low, k, ak), qxt[0], preferred_element_type=F32),
                                jnp.dot(jnp.where(low, ak, k), qxt[1], preferred_element_type=F32)],
                               axis=1)

    def softmax(j, buf):
        offset = nslope * (jnp.abs(qi - j) * tile).astype(F32)
        for g in range(2 * groups):
            cols = slice(g * LANES, (g + 1) * LANES)
            m_old = m_sc[:, cols]
            mx = m_old
            for r in range(tile // kb):
                s = st_sc[buf][r * kb:(r + 1) * kb, cols]
                mx = jnp.maximum(mx, jnp.max(s, axis=0, keepdims=True) + offset)
            al_sc[buf][:, cols] = jnp.exp2(m_old - mx)
            m_sc[:, cols] = mx
            sub = mx - offset
            for r in range(tile // kb):
                rows = slice(r * kb, (r + 1) * kb)
                p_sc[buf][rows, cols] = jnp.exp2(st_sc[buf][rows, cols] - sub).astype(BF16)

    def accumulate(j, buf):
        acc_sc[...] = al_sc[buf][...] * acc_sc[...] + jnp.dot(vt_sc[j], p_sc[buf][...], preferred_element_type=F32)

    p_sc[1][...] = jnp.zeros(p_sc[1].shape, BF16)
    al_sc[1][...] = jnp.ones(al_sc[1].shape, F32)
    st_sc[0][...] = jnp.minimum(scores(qi, 0), scores(qi, 1))

    def step(pos, buf):
        nxt = chunk_at(pos + 1)
        st_sc[1 - buf][...] = scores(nxt, (nxt > qi).astype(jnp.int32))
        softmax(chunk_at(pos), buf)
        accumulate(chunk_at(jnp.maximum(pos - 1, 0)), 1 - buf)

    def body(i, carry):
        step(2 * i, 0)
        step(2 * i + 1, 1)
        return carry

    lax.fori_loop(0, (cnt - 1) // 2, body, 0)
    if (n - 1) % 2:
        step(cnt - 2, (n - 2) & 1)
    last = (n - 1) & 1
    softmax(chunk_at(cnt - 1), last)
    if n > 1:
        accumulate(chunk_at(jnp.maximum(cnt - 2, 0)), 1 - last)
    accumulate(chunk_at(cnt - 1), last)

    lp = lp_ref[...]
    lam = (jnp.exp(jnp.sum(lp[0:1] * lp[1:2], axis=1, keepdims=True))
           - jnp.exp(jnp.sum(lp[2:3] * lp[3:4], axis=1, keepdims=True)) + lambda_init)
    on = acc_sc[:LANES, :] / acc_sc[LANES:LANES + 1, :]
    o = on[:, :tile] - lam * on[:, tile:]
    o = o * lax.rsqrt(jnp.mean(o * o, axis=0, keepdims=True) + EPS) * (1.0 - lambda_init)
    o_ref[0] = (o.T * gain_ref[...]).astype(o_ref.dtype)


def _diff_attention(p, lam_params, subln_gain, *, b, t, nh, lambda_init, tile=512):
    tile = min(tile, t)
    assert t % tile == 0 and tile % DIFF_KEY_BLOCK == 0
    nq = t // tile
    augq, augk, nslopes = _alibi_tables(nh, tile, tile)
    skip = nq >= DIFF_SKIP_MIN_CHUNKS
    return pl.pallas_call(
        functools.partial(_diff_attn_kernel, tile=tile, lambda_init=lambda_init, skip=skip),
        grid=(b, nh, nq),
        in_specs=[pl.BlockSpec((1, tile, LANES), lambda bi, h, qi: (h, bi * nq + qi, 0)),
                  pl.BlockSpec((1, t, LANES), lambda bi, h, qi: (h, bi, 0)),
                  pl.BlockSpec((1, t, LANES), lambda bi, h, qi: (nh + h, bi, 0)),
                  pl.BlockSpec((1, t, LANES), lambda bi, h, qi: (2 * nh + h, bi, 0)),
                  pl.BlockSpec((1, tile, LANES), lambda bi, h, qi: (h, 0, 0)),
                  pl.BlockSpec((1, 2, tile, LANES), lambda bi, h, qi: (h, 0, 0, 0)),
                  pl.BlockSpec((1, 1, LANES), lambda bi, h, qi: (h, 0, 0)),
                  pl.BlockSpec(memory_space=pltpu.SMEM),
                  pl.BlockSpec(lam_params.shape, lambda bi, h, qi: (0, 0)),
                  pl.BlockSpec((1, LANES), lambda bi, h, qi: (0, 0))],
        out_specs=pl.BlockSpec((1, tile, LANES), lambda bi, h, qi: (bi, qi, h)),
        out_shape=jax.ShapeDtypeStruct((b, t, nh * LANES), BF16),
        scratch_shapes=[pltpu.VMEM((t // tile, LANES + DIFF_PAD_ROWS, tile), BF16),
                        pltpu.VMEM((tile, 2 * tile), F32), pltpu.VMEM((tile, 2 * tile), F32),
                        pltpu.VMEM((tile, 2 * tile), BF16), pltpu.VMEM((tile, 2 * tile), BF16),
                        pltpu.VMEM((1, 2 * tile), F32), pltpu.VMEM((1, 2 * tile), F32),
                        pltpu.VMEM((1, 2 * tile), F32),
                        pltpu.VMEM((LANES + DIFF_PAD_ROWS, 2 * tile), F32),
                        pltpu.SMEM((t // tile,), F32), pltpu.SMEM((t // tile,), F32)],
        compiler_params=_params(("parallel", "parallel", "arbitrary"),
                                7 * t * LANES * 2 + 2 * tile * tile * 20 + 4 * LANES * tile * 4 + 8 * 2**20),
        name="diff_attn",
    )(p, p, p, p, augq, augk, nslopes, -nslopes[:, 0, 0], lam_params, subln_gain.reshape(1, LANES))


def _dil_attn_kernel(q_ref, k_ref, v_ref, ns_ref, o_ref, qf, kf, vf, qg, kg, vg, qr, kr, vr, ob, lb, bt, *, blk):
    t = q_ref.shape[1]

    @pl.when(pl.program_id(1) == 0)
    def _():
        nslope = ns_ref[0][:, :1]
        for n, (window, g) in enumerate(DIL_CONFIGS):
            radius = window // (2 * g)
            wk = min(blk + 2 * radius, t // g)
            rel0 = (lax.broadcasted_iota(jnp.int32, (blk, wk), 1)
                    - lax.broadcasted_iota(jnp.int32, (blk, wk), 0))
            for idx, shift in enumerate((0, -radius, blk - wk)):
                ar = jnp.abs(rel0 + shift)
                bt[n, idx, :, :wk] = jnp.where(ar <= radius, (nslope * float(g)) * ar.astype(F32), MASK_VALUE)

    srcs = {1: (q_ref.at[0], k_ref.at[0], v_ref.at[0])}
    if any(g > 1 for _, g in DIL_CONFIGS):
        qf[...] = q_ref[0].astype(F32)
        kf[...] = k_ref[0].astype(F32)
        vf[...] = v_ref[0].astype(F32)
    f32_src = {1: (qf, kf, vf)}

    for n, (window, g) in enumerate(DIL_CONFIGS):
        radius = window // (2 * g)
        ln = t // g
        wk = min(blk + 2 * radius, ln)
        nblk = ln // blk
        if g not in srcs:
            gp = max(d for d in f32_src if g % d == 0)
            step, lp = g // gp, t // gp
            keep = any(d > g and d % g == 0 for _, d in DIL_CONFIGS)
            assert not (keep and gp > 1), "one f32 staging level"
            dst16 = (qr, kr, vr)
            dst32 = (qg, kg, vg) if keep else (None, None, None)
            for a_src, a16, a32 in zip(f32_src[gp], dst16, dst32):
                for r in range(gp):
                    for s in range(step):
                        rho = r + gp * s
                        rows = a_src[pl.ds(r * lp + s, ln, stride=step), :]
                        a16[rho * ln:(rho + 1) * ln, :] = rows.astype(BF16)
                        if keep:
                            a32[rho * ln:(rho + 1) * ln, :] = rows
            srcs[g] = dst16
            if keep:
                f32_src[g] = dst32
        qs, ks, vs = srcs[g]
        def body(i, carry, n=n, g=g, radius=radius, ln=ln, wk=wk, nblk=nblk, qs=qs, ks=ks, vs=vs):
            rho = i // nblk
            bi = i % nblk
            l0 = bi * blk
            ws = jnp.clip(l0 - radius, 0, ln - wk)
            base = rho * ln
            qb = qs[pl.ds(pl.multiple_of(base + l0, blk), blk), :]
            kw = ks[pl.ds(pl.multiple_of(base + ws, radius), wk), :]
            vw = vs[pl.ds(pl.multiple_of(base + ws, radius), wk), :]
            place = jnp.where(bi == 0, 0, jnp.where(bi == nblk - 1, 2, 1))
            s = lax.dot_general(qb, kw, NT_DIMS, preferred_element_type=F32) + bt[n, place, :, :wk]
            mx = jnp.max(s, axis=1, keepdims=True)
            e = jnp.exp2(s - mx)
            den = jnp.sum(e, axis=1, keepdims=True)
            o = jnp.dot(e.astype(BF16), vw, preferred_element_type=F32) / den
            lse = mx + jnp.log2(den)
            rows = pl.ds(l0 * g + rho, blk, stride=g) if g > 1 else pl.ds(pl.multiple_of(l0, blk), blk)
            ob[n, rows, :] = o
            lb[n, rows, :] = jnp.broadcast_to(lse, (blk, LANES))
            return carry

        lax.fori_loop(0, g * nblk, body, 0, unroll=min(DIL_UNROLL, g * nblk))

    chunk = min(512, t)

    def merge(i, carry):
        rows = pl.ds(pl.multiple_of(i * chunk, chunk), chunk)
        l0, l1, l2 = lb[0, rows, :], lb[1, rows, :], lb[2, rows, :]
        mx = jnp.maximum(jnp.maximum(l0, l1), l2)
        w0, w1, w2 = jnp.exp2(l0 - mx), jnp.exp2(l1 - mx), jnp.exp2(l2 - mx)
        o = (w0 * ob[0, rows, :] + w1 * ob[1, rows, :] + w2 * ob[2, rows, :]) / (w0 + w1 + w2)
        o_ref[0, rows, :] = o.astype(o_ref.dtype)
        return carry

    lax.fori_loop(0, t // chunk, merge, 0)


def _dil_attention(p, nslopes, *, b, t, nh, blk=128):
    nbr = len(DIL_CONFIGS)
    assert all(t % (g * blk) == 0 and window == 2 * g * (blk // 2) for window, g in DIL_CONFIGS)
    return pl.pallas_call(
        functools.partial(_dil_attn_kernel, blk=blk),
        grid=(nh, b),
        in_specs=[pl.BlockSpec((1, t, LANES), lambda h, bi: (3 * nh + h, bi, 0)),
                  pl.BlockSpec((1, t, LANES), lambda h, bi: (4 * nh + h, bi, 0)),
                  pl.BlockSpec((1, t, LANES), lambda h, bi: (5 * nh + h, bi, 0)),
                  pl.BlockSpec((1, 1, LANES), lambda h, bi: (h, 0, 0))],
        out_specs=pl.BlockSpec((1, t, LANES), lambda h, bi: (bi, 0, h)),
        out_shape=jax.ShapeDtypeStruct((b, t, nh * LANES), BF16),
        scratch_shapes=[pltpu.VMEM((t, LANES), F32)] * 6 + [pltpu.VMEM((t, LANES), BF16)] * 3
                       + [pltpu.VMEM((nbr, t, LANES), F32)] * 2 + [pltpu.VMEM((nbr, 3, blk, 2 * blk), F32)],
        compiler_params=_params(("parallel", "arbitrary"),
                                8 * t * LANES * 2 + 3 * t * LANES * 10 + 2 * nbr * t * LANES * 4
                                + DIL_UNROLL * blk * 2 * blk * 16 + 8 * 2**20),
        name="dil_attn",
    )(p, p, p, nslopes)


def _outproj_kernel(ma_ref, md_ref, wt_ref, wb_ref, x_ref, g_ref, o_ref):
    acc = (jnp.dot(ma_ref[...], wt_ref[...], preferred_element_type=F32)
           + jnp.dot(md_ref[...], wb_ref[...], preferred_element_type=F32))
    o_ref[...] = x_ref[...] + g_ref[0] * acc


def _outproj(ma, md, w_out, x2d, gate, *, t, tm=1024, tn=512):
    m, kh = ma.shape
    n = w_out.shape[1]
    tm, tn = min(tm, t), min(tn, n)
    return pl.pallas_call(
        _outproj_kernel,
        grid=(m // tm, n // tn),
        in_specs=[pl.BlockSpec((tm, kh), lambda i, j: (i, 0)),
                  pl.BlockSpec((tm, kh), lambda i, j: (i, 0)),
                  pl.BlockSpec((kh, tn), lambda i, j: (0, j)),
                  pl.BlockSpec((kh, tn), lambda i, j: (1, j)),
                  pl.BlockSpec((tm, tn), lambda i, j: (i, j)),
                  pl.BlockSpec((1, 1, tn), lambda i, j: ((i * tm) // t, 0, j))],
        out_specs=pl.BlockSpec((tm, tn), lambda i, j: (i, j)),
        out_shape=jax.ShapeDtypeStruct((m, n), F32),
        compiler_params=_params(("parallel", "arbitrary"),
                                2 * (2 * tm * kh + 2 * kh * tn) * 2 + 5 * tm * tn * 4 + 2 * 2**20),
        name="outproj",
    )(ma, md, w_out, w_out, x2d, gate)


def _take_top_rows(s, rows_ref, count):
    for lg in range(s.shape[1] // LANES):
        lanes = slice(lg * LANES, (lg + 1) * LANES)
        part = s[:, lanes]
        for r in range(count):
            mx = jnp.max(part, axis=0, keepdims=True)
            rows_ref[r:r + 1, lanes] = mx
            part = jnp.where(part >= mx, -jnp.inf, part)


def _router_kernel(h_ref, wq_ref, keys_ref, thr_ref, s1_ref, a_ref, b_ref, ra_ref, rb_ref, rc_ref):
    k = PEER_TOPK
    sub = keys_ref.shape[2]
    qt = jnp.dot(wq_ref[...], h_ref[...], preferred_element_type=F32)
    s0 = jnp.dot(keys_ref[0, 0], qt[:sub].astype(BF16), preferred_element_type=F32)
    s1 = jnp.dot(keys_ref[0, 1], qt[sub:].astype(BF16), preferred_element_type=F32)
    _take_top_rows(s0, ra_ref, k)
    _take_top_rows(s1, rb_ref, k)
    ra, rb = ra_ref[...], rb_ref[...]
    row8 = lax.broadcasted_iota(jnp.int32, (8, ra.shape[1]), 0)
    pieces = [ra[0:1] + rb, ra[1:2] + rb[0:8]]
    for r in range(2, 8):
        pieces.append(jnp.where(row8 < k // (r + 1), ra[r:r + 1] + rb[0:8], -jnp.inf))
    pieces.append(ra[8:16] + rb[0:1])
    cand = jnp.concatenate(pieces, axis=0)
    _take_top_rows(cand, rc_ref, k)
    tau = rc_ref[k - 1:k, :]
    top = ra[0:1] + rb[0:1]
    z = jnp.sum(jnp.where(cand >= tau, jnp.exp(cand - top), 0.0), axis=0, keepdims=True)
    thr = jnp.full(s0.shape, jnp.inf, F32)
    for r in range(k):
        row_thr = jnp.min(jnp.where(ra[r:r + 1] + rb >= tau, rb, jnp.inf), axis=0, keepdims=True)
        thr = jnp.where(s0 == ra[r:r + 1], row_thr, thr)
    thr_ref[0] = thr
    s1_ref[0] = s1
    a_ref[0] = jnp.exp(s0 - ra[0:1]) / z
    b_ref[0] = jnp.exp(s1 - rb[0:1])


def _router(ht, wq_t, keys, *, tm=1024):
    d, m = ht.shape
    heads, _, n_keys, sub = keys.shape
    tm = min(tm, m)
    big = jax.ShapeDtypeStruct((heads, n_keys, m), F32)
    big_spec = pl.BlockSpec((1, n_keys, tm), lambda i, h: (h, 0, i))
    return pl.pallas_call(
        _router_kernel,
        grid=(m // tm, heads),
        in_specs=[pl.BlockSpec((d, tm), lambda i, h: (0, i)),
                  pl.BlockSpec((2 * sub, d), lambda i, h: (h, 0)),
                  pl.BlockSpec((1, 2, n_keys, sub), lambda i, h: (h, 0, 0, 0))],
        out_specs=[big_spec, big_spec, big_spec, big_spec],
        out_shape=[big, big, big, big],
        scratch_shapes=[pltpu.VMEM((PEER_TOPK, tm), F32)] * 3,
        compiler_params=_params(("parallel", "arbitrary"),
                                2 * d * tm * 2 + 2 * 2 * sub * d * 2 + 8 * n_keys * tm * 4 + 16 * 2**20),
        name="peer_router",
    )(ht, wq_t, keys)


def _expert_table_kernel(x_ref, q_ref, inv_ref, nrm_ref, *, transpose):
    x = x_ref[...]
    amax = jnp.maximum(jnp.max(jnp.abs(x), axis=1, keepdims=True), TINY)
    xs = x * (F8_TARGET / amax)
    q_ref[...] = (xs.T if transpose else xs).astype(q_ref.dtype)
    inv_ref[...] = jnp.broadcast_to(amax * (1.0 / F8_TARGET), inv_ref.shape)
    nrm_ref[...] = jnp.broadcast_to(jnp.sqrt(jnp.sum(x * x, axis=1, keepdims=True)), nrm_ref.shape)


def _expert_table(x, *, transpose, te=256):
    n_exp, d = x.shape
    side = jax.ShapeDtypeStruct((n_exp, LANES), F32)
    side_spec = pl.BlockSpec((te, LANES), lambda e: (e, 0))
    return pl.pallas_call(
        functools.partial(_expert_table_kernel, transpose=transpose),
        grid=(n_exp // te,),
        in_specs=[pl.BlockSpec((te, d), lambda e: (e, 0))],
        out_specs=[pl.BlockSpec((d, te), lambda e: (0, e)) if transpose else pl.BlockSpec((te, d), lambda e: (e, 0)),
                   side_spec, side_spec],
        out_shape=[jax.ShapeDtypeStruct((d, n_exp) if transpose else (n_exp, d), F8), side, side],
        compiler_params=_params(("parallel",), 6 * te * d * 4 + 4 * 2**20),
        name="expert_table_t" if transpose else "expert_table",
    )(x)


def _peer_dense_kernel(h_ref, u_ref, vt_ref, ui_ref, vi_ref, sc_ref, thr_ref, s1_ref, a_ref, b_ref, o_ref,
                       g_sc, acc_sc):
    e = pl.program_id(1)
    heads, n_keys, tm = s1_ref.shape
    te = u_ref.shape[0]
    sub = min(te, PEER_SUBTILE)

    @pl.when(e == 0)
    def _():
        acc_sc[...] = jnp.zeros(acc_sc.shape, F32)

    for ii in range(te // n_keys):
        i = e * (te // n_keys) + ii
        gates = jnp.zeros((n_keys, tm), F32)
        for h in range(heads):
            gates = gates + (jnp.where(s1_ref[h] >= thr_ref[h, pl.ds(i, 1), :], b_ref[h], 0.0)
                             * a_ref[h, pl.ds(i, 1), :])
        g_sc[ii * n_keys:(ii + 1) * n_keys, :] = gates
    nsub = te // sub
    act_scale, w_scale = sc_ref[0:1, :], sc_ref[1:2, :]
    across = lambda col: jnp.concatenate([col] * (tm // LANES), axis=1)
    acts = [jnp.dot(u_ref[t * sub:(t + 1) * sub, :], h_ref[...], preferred_element_type=F32)
            * across(ui_ref[t * sub:(t + 1) * sub, :]) * act_scale for t in range(nsub)]
    for t in range(nsub):
        gel = 0.5 * acts[t] * (1.0 + lax.erf(acts[t] * (1.0 / math.sqrt(2.0))))
        w = (g_sc[t * sub:(t + 1) * sub, :] * gel * across(vi_ref[t * sub:(t + 1) * sub, :])
             * w_scale).astype(vt_ref.dtype)
        acc_sc[...] += jnp.dot(vt_ref[:, t * sub:(t + 1) * sub], w, preferred_element_type=F32)

    @pl.when(e == pl.num_programs(1) - 1)
    def _():
        o_ref[...] = acc_sc[...].astype(o_ref.dtype)


def _peer_dense(ht, u, vt, u_inv, v_inv, scales, thr, s1, a, bf, *, tm=512, te=1024):
    d, m = ht.shape
    n_exp = u.shape[0]
    heads, n_keys, _ = s1.shape
    tm = min(tm, m)
    assert m % tm == 0 and n_exp % te == 0 and te % PEER_SUBTILE == 0 and PEER_SUBTILE % n_keys == 0
    once = pl.Buffered(1)
    tok = lambda i, e: (0, 0, i)
    big_spec = pl.BlockSpec((heads, n_keys, tm), tok, pipeline_mode=once)
    return pl.pallas_call(
        _peer_dense_kernel,
        grid=(m // tm, n_exp // te),
        in_specs=[pl.BlockSpec((d, tm), lambda i, e: (0, i), pipeline_mode=once),
                  pl.BlockSpec((te, d), lambda i, e: (e, 0)),
                  pl.BlockSpec((d, te), lambda i, e: (0, e)),
                  pl.BlockSpec((te, LANES), lambda i, e: (e, 0)),
                  pl.BlockSpec((te, LANES), lambda i, e: (e, 0)),
                  pl.BlockSpec((2, tm), lambda i, e: (0, i), pipeline_mode=once),
                  big_spec, big_spec, big_spec, big_spec],
        out_specs=pl.BlockSpec((d, tm), lambda i, e: (0, i)),
        out_shape=jax.ShapeDtypeStruct((d, m), BF16),
        scratch_shapes=[pltpu.VMEM((te, tm), F32), pltpu.VMEM((d, tm), F32)],
        compiler_params=_params(("parallel", "arbitrary"),
                                d * tm * 2 + 4 * te * d * 2 + 2 * d * tm * 4 + 4 * heads * n_keys * tm * 4
                                + 6 * te * tm * 4 + d * tm * 4),
        name="peer_dense",
    )(ht, u, vt, u_inv, v_inv, scales, thr, s1, a, bf)


def _final_kernel(x_ref, pt_ref, ps_ref, g_ref, gain_ref, o_ref):
    x = x_ref[...] + g_ref[0] * (pt_ref[...].astype(F32) * ps_ref[...]).T
    ms = jnp.mean(x * x, axis=-1, keepdims=True)
    o_ref[...] = x * lax.rsqrt(ms + EPS) * gain_ref[...]


def _final(x1, pt, pscale, gate, gain, *, t, tm=256):
    m, d = x1.shape
    tm = min(tm, t)
    return pl.pallas_call(
        _final_kernel,
        grid=(m // tm,),
        in_specs=[pl.BlockSpec((tm, d), lambda i: (i, 0)),
                  pl.BlockSpec((d, tm), lambda i: (0, i)),
                  pl.BlockSpec((1, tm), lambda i: (0, i)),
                  pl.BlockSpec((1, 1, d), lambda i: ((i * tm) // t, 0, 0)),
                  pl.BlockSpec((1, d), lambda i: (0, 0))],
        out_specs=pl.BlockSpec((tm, d), lambda i: (i, 0)),
        out_shape=jax.ShapeDtypeStruct((m, d), F32),
        compiler_params=_params(("parallel",), 8 * tm * d * 4 + 4 * 2**20),
        name="final_norm",
    )(x1, pt, pscale, gate, gain.reshape(1, d))


def _trunk(x, ada, w, depth_index=0):
    b, t, d = x.shape
    nh = d // (2 * LANES)
    lambda_init = 0.8 - 0.6 * math.exp(-0.3 * depth_index)
    sh1, sc1, g1, sh2, sc2, g2 = [a.reshape(b, 1, d) for a in jnp.split(ada, 6, axis=-1)]

    h = _norm_mod(x, w["norm1_gain"], sc1, sh1, transpose=False)
    p = _inproj(h, w["w_in"], w["colscale"])
    ma = _diff_attention(p, w["diff_lambda"], w["diff_subln_gain"], b=b, t=t, nh=nh, lambda_init=lambda_init)
    md = _dil_attention(p, w["dil_nslopes"], b=b, t=t, nh=nh)
    x1 = _outproj(ma.reshape(b * t, nh * LANES), md.reshape(b * t, nh * LANES), w["w_out"],
                  x.reshape(b * t, d), g1, t=t)

    ht, ht8, stat = _norm_mod(x1.reshape(b, t, d), w["norm2_gain"], sc2, sh2, transpose=True)
    thr, s1, a, bf = _router(ht, w["wq_t"], w["sub_keys"])
    w_scale = F8_TARGET / jnp.maximum(stat[1:2] * w["uv_bound"], TINY)
    pt = _peer_dense(ht8, w["peer_u"], w["peer_vt"], w["u_inv"], w["v_inv"],
                     jnp.concatenate([stat[0:1], w_scale], axis=0), thr, s1, a, bf)
    y = _final(x1, pt, 1.0 / w_scale, g2, w["final_gain"], t=t)
    return y.reshape(b, t, d)


def kernel(x_prompt, x_sample, c_prompt, c_sample, norm1_gain, norm2_gain, w_ada, b_ada, w_in, diff_lambda,
           diff_subln_gain, w_out, peer_w_query, peer_sub_keys, peer_u, peer_v, final_gain):
    d = x_prompt.shape[-1]
    nh = d // (2 * LANES)
    bp, bs = c_prompt.shape[0], c_sample.shape[0]
    assert norm1_gain.shape[0] == 1, "single-layer trunk"
    assert peer_sub_keys.shape[3] == LANES and peer_sub_keys.shape[4] == LANES

    rows = -(-(bp + bs) // 8) * 8
    c_all = jnp.concatenate([c_prompt, c_sample, jnp.zeros((rows - bp - bs, d), F32)], axis=0)
    ada = _ada(c_all, w_ada[0], b_ada[0])

    colscale = np.ones((1, w_in.shape[-1]), np.float32)
    colscale[0, :nh * LANES] = DIFF_HALF_DIM ** -0.5 * math.log2(math.e)
    colscale[0, 3 * nh * LANES:4 * nh * LANES] = LANES ** -0.5 * math.log2(math.e)
    lane_bcast = lambda v: jnp.asarray(np.broadcast_to(np.asarray(v, np.float32)[:, None, None], (nh, 1, LANES)))
    w = dict(
        norm1_gain=norm1_gain[0], norm2_gain=norm2_gain[0], final_gain=final_gain,
        w_in=w_in[0].astype(BF16), colscale=jnp.asarray(colscale),
        dil_nslopes=lane_bcast(-_alibi_slopes(nh) * math.log2(math.e)),
        diff_lambda=diff_lambda[0], diff_subln_gain=diff_subln_gain[0],
        w_out=w_out[0].astype(BF16),
        wq_t=peer_w_query[0].T.astype(BF16), sub_keys=peer_sub_keys[0].astype(BF16),
    )
    peer_u8, u_inv, u_nrm = _expert_table(peer_u[0], transpose=False)
    peer_vt8, v_inv, _ = _expert_table(peer_v[0], transpose=True)
    w.update(peer_u=peer_u8, peer_vt=peer_vt8, u_inv=u_inv, v_inv=v_inv,
             uv_bound=jnp.max(u_nrm[:, 0]) * jnp.max(v_inv[:, 0]))
    y_prompt = _trunk(x_prompt, ada[:bp], w)
    y_sample = _trunk(x_sample, ada[bp:bp + bs], w)
    return (y_prompt, y_sample)
```

```python
import functools
import math

import ml_dtypes
import numpy as np
import jax
import jax.numpy as jnp
from jax import lax
from jax.experimental import pallas as pl
from jax.experimental.pallas import tpu as pltpu

F32 = jnp.float32
BF16 = jnp.bfloat16
F8 = jnp.float8_e4m3fn
F8_TARGET = 240.0
TINY = 1e-30
EPS = 1e-6
MASK_VALUE = -1e30
LANES = 128
VMEM_CAP_BYTES = 60000 * 1024
DIFF_HALF_DIM = 64
DIFF_KEY_BLOCK = 256
DIFF_SKIP_MIN_CHUNKS = 8
DIFF_SKIP_MARGIN = 140.0
DIFF_PAD_ROWS = 16
DIL_CONFIGS = ((128, 1), (512, 4), (2048, 16))
DIL_UNROLL = 32
PEER_TOPK = 16
PEER_SUBTILE = 256
NT_DIMS = (((1,), (1,)), ((), ()))


def _params(semantics, vmem_bytes, fuse_inputs=None):
    return pltpu.CompilerParams(
        dimension_semantics=semantics,
        allow_input_fusion=fuse_inputs,
        vmem_limit_bytes=int(min(max(vmem_bytes, 16 * 2**20), VMEM_CAP_BYTES)))


def _alibi_slopes(n):
    return 2.0 ** (-8.0 * np.arange(1, n + 1) / n)


def _ada_kernel(c_ref, w_ref, b_ref, o_ref):
    c = c_ref[...]
    a = (c / (1.0 + jnp.exp(-c))).astype(BF16)
    o_ref[...] = jnp.dot(a, w_ref[...].astype(BF16), preferred_element_type=F32) + b_ref[...]


def _ada(c_all, w_ada, b_ada):
    rows, d = c_all.shape
    n = w_ada.shape[1]
    tn = 512
    return pl.pallas_call(
        _ada_kernel,
        grid=(n // tn,),
        in_specs=[pl.BlockSpec((rows, d), lambda j: (0, 0)),
                  pl.BlockSpec((d, tn), lambda j: (0, j)),
                  pl.BlockSpec((1, tn), lambda j: (0, j))],
        out_specs=pl.BlockSpec((rows, tn), lambda j: (0, j)),
        out_shape=jax.ShapeDtypeStruct((rows, n), F32),
        compiler_params=_params(("arbitrary",), 2 * d * tn * 4 + d * tn * 2 + 4 * 2**20),
        name="ada",
    )(c_all, w_ada, b_ada.reshape(1, n))


def _norm_mod_kernel(x_ref, gain_ref, sc_ref, sh_ref, o_ref, *rest, transpose):
    x = x_ref[0]
    ms = jnp.mean(x * x, axis=-1, keepdims=True)
    y = x * lax.rsqrt(ms + EPS) * gain_ref[...]
    h = y * (1.0 + sc_ref[0]) + sh_ref[0]
    if transpose:
        q_ref, stat_ref = rest
        ht = h.T
        o_ref[...] = ht.astype(o_ref.dtype)
        amax = jnp.maximum(jnp.max(jnp.abs(ht), axis=0, keepdims=True), TINY)
        q_ref[...] = (ht * (F8_TARGET / amax)).astype(q_ref.dtype)
        stat_ref[0:1, :] = amax * (1.0 / F8_TARGET)
        stat_ref[1:2, :] = jnp.sqrt(jnp.sum(ht * ht, axis=0, keepdims=True))
    else:
        o_ref[...] = h.astype(o_ref.dtype)


def _norm_mod(x, gain, sc, sh, *, transpose, tm=256):
    b, t, d = x.shape
    m = b * t
    tm = min(tm, t)
    nt = t // tm
    if transpose:
        col = lambda bi, i: (0, bi * nt + i)
        out_spec = [pl.BlockSpec((d, tm), col), pl.BlockSpec((d, tm), col), pl.BlockSpec((2, tm), col)]
        out_shape = [jax.ShapeDtypeStruct((d, m), BF16), jax.ShapeDtypeStruct((d, m), F8),
                     jax.ShapeDtypeStruct((2, m), F32)]
    else:
        out_spec = pl.BlockSpec((tm, d), lambda bi, i: (bi * nt + i, 0))
        out_shape = jax.ShapeDtypeStruct((m, d), BF16)
    return pl.pallas_call(
        functools.partial(_norm_mod_kernel, transpose=transpose),
        grid=(b, nt),
        in_specs=[pl.BlockSpec((1, tm, d), lambda bi, i: (bi, i, 0)),
                  pl.BlockSpec((1, d), lambda bi, i: (0, 0)),
                  pl.BlockSpec((1, 1, d), lambda bi, i: (bi, 0, 0)),
                  pl.BlockSpec((1, 1, d), lambda bi, i: (bi, 0, 0))],
        out_specs=out_spec,
        out_shape=out_shape,
        compiler_params=_params(("parallel", "parallel"), 8 * tm * d * 4 + 4 * 2**20),
        name="norm_mod_t" if transpose else "norm_mod",
    )(x, gain.reshape(1, d), sc, sh)


def _inproj_kernel(a_ref, w_ref, cs_ref, o_ref):
    acc = jnp.dot(a_ref[...], w_ref[...], preferred_element_type=F32) * cs_ref[...]
    for c in range(o_ref.shape[0]):
        o_ref[c] = acc[:, c * LANES:(c + 1) * LANES].astype(o_ref.dtype)


def _inproj(h, w, colscale, *, tm=1024, tn=1024):
    m, k = h.shape
    n = w.shape[1]
    tm = min(tm, m)
    tn = max(c for c in (tn, tn // 2, tn // 4, LANES) if n % c == 0)
    assert m % tm == 0
    nb = tn // LANES
    return pl.pallas_call(
        _inproj_kernel,
        grid=(m // tm, n // tn),
        in_specs=[pl.BlockSpec((tm, k), lambda i, j: (i, 0)),
                  pl.BlockSpec((k, tn), lambda i, j: (0, j)),
                  pl.BlockSpec((1, tn), lambda i, j: (0, j))],
        out_specs=pl.BlockSpec((nb, tm, LANES), lambda i, j: (j, i, 0)),
        out_shape=jax.ShapeDtypeStruct((n // LANES, m, LANES), BF16),
        compiler_params=_params(("parallel", "arbitrary"),
                                2 * (tm * k + k * tn + tm * tn) * 2 + 2 * tm * tn * 4 + 2 * 2**20,
                                fuse_inputs=[False, True, False]),
        name="inproj",
    )(h, w, colscale)


def _alibi_tables(nh, tq, tk):
    bf = lambda a: np.asarray(a, np.float64).astype(ml_dtypes.bfloat16).astype(np.float64)
    slope = _alibi_slopes(nh) * math.log2(math.e)
    hi = bf(slope)
    lo = bf(slope - hi)
    augq = np.zeros((nh, tq, LANES), np.float64)
    augk = np.zeros((nh, 2, tk, LANES), np.float64)
    qpos, kpos = np.arange(tq), np.arange(tk)
    for h in range(nh):
        qcols = [qpos // 16, qpos // 16, qpos % 16, qpos % 16,
                 16 * hi[h] + 0 * qpos, 16 * lo[h] + 0 * qpos, hi[h] + 0 * qpos, lo[h] + 0 * qpos]
        kcols = [-16 * hi[h] + 0 * kpos, -16 * lo[h] + 0 * kpos, -hi[h] + 0 * kpos, -lo[h] + 0 * kpos,
                 kpos // 16, kpos // 16, kpos % 16, kpos % 16]
        for c in range(8):
            for base in (0, DIFF_HALF_DIM):
                augq[h, :, base + c] = qcols[c]
                augk[h, 0, :, base + c] = kcols[c]
                augk[h, 1, :, base + c] = -kcols[c]
    nslope = np.broadcast_to(-(hi + lo)[:, None, None], (nh, 1, LANES))
    return (jnp.asarray(augq, BF16), jnp.asarray(augk, BF16), jnp.asarray(nslope, F32))


def _diff_attn_kernel(q_ref, qall_ref, k_ref, v_ref, augq_ref, augk_ref, ns_ref, sl_ref, lp_ref, gain_ref, o_ref,
                      vt_sc, st0, st1, p0, p1, al0, al1, m_sc, acc_sc, qmax_sm, kmax_sm, *, tile, lambda_init, skip):
    qi = pl.program_id(2)
    t = k_ref.shape[1]
    n = t // tile
    kb = min(tile, DIFF_KEY_BLOCK)
    groups = tile // LANES

    @pl.when(qi == 0)
    def _():
        ones_row = jnp.where(lax.broadcasted_iota(jnp.int32, (DIFF_PAD_ROWS, tile), 0) == 0, 1.0, 0.0)
        for c in range(n):
            vt_sc[c, :LANES, :] = v_ref[0, c * tile:(c + 1) * tile, :].astype(F32).T.astype(BF16)
            vt_sc[c, LANES:, :] = ones_row.astype(BF16)
            if skip:
                kf = k_ref[0, c * tile:(c + 1) * tile, :].astype(F32)
                kmax_sm[c] = jnp.max(jnp.sqrt(jnp.max(jnp.sum(kf * kf, axis=1, keepdims=True), axis=0, keepdims=True)))
                qf = qall_ref[0, c * tile:(c + 1) * tile, :].astype(F32)
                qmax_sm[c] = jnp.max(jnp.sqrt(jnp.max(jnp.sum(qf * qf, axis=1, keepdims=True), axis=0, keepdims=True)))

    q = q_ref[0]
    lane = lax.broadcasted_iota(jnp.int32, q.shape, 1)
    low = lane < DIFF_HALF_DIM
    aq = augq_ref[0]
    qxt = [jnp.where(low, q, aq).astype(F32).T.astype(BF16),
           jnp.where(low, aq, q).astype(F32).T.astype(BF16)]
    nslope = ns_ref[0][:1, :1]
    m_sc[...] = jnp.full(m_sc.shape, MASK_VALUE, F32)
    acc_sc[...] = jnp.zeros(acc_sc.shape, F32)
    st_sc, p_sc, al_sc = (st0, st1), (p0, p1), (al0, al1)

    if skip:
        bq = qmax_sm[qi]
        slope = sl_ref[pl.program_id(1)]
        floor_m = -(bq * kmax_sm[qi]) - DIFF_SKIP_MARGIN

        def dead(j):
            gap = ((jnp.abs(qi - j) - 1) * tile + 1).astype(F32)
            return bq * kmax_sm[j] - slope * gap < floor_m

        lo = jnp.int32(0)
        for j in range(n - 1):
            lo = lo + ((lo == j) & (j < qi) & dead(j)).astype(jnp.int32)
        hi = jnp.int32(n - 1)
        for j in range(n - 1, 0, -1):
            hi = hi - ((hi == j) & (j > qi) & dead(j)).astype(jnp.int32)
        odd = ((hi - lo + 1 - n) & 1) == 1
        lo, hi = jnp.where(odd & (lo > 0), lo - 1, lo), jnp.where(odd & (lo == 0), hi + 1, hi)
    else:
        lo, hi = 0, n - 1
    cnt = hi - lo + 1

    def chunk_at(pos):
        nxt = lo + pos - 1
        return jnp.where(pos == 0, qi, nxt + (nxt >= qi).astype(jnp.int32))

    def scores(j, side):
        rows = pl.ds(pl.multiple_of(j * tile, tile), tile)
        k, ak = k_ref[0, rows, :], augk_ref[0, side]
        return jnp.concatenate([jnp.dot(jnp.where(low, k, ak), qxt[0], preferred_element_type=F32),
                                jnp.dot(jnp.where(low, ak, k), qxt[1], preferred_element_type=F32)],
                               axis=1)

    def softmax(j, buf):
        offset = nslope * (jnp.abs(qi - j) * tile).astype(F32)
        for g in range(2 * groups):
            cols = slice(g * LANES, (g + 1) * LANES)
            m_old = m_sc[:, cols]
            mx = m_old
            for r in range(tile // kb):
                s = st_sc[buf][r * kb:(r + 1) * kb, cols]
                mx = jnp.maximum(mx, jnp.max(s, axis=0, keepdims=True) + offset)
            al_sc[buf][:, cols] = jnp.exp2(m_old - mx)
            m_sc[:, cols] = mx
            sub = mx - offset
            for r in range(tile // kb):
                rows = slice(r * kb, (r + 1) * kb)
                p_sc[buf][rows, cols] = jnp.exp2(st_sc[buf][rows, cols] - sub).astype(BF16)

    def accumulate(j, buf):
        acc_sc[...] = al_sc[buf][...] * acc_sc[...] + jnp.dot(vt_sc[j], p_sc[buf][...], preferred_element_type=F32)

    p_sc[1][...] = jnp.zeros(p_sc[1].shape, BF16)
    al_sc[1][...] = jnp.ones(al_sc[1].shape, F32)
    st_sc[0][...] = jnp.minimum(scores(qi, 0), scores(qi, 1))

    def step(pos, buf):
        nxt = chunk_at(pos + 1)
        st_sc[1 - buf][...] = scores(nxt, (nxt > qi).astype(jnp.int32))
        softmax(chunk_at(pos), buf)
        accumulate(chunk_at(jnp.maximum(pos - 1, 0)), 1 - buf)

    def body(i, carry):
        step(2 * i, 0)
        step(2 * i + 1, 1)
        return carry

    lax.fori_loop(0, (cnt - 1) // 2, body, 0)
    if (n - 1) % 2:
        step(cnt - 2, (n - 2) & 1)
    last = (n - 1) & 1
    softmax(chunk_at(cnt - 1), last)
    if n > 1:
        accumulate(chunk_at(jnp.maximum(cnt - 2, 0)), 1 - last)
    accumulate(chunk_at(cnt - 1), last)

    lp = lp_ref[...]
    lam = (jnp.exp(jnp.sum(lp[0:1] * lp[1:2], axis=1, keepdims=True))
           - jnp.exp(jnp.sum(lp[2:3] * lp[3:4], axis=1, keepdims=True)) + lambda_init)
    on = acc_sc[:LANES, :] / acc_sc[LANES:LANES + 1, :]
    o = on[:, :tile] - lam * on[:, tile:]
    o = o * lax.rsqrt(jnp.mean(o * o, axis=0, keepdims=True) + EPS) * (1.0 - lambda_init)
    o_ref[0] = (o.T * gain_ref[...]).astype(o_ref.dtype)


def _diff_attention(p, lam_params, subln_gain, *, b, t, nh, lambda_init, tile=512):
    tile = min(tile, t)
    assert t % tile == 0 and tile % DIFF_KEY_BLOCK == 0
    nq = t // tile
    augq, augk, nslopes = _alibi_tables(nh, tile, tile)
    skip = nq >= DIFF_SKIP_MIN_CHUNKS
    return pl.pallas_call(
        functools.partial(_diff_attn_kernel, tile=tile, lambda_init=lambda_init, skip=skip),
        grid=(b, nh, nq),
        in_specs=[pl.BlockSpec((1, tile, LANES), lambda bi, h, qi: (h, bi * nq + qi, 0)),
                  pl.BlockSpec((1, t, LANES), lambda bi, h, qi: (h, bi, 0)),
                  pl.BlockSpec((1, t, LANES), lambda bi, h, qi: (nh + h, bi, 0)),
                  pl.BlockSpec((1, t, LANES), lambda bi, h, qi: (2 * nh + h, bi, 0)),
                  pl.BlockSpec((1, tile, LANES), lambda bi, h, qi: (h, 0, 0)),
                  pl.BlockSpec((1, 2, tile, LANES), lambda bi, h, qi: (h, 0, 0, 0)),
                  pl.BlockSpec((1, 1, LANES), lambda bi, h, qi: (h, 0, 0)),
                  pl.BlockSpec(memory_space=pltpu.SMEM),
                  pl.BlockSpec(lam_params.shape, lambda bi, h, qi: (0, 0)),
                  pl.BlockSpec((1, LANES), lambda bi, h, qi: (0, 0))],
        out_specs=pl.BlockSpec((1, tile, LANES), lambda bi, h, qi: (bi, qi, h)),
        out_shape=jax.ShapeDtypeStruct((b, t, nh * LANES), BF16),
        scratch_shapes=[pltpu.VMEM((t // tile, LANES + DIFF_PAD_ROWS, tile), BF16),
                        pltpu.VMEM((tile, 2 * tile), F32), pltpu.VMEM((tile, 2 * tile), F32),
                        pltpu.VMEM((tile, 2 * tile), BF16), pltpu.VMEM((tile, 2 * tile), BF16),
                        pltpu.VMEM((1, 2 * tile), F32), pltpu.VMEM((1, 2 * tile), F32),
                        pltpu.VMEM((1, 2 * tile), F32),
                        pltpu.VMEM((LANES + DIFF_PAD_ROWS, 2 * tile), F32),
                        pltpu.SMEM((t // tile,), F32), pltpu.SMEM((t // tile,), F32)],
        compiler_params=_params(("parallel", "parallel", "arbitrary"),
                                7 * t * LANES * 2 + 2 * tile * tile * 20 + 4 * LANES * tile * 4 + 8 * 2**20),
        name="diff_attn",
    )(p, p, p, p, augq, augk, nslopes, -nslopes[:, 0, 0], lam_params, subln_gain.reshape(1, LANES))


def _dil_attn_kernel(q_ref, k_ref, v_ref, ns_ref, o_ref, qf, kf, vf, qg, kg, vg, qr, kr, vr, ob, lb, bt, *, blk):
    t = q_ref.shape[1]

    @pl.when(pl.program_id(1) == 0)
    def _():
        nslope = ns_ref[0][:, :1]
        for n, (window, g) in enumerate(DIL_CONFIGS):
            radius = window // (2 * g)
            wk = min(blk + 2 * radius, t // g)
            rel0 = (lax.broadcasted_iota(jnp.int32, (blk, wk), 1)
                    - lax.broadcasted_iota(jnp.int32, (blk, wk), 0))
            for idx, shift in enumerate((0, -radius, blk - wk)):
                ar = jnp.abs(rel0 + shift)
                bt[n, idx, :, :wk] = jnp.where(ar <= radius, (nslope * float(g)) * ar.astype(F32), MASK_VALUE)

    srcs = {1: (q_ref.at[0], k_ref.at[0], v_ref.at[0])}
    if any(g > 1 for _, g in DIL_CONFIGS):
        qf[...] = q_ref[0].astype(F32)
        kf[...] = k_ref[0].astype(F32)
        vf[...] = v_ref[0].astype(F32)
    f32_src = {1: (qf, kf, vf)}

    for n, (window, g) in enumerate(DIL_CONFIGS):
        radius = window // (2 * g)
        ln = t // g
        wk = min(blk + 2 * radius, ln)
        nblk = ln // blk
        if g not in srcs:
            gp = max(d for d in f32_src if g % d == 0)
            step, lp = g // gp, t // gp
            keep = any(d > g and d % g == 0 for _, d in DIL_CONFIGS)
            assert not (keep and gp > 1), "one f32 staging level"
            dst16 = (qr, kr, vr)
            dst32 = (qg, kg, vg) if keep else (None, None, None)
            for a_src, a16, a32 in zip(f32_src[gp], dst16, dst32):
                for r in range(gp):
                    for s in range(step):
                        rho = r + gp * s
                        rows = a_src[pl.ds(r * lp + s, ln, stride=step), :]
                        a16[rho * ln:(rho + 1) * ln, :] = rows.astype(BF16)
                        if keep:
                            a32[rho * ln:(rho + 1) * ln, :] = rows
            srcs[g] = dst16
            if keep:
                f32_src[g] = dst32
        qs, ks, vs = srcs[g]
        def body(i, carry, n=n, g=g, radius=radius, ln=ln, wk=wk, nblk=nblk, qs=qs, ks=ks, vs=vs):
            rho = i // nblk
            bi = i % nblk
            l0 = bi * blk
            ws = jnp.clip(l0 - radius, 0, ln - wk)
            base = rho * ln
            qb = qs[pl.ds(pl.multiple_of(base + l0, blk), blk), :]
            kw = ks[pl.ds(pl.multiple_of(base + ws, radius), wk), :]
            vw = vs[pl.ds(pl.multiple_of(base + ws, radius), wk), :]
            place = jnp.where(bi == 0, 0, jnp.where(bi == nblk - 1, 2, 1))
            s = lax.dot_general(qb, kw, NT_DIMS, preferred_element_type=F32) + bt[n, place, :, :wk]
            mx = jnp.max(s, axis=1, keepdims=True)
            e = jnp.exp2(s - mx)
            den = jnp.sum(e, axis=1, keepdims=True)
            o = jnp.dot(e.astype(BF16), vw, preferred_element_type=F32) / den
            lse = mx + jnp.log2(den)
            rows = pl.ds(l0 * g + rho, blk, stride=g) if g > 1 else pl.ds(pl.multiple_of(l0, blk), blk)
            ob[n, rows, :] = o
            lb[n, rows, :] = jnp.broadcast_to(lse, (blk, LANES))
            return carry

        lax.fori_loop(0, g * nblk, body, 0, unroll=min(DIL_UNROLL, g * nblk))

    chunk = min(512, t)

    def merge(i, carry):
        rows = pl.ds(pl.multiple_of(i * chunk, chunk), chunk)
        l0, l1, l2 = lb[0, rows, :], lb[1, rows, :], lb[2, rows, :]
        mx = jnp.maximum(jnp.maximum(l0, l1), l2)
        w0, w1, w2 = jnp.exp2(l0 - mx), jnp.exp2(l1 - mx), jnp.exp2(l2 - mx)
        o = (w0 * ob[0, rows, :] + w1 * ob[1, rows, :] + w2 * ob[2, rows, :]) / (w0 + w1 + w2)
        o_ref[0, rows, :] = o.astype(o_ref.dtype)
        return carry

    lax.fori_loop(0, t // chunk, merge, 0)


def _dil_attention(p, nslopes, *, b, t, nh, blk=128):
    nbr = len(DIL_CONFIGS)
    assert all(t % (g * blk) == 0 and window == 2 * g * (blk // 2) for window, g in DIL_CONFIGS)
    return pl.pallas_call(
        functools.partial(_dil_attn_kernel, blk=blk),
        grid=(nh, b),
        in_specs=[pl.BlockSpec((1, t, LANES), lambda h, bi: (3 * nh + h, bi, 0)),
                  pl.BlockSpec((1, t, LANES), lambda h, bi: (4 * nh + h, bi, 0)),
                  pl.BlockSpec((1, t, LANES), lambda h, bi: (5 * nh + h, bi, 0)),
                  pl.BlockSpec((1, 1, LANES), lambda h, bi: (h, 0, 0))],
        out_specs=pl.BlockSpec((1, t, LANES), lambda h, bi: (bi, 0, h)),
        out_shape=jax.ShapeDtypeStruct((b, t, nh * LANES), BF16),
        scratch_shapes=[pltpu.VMEM((t, LANES), F32)] * 6 + [pltpu.VMEM((t, LANES), BF16)] * 3
                       + [pltpu.VMEM((nbr, t, LANES), F32)] * 2 + [pltpu.VMEM((nbr, 3, blk, 2 * blk), F32)],
        compiler_params=_params(("parallel", "arbitrary"),
                                8 * t * LANES * 2 + 3 * t * LANES * 10 + 2 * nbr * t * LANES * 4
                                + DIL_UNROLL * blk * 2 * blk * 16 + 8 * 2**20),
        name="dil_attn",
    )(p, p, p, nslopes)


def _outproj_kernel(ma_ref, md_ref, wt_ref, wb_ref, x_ref, g_ref, o_ref):
    acc = (jnp.dot(ma_ref[...], wt_ref[...], preferred_element_type=F32)
           + jnp.dot(md_ref[...], wb_ref[...], preferred_element_type=F32))
    o_ref[...] = x_ref[...] + g_ref[0] * acc


def _outproj(ma, md, w_out, x2d, gate, *, t, tm=1024, tn=512):
    m, kh = ma.shape
    n = w_out.shape[1]
    tm, tn = min(tm, t), min(tn, n)
    return pl.pallas_call(
        _outproj_kernel,
        grid=(m // tm, n // tn),
        in_specs=[pl.BlockSpec((tm, kh), lambda i, j: (i, 0)),
                  pl.BlockSpec((tm, kh), lambda i, j: (i, 0)),
                  pl.BlockSpec((kh, tn), lambda i, j: (0, j)),
                  pl.BlockSpec((kh, tn), lambda i, j: (1, j)),
                  pl.BlockSpec((tm, tn), lambda i, j: (i, j)),
                  pl.BlockSpec((1, 1, tn), lambda i, j: ((i * tm) // t, 0, j))],
        out_specs=pl.BlockSpec((tm, tn), lambda i, j: (i, j)),
        out_shape=jax.ShapeDtypeStruct((m, n), F32),
        compiler_params=_params(("parallel", "arbitrary"),
                                2 * (2 * tm * kh + 2 * kh * tn) * 2 + 5 * tm * tn * 4 + 2 * 2**20,
                                fuse_inputs=[False, False, True, True, False, False]),
        name="outproj",
    )(ma, md, w_out, w_out, x2d, gate)


def _take_top_rows(s, rows_ref, count):
    for lg in range(s.shape[1] // LANES):
        lanes = slice(lg * LANES, (lg + 1) * LANES)
        part = s[:, lanes]
        for r in range(count):
            mx = jnp.max(part, axis=0, keepdims=True)
            rows_ref[r:r + 1, lanes] = mx
            part = jnp.where(part >= mx, -jnp.inf, part)


def _router_kernel(h_ref, wq_ref, keys_ref, thr_ref, s1_ref, a_ref, b_ref, ra_ref, rb_ref, rc_ref):
    k = PEER_TOPK
    sub = keys_ref.shape[2]
    qt = jnp.dot(wq_ref[...], h_ref[...], preferred_element_type=F32)
    s0 = jnp.dot(keys_ref[0, 0], qt[:sub].astype(BF16), preferred_element_type=F32)
    s1 = jnp.dot(keys_ref[0, 1], qt[sub:].astype(BF16), preferred_element_type=F32)
    _take_top_rows(s0, ra_ref, k)
    _take_top_rows(s1, rb_ref, k)
    ra, rb = ra_ref[...], rb_ref[...]
    row8 = lax.broadcasted_iota(jnp.int32, (8, ra.shape[1]), 0)
    pieces = [ra[0:1] + rb, ra[1:2] + rb[0:8]]
    for r in range(2, 8):
        pieces.append(jnp.where(row8 < k // (r + 1), ra[r:r + 1] + rb[0:8], -jnp.inf))
    pieces.append(ra[8:16] + rb[0:1])
    cand = jnp.concatenate(pieces, axis=0)
    _take_top_rows(cand, rc_ref, k)
    tau = rc_ref[k - 1:k, :]
    top = ra[0:1] + rb[0:1]
    z = jnp.sum(jnp.where(cand >= tau, jnp.exp(cand - top), 0.0), axis=0, keepdims=True)
    thr = jnp.full(s0.shape, jnp.inf, F32)
    for r in range(k):
        row_thr = jnp.min(jnp.where(ra[r:r + 1] + rb >= tau, rb, jnp.inf), axis=0, keepdims=True)
        thr = jnp.where(s0 == ra[r:r + 1], row_thr, thr)
    thr_ref[0] = thr
    s1_ref[0] = s1
    a_ref[0] = jnp.exp(s0 - ra[0:1]) / z
    b_ref[0] = jnp.exp(s1 - rb[0:1])


def _router(ht, wq_t, keys, *, tm=1024):
    d, m = ht.shape
    heads, _, n_keys, sub = keys.shape
    tm = min(tm, m)
    big = jax.ShapeDtypeStruct((heads, n_keys, m), F32)
    big_spec = pl.BlockSpec((1, n_keys, tm), lambda i, h: (h, 0, i))
    return pl.pallas_call(
        _router_kernel,
        grid=(m // tm, heads),
        in_specs=[pl.BlockSpec((d, tm), lambda i, h: (0, i)),
                  pl.BlockSpec((2 * sub, d), lambda i, h: (h, 0)),
                  pl.BlockSpec((1, 2, n_keys, sub), lambda i, h: (h, 0, 0, 0))],
        out_specs=[big_spec, big_spec, big_spec, big_spec],
        out_shape=[big, big, big, big],
        scratch_shapes=[pltpu.VMEM((PEER_TOPK, tm), F32)] * 3,
        compiler_params=_params(("parallel", "arbitrary"),
                                2 * d * tm * 2 + 2 * 2 * sub * d * 2 + 8 * n_keys * tm * 4 + 16 * 2**20),
        name="peer_router",
    )(ht, wq_t, keys)


def _expert_table_kernel(x_ref, q_ref, inv_ref, nrm_ref, *, transpose):
    x = x_ref[...]
    amax = jnp.maximum(jnp.max(jnp.abs(x), axis=1, keepdims=True), TINY)
    xs = x * (F8_TARGET / amax)
    q_ref[...] = (xs.T if transpose else xs).astype(q_ref.dtype)
    inv_ref[...] = jnp.broadcast_to(amax * (1.0 / F8_TARGET), inv_ref.shape)
    nrm_ref[...] = jnp.broadcast_to(jnp.sqrt(jnp.sum(x * x, axis=1, keepdims=True)), nrm_ref.shape)


def _expert_table(x, *, transpose, te=256):
    n_exp, d = x.shape
    side = jax.ShapeDtypeStruct((n_exp, LANES), F32)
    side_spec = pl.BlockSpec((te, LANES), lambda e: (e, 0))
    return pl.pallas_call(
        functools.partial(_expert_table_kernel, transpose=transpose),
        grid=(n_exp // te,),
        in_specs=[pl.BlockSpec((te, d), lambda e: (e, 0))],
        out_specs=[pl.BlockSpec((d, te), lambda e: (0, e)) if transpose else pl.BlockSpec((te, d), lambda e: (e, 0)),
                   side_spec, side_spec],
        out_shape=[jax.ShapeDtypeStruct((d, n_exp) if transpose else (n_exp, d), F8), side, side],
        compiler_params=_params(("parallel",), 6 * te * d * 4 + 4 * 2**20),
        name="expert_table_t" if transpose else "expert_table",
    )(x)


def _peer_dense_kernel(h_ref, u_ref, vt_ref, ui_ref, vi_ref, sc_ref, thr_ref, s1_ref, a_ref, b_ref, o_ref,
                       g_sc, acc_sc):
    e = pl.program_id(1)
    heads, n_keys, tm = s1_ref.shape
    te = u_ref.shape[0]
    sub = min(te, PEER_SUBTILE)

    @pl.when(e == 0)
    def _():
        acc_sc[...] = jnp.zeros(acc_sc.shape, F32)

    for ii in range(te // n_keys):
        i = e * (te // n_keys) + ii
        gates = jnp.zeros((n_keys, tm), F32)
        for h in range(heads):
            gates = gates + (jnp.where(s1_ref[h] >= thr_ref[h, pl.ds(i, 1), :], b_ref[h], 0.0)
                             * a_ref[h, pl.ds(i, 1), :])
        g_sc[ii * n_keys:(ii + 1) * n_keys, :] = gates
    nsub = te // sub
    act_scale, w_scale = sc_ref[0:1, :], sc_ref[1:2, :]
    across = lambda col: jnp.concatenate([col] * (tm // LANES), axis=1)
    acts = [jnp.dot(u_ref[t * sub:(t + 1) * sub, :], h_ref[...], preferred_element_type=F32)
            * across(ui_ref[t * sub:(t + 1) * sub, :]) * act_scale for t in range(nsub)]
    for t in range(nsub):
        gel = 0.5 * acts[t] * (1.0 + lax.erf(acts[t] * (1.0 / math.sqrt(2.0))))
        w = (g_sc[t * sub:(t + 1) * sub, :] * gel * across(vi_ref[t * sub:(t + 1) * sub, :])
             * w_scale).astype(vt_ref.dtype)
        acc_sc[...] += jnp.dot(vt_ref[:, t * sub:(t + 1) * sub], w, preferred_element_type=F32)

    @pl.when(e == pl.num_programs(1) - 1)
    def _():
        o_ref[...] = acc_sc[...].astype(o_ref.dtype)


def _peer_dense(ht, u, vt, u_inv, v_inv, scales, thr, s1, a, bf, *, tm=512, te=1024):
    d, m = ht.shape
    n_exp = u.shape[0]
    heads, n_keys, _ = s1.shape
    tm = min(tm, m)
    assert m % tm == 0 and n_exp % te == 0 and te % PEER_SUBTILE == 0 and PEER_SUBTILE % n_keys == 0
    once = pl.Buffered(1)
    tok = lambda i, e: (0, 0, i)
    big_spec = pl.BlockSpec((heads, n_keys, tm), tok, pipeline_mode=once)
    return pl.pallas_call(
        _peer_dense_kernel,
        grid=(m // tm, n_exp // te),
        in_specs=[pl.BlockSpec((d, tm), lambda i, e: (0, i), pipeline_mode=once),
                  pl.BlockSpec((te, d), lambda i, e: (e, 0)),
                  pl.BlockSpec((d, te), lambda i, e: (0, e)),
                  pl.BlockSpec((te, LANES), lambda i, e: (e, 0)),
                  pl.BlockSpec((te, LANES), lambda i, e: (e, 0)),
                  pl.BlockSpec((2, tm), lambda i, e: (0, i), pipeline_mode=once),
                  big_spec, big_spec, big_spec, big_spec],
        out_specs=pl.BlockSpec((d, tm), lambda i, e: (0, i)),
        out_shape=jax.ShapeDtypeStruct((d, m), BF16),
        scratch_shapes=[pltpu.VMEM((te, tm), F32), pltpu.VMEM((d, tm), F32)],
        compiler_params=_params(("parallel", "arbitrary"),
                                d * tm * 2 + 4 * te * d * 2 + 2 * d * tm * 4 + 4 * heads * n_keys * tm * 4
                                + 6 * te * tm * 4 + d * tm * 4),
        name="peer_dense",
    )(ht, u, vt, u_inv, v_inv, scales, thr, s1, a, bf)


def _final_kernel(x_ref, pt_ref, ps_ref, g_ref, gain_ref, o_ref):
    x = x_ref[...] + g_ref[0] * (pt_ref[...].astype(F32) * ps_ref[...]).T
    ms = jnp.mean(x * x, axis=-1, keepdims=True)
    o_ref[...] = x * lax.rsqrt(ms + EPS) * gain_ref[...]


def _final(x1, pt, pscale, gate, gain, *, t, tm=256):
    m, d = x1.shape
    tm = min(tm, t)
    return pl.pallas_call(
        _final_kernel,
        grid=(m // tm,),
        in_specs=[pl.BlockSpec((tm, d), lambda i: (i, 0)),
                  pl.BlockSpec((d, tm), lambda i: (0, i)),
                  pl.BlockSpec((1, tm), lambda i: (0, i)),
                  pl.BlockSpec((1, 1, d), lambda i: ((i * tm) // t, 0, 0)),
                  pl.BlockSpec((1, d), lambda i: (0, 0))],
        out_specs=pl.BlockSpec((tm, d), lambda i: (i, 0)),
        out_shape=jax.ShapeDtypeStruct((m, d), F32),
        compiler_params=_params(("parallel",), 8 * tm * d * 4 + 4 * 2**20),
        name="final_norm",
    )(x1, pt, pscale, gate, gain.reshape(1, d))


def _trunk(x, ada, w, depth_index=0):
    b, t, d = x.shape
    nh = d // (2 * LANES)
    lambda_init = 0.8 - 0.6 * math.exp(-0.3 * depth_index)
    sh1, sc1, g1, sh2, sc2, g2 = [a.reshape(b, 1, d) for a in jnp.split(ada, 6, axis=-1)]

    h = _norm_mod(x, w["norm1_gain"], sc1, sh1, transpose=False)
    p = _inproj(h, w["w_in"], w["colscale"])
    ma = _diff_attention(p, w["diff_lambda"], w["diff_subln_gain"], b=b, t=t, nh=nh, lambda_init=lambda_init)
    md = _dil_attention(p, w["dil_nslopes"], b=b, t=t, nh=nh)
    x1 = _outproj(ma.reshape(b * t, nh * LANES), md.reshape(b * t, nh * LANES), w["w_out"],
                  x.reshape(b * t, d), g1, t=t)

    ht, ht8, stat = _norm_mod(x1.reshape(b, t, d), w["norm2_gain"], sc2, sh2, transpose=True)
    thr, s1, a, bf = _router(ht, w["wq_t"], w["sub_keys"])
    w_scale = F8_TARGET / jnp.maximum(stat[1:2] * w["uv_bound"], TINY)
    pt = _peer_dense(ht8, w["peer_u"], w["peer_vt"], w["u_inv"], w["v_inv"],
                     jnp.concatenate([stat[0:1], w_scale], axis=0), thr, s1, a, bf)
    y = _final(x1, pt, 1.0 / w_scale, g2, w["final_gain"], t=t)
    return y.reshape(b, t, d)


def kernel(x_prompt, x_sample, c_prompt, c_sample, norm1_gain, norm2_gain, w_ada, b_ada, w_in, diff_lambda,
           diff_subln_gain, w_out, peer_w_query, peer_sub_keys, peer_u, peer_v, final_gain):
    d = x_prompt.shape[-1]
    nh = d // (2 * LANES)
    bp, bs = c_prompt.shape[0], c_sample.shape[0]
    assert norm1_gain.shape[0] == 1, "single-layer trunk"
    assert peer_sub_keys.shape[3] == LANES and peer_sub_keys.shape[4] == LANES

    rows = -(-(bp + bs) // 8) * 8
    c_all = jnp.concatenate([c_prompt, c_sample, jnp.zeros((rows - bp - bs, d), F32)], axis=0)
    ada = _ada(c_all, w_ada[0], b_ada[0])

    colscale = np.ones((1, w_in.shape[-1]), np.float32)
    colscale[0, :nh * LANES] = DIFF_HALF_DIM ** -0.5 * math.log2(math.e)
    colscale[0, 3 * nh * LANES:4 * nh * LANES] = LANES ** -0.5 * math.log2(math.e)
    lane_bcast = lambda v: jnp.asarray(np.broadcast_to(np.asarray(v, np.float32)[:, None, None], (nh, 1, LANES)))
    w = dict(
        norm1_gain=norm1_gain[0], norm2_gain=norm2_gain[0], final_gain=final_gain,
        w_in=w_in[0].astype(BF16), colscale=jnp.asarray(colscale),
        dil_nslopes=lane_bcast(-_alibi_slopes(nh) * math.log2(math.e)),
        diff_lambda=diff_lambda[0], diff_subln_gain=diff_subln_gain[0],
        w_out=w_out[0].astype(BF16),
        wq_t=peer_w_query[0].T.astype(BF16), sub_keys=peer_sub_keys[0].astype(BF16),
    )
    peer_u8, u_inv, u_nrm = _expert_table(peer_u[0], transpose=False)
    peer_vt8, v_inv, _ = _expert_table(peer_v[0], transpose=True)
    w.update(peer_u=peer_u8, peer_vt=peer_vt8, u_inv=u_inv, v_inv=v_inv,
             uv_bound=jnp.max(u_nrm[:, 0]) * jnp.max(v_inv[:, 0]))
    y_prompt = _trunk(x_prompt, ada[:bp], w)
    y_sample = _trunk(x_sample, ada[bp:bp + bs], w)
    return (y_prompt, y_sample)
```
